```python
import jax
import jax.numpy as jnp
from jax import lax
import numpy as np

D_MODEL = 1024
BATCH = 8
SEQ = 4096
DEPTH = 2

GRID_W = 64
CTX_LEN = 256
CHUNK = 128
A_GROUP_DIM = 128
A_GROUPS = D_MODEL // A_GROUP_DIM
D_A = A_GROUPS * A_GROUP_DIM
R_HEADS = 4
R_QK_DIM = D_MODEL // 8
R_V_DIM = D_MODEL // 4
D_RQK = R_HEADS * R_QK_DIM
D_RV = R_HEADS * R_V_DIM
D_IN = 2 * D_A + 2 * D_RQK + 2 * D_RV + 2 * D_MODEL
KV_START = 2 * D_A + D_RQK
KV_END = KV_START + D_RQK + D_RV
ROPE_BASE = 10000.0
MOE_GROUPS = 4
EXPERTS_PER_GROUP = 8
N_EXPERTS = MOE_GROUPS * EXPERTS_PER_GROUP
TOP_K = 2
EXPERT_FF = D_MODEL // 2
MOE_BLOCK = 128
EPS = 1e-6

kernel_name = 'hybrid_sgu_retention_hmoe_dit'


def rmsnorm(x, g):
    xf = x.astype(jnp.float32)
    y = xf * lax.rsqrt(jnp.mean(xf * xf, axis=-1, keepdims=True) + EPS)
    return (y * g.astype(jnp.float32)).astype(x.dtype)


def layernorm(x):
    xf = x.astype(jnp.float32)
    mu = jnp.mean(xf, axis=-1, keepdims=True)
    var = jnp.mean(jnp.square(xf - mu), axis=-1, keepdims=True)
    return ((xf - mu) * lax.rsqrt(var + EPS)).astype(x.dtype)


def adaln(cond, w_ada, b_ada):
    m = jax.nn.silu(cond) @ w_ada + b_ada
    return jnp.split(m[..., None, :], 6, axis=-1)


def modulate(h, shift, scale):
    return h * (1.0 + scale) + shift


def to_heads(t, n_heads):
    b, n, _ = t.shape
    return t.reshape(b, n, n_heads, -1).transpose(0, 2, 1, 3)


def grid_rotary(rows):
    n_freq = R_QK_DIM // 4
    inv = ROPE_BASE ** (-jnp.arange(n_freq, dtype=jnp.float32) / n_freq)
    row = jnp.repeat(jnp.arange(rows, dtype=jnp.float32), GRID_W)
    col = jnp.tile(jnp.arange(GRID_W, dtype=jnp.float32), rows)
    ang = jnp.concatenate([row[:, None] * inv, col[:, None] * inv], axis=-1)
    return jnp.cos(ang), jnp.sin(ang)


def apply_rotary(t, cos, sin):
    t1, t2 = t[..., 0::2], t[..., 1::2]
    r = jnp.stack([t1 * cos - t2 * sin, t1 * sin + t2 * cos], axis=-1)
    return r.reshape(t.shape).astype(t.dtype)


def retention_direction(q, k, v, log_g, s0, inclusive):
    b, h, n, dk = q.shape
    dv = v.shape[-1]
    nc = n // CHUNK
    qc = q.reshape(b, h, nc, CHUNK, dk)
    kc = k.reshape(b, h, nc, CHUNK, dk)
    vc = v.reshape(b, h, nc, CHUNK, dv)
    j = jnp.arange(CHUNK, dtype=jnp.float32)
    diff = j[:, None] - j[None, :]
    keep = (diff >= 0) if inclusive else (diff > 0)
    d_intra = jnp.where(keep[None], jnp.exp(log_g[:, None, None] * jnp.maximum(diff, 0.0)[None]), 0.0)
    q_dec = jnp.exp(log_g[:, None] * (j + 1.0))
    k_dec = jnp.exp(log_g[:, None] * (CHUNK - 1.0 - j))
    c_dec = jnp.exp(log_g * CHUNK)[None, :, None, None]
    scores = jnp.einsum('bhncd,bhnmd->bhncm', qc, kc) * d_intra[None, :, None]
    intra = jnp.einsum('bhncm,bhnme->bhnce', scores, vc)
    kv = jnp.einsum('bhncd,bhnce->nbhde', kc * k_dec[None, :, None, :, None], vc)

    def step(s, kv_i):
        return s * c_dec + kv_i, s

    _, s_prev = lax.scan(step, s0, kv)
    cross = jnp.einsum('bhncd,nbhde->bhnce', qc * q_dec[None, :, None, :, None], s_prev)
    return (intra + cross).reshape(b, h, n, dv)


def retention_bidirectional(q, k, v, log_gf, log_gb, s0_f, s0_b):
    flip = lambda t: jnp.flip(t, axis=2)
    o_f = retention_direction(q, k, v, log_gf, s0_f, True)
    o_b = retention_direction(flip(q), flip(k), flip(v), log_gb, s0_b, False)
    return o_f + flip(o_b)


def context_state(k, v, log_g, reverse):
    n = k.shape[2]
    m = jnp.arange(n, dtype=jnp.float32)
    expo = m if reverse else (n - 1.0) - m
    w = jnp.exp(log_g[:, None] * expo[None, :])
    return jnp.einsum('bhmd,bhme,hm->bhde', k, v, w)


def spatial_gating(u, v, w_s, b_s, g_sgu):
    b, n, _ = u.shape
    nc = n // CHUNK
    vn = (layernorm(v) * g_sgu).reshape(b, nc, CHUNK, A_GROUPS, A_GROUP_DIM)
    s = jnp.einsum('gpq,bnqgc->bnpgc', w_s, vn) + b_s.T[None, None, :, :, None]
    return u * s.reshape(b, n, D_A)


def split_projection(p):
    sizes = (D_A, D_A, D_RQK, D_RQK, D_RV, D_RV, D_MODEL, D_MODEL)
    cuts = [int(s) for s in np.cumsum(sizes)[:-1]]
    return jnp.split(p, cuts, axis=-1)


def token_mix(p, s0_f, s0_b, log_gf, log_gb, rotary, w_s, b_s, g_sgu, w_pa, w_pb, w_out):
    u, va, q, k, vr, gr, ga, gb = split_projection(p)
    y_a = spatial_gating(jax.nn.gelu(u), jax.nn.gelu(va), w_s, b_s, g_sgu)
    qh = to_heads(q, R_HEADS).astype(jnp.float32)
    kh = to_heads(k, R_HEADS).astype(jnp.float32) * (R_QK_DIM ** -0.5)
    vh = to_heads(vr, R_HEADS).astype(jnp.float32)
    if rotary is not None:
        cos, sin = rotary
        qh = apply_rotary(qh, cos, sin)
        kh = apply_rotary(kh, cos, sin)
    o = retention_bidirectional(qh, kh, vh, log_gf, log_gb, s0_f, s0_b)
    o = layernorm(o).transpose(0, 2, 1, 3).reshape(p.shape[0], p.shape[1], D_RV).astype(p.dtype)
    y_b = jax.nn.silu(gr) * o
    merged = jax.nn.sigmoid(ga) * (y_a @ w_pa) + jax.nn.sigmoid(gb) * (y_b @ w_pb)
    return merged @ w_out


def expert_dispatch(h, expert_id, weights, w1, w3, w2):
    t, d = h.shape
    a = t * TOP_K
    e_flat = expert_id.reshape(a)
    tok_flat = jnp.repeat(jnp.arange(t, dtype=jnp.int32), TOP_K)
    w_flat = weights.reshape(a)
    order = jnp.argsort(e_flat)
    e_sorted, tok_sorted, w_sorted = e_flat[order], tok_flat[order], w_flat[order]
    counts = jnp.bincount(e_flat, length=N_EXPERTS)
    padded = (counts + MOE_BLOCK - 1) // MOE_BLOCK * MOE_BLOCK
    start = jnp.cumsum(counts) - counts
    pend = jnp.cumsum(padded)
    dest = (pend - padded)[e_sorted] + (jnp.arange(a) - start[e_sorted])
    n_blocks = (a + N_EXPERTS * (MOE_BLOCK - 1) + MOE_BLOCK - 1) // MOE_BLOCK
    p = n_blocks * MOE_BLOCK
    buf_tok = jnp.full((p,), t, dtype=jnp.int32).at[dest].set(tok_sorted)
    buf_w = jnp.zeros((p,), h.dtype).at[dest].set(w_sorted.astype(h.dtype))
    blk_e = jnp.minimum(jnp.searchsorted(pend, jnp.arange(n_blocks) * MOE_BLOCK, side='right'), N_EXPERTS - 1)
    h_pad = jnp.concatenate([h, jnp.zeros((1, d), h.dtype)], axis=0)
    xb = h_pad[buf_tok].reshape(n_blocks, MOE_BLOCK, d)

    def expert_block(args):
        xblk, e = args
        return (jax.nn.silu(xblk @ w1[e]) * (xblk @ w3[e])) @ w2[e]

    yb = lax.map(expert_block, (xb, blk_e))
    y = yb.reshape(p, d) * buf_w[:, None]
    return jax.ops.segment_sum(y, buf_tok, num_segments=t + 1)[:t]


def hier_moe(h, w_group, b_group, w_erouter, b_erouter, w1, w3, w2):
    t = h.shape[0]
    hf = h.astype(jnp.float32)
    g_logits = hf @ w_group.astype(jnp.float32) + b_group.astype(jnp.float32)
    g_top, g_sel = lax.top_k(g_logits, 1)
    p_group = jnp.exp(g_top - jax.nn.logsumexp(g_logits, axis=-1, keepdims=True))
    e_logits = (hf @ w_erouter.astype(jnp.float32) + b_erouter.astype(jnp.float32)).reshape(t, MOE_GROUPS, EXPERTS_PER_GROUP)
    idx = jnp.broadcast_to(g_sel[:, :, None], (t, 1, EXPERTS_PER_GROUP))
    e_in_group = jnp.take_along_axis(e_logits, idx, axis=1)[:, 0]
    e_top, e_sel = lax.top_k(e_in_group, TOP_K)
    weights = p_group * jax.nn.softmax(e_top, axis=-1)
    expert_id = g_sel * EXPERTS_PER_GROUP + e_sel
    return expert_dispatch(h, expert_id, weights, w1, w3, w2)


def hybrid_layer(x, xc, c, c_ctx, w_ada, b_ada, g_mix, g_ffn, w_in, w_s, b_s, g_sgu, decay_logit,
                 w_pa, w_pb, w_out, w_group, b_group, w_erouter, b_erouter, w1, w3, w2, rotary, update_ctx):
    b, n, d = x.shape
    l_ctx = xc.shape[1]
    sh1, sc1, gt1, sh2, sc2, gt2 = adaln(c, w_ada, b_ada)
    sh1c, sc1c, gt1c, sh2c, sc2c, gt2c = adaln(c_ctx, w_ada, b_ada)
    log_g = jax.nn.log_sigmoid(decay_logit.astype(jnp.float32))
    log_gf, log_gb = log_g[0], log_g[1]
    mix_w = (w_s, b_s, g_sgu, w_pa, w_pb, w_out)
    h = modulate(rmsnorm(x, g_mix), sh1, sc1)
    hc = modulate(rmsnorm(xc, g_mix), sh1c, sc1c)
    if update_ctx:
        pc = hc @ w_in
        kv_c = pc[..., KV_START:KV_END]
    else:
        kv_c = hc @ w_in[:, KV_START:KV_END]
    kh_c = to_heads(kv_c[..., :D_RQK], R_HEADS).astype(jnp.float32) * (R_QK_DIM ** -0.5)
    vh_c = to_heads(kv_c[..., D_RQK:], R_HEADS).astype(jnp.float32)
    s_ctx_f = context_state(kh_c, vh_c, log_gf, False)
    s_ctx_b = context_state(kh_c, vh_c, log_gb, True)
    x = x + gt1 * token_mix(h @ w_in, s_ctx_f, s_ctx_b, log_gf, log_gb, rotary, *mix_w)
    if update_ctx:
        zero_state = jnp.zeros_like(s_ctx_f)
        xc = xc + gt1c * token_mix(pc, zero_state, zero_state, log_gf, log_gb, None, *mix_w)
    h2 = modulate(rmsnorm(x, g_ffn), sh2, sc2).reshape(b * n, d)
    if update_ctx:
        h2c = modulate(rmsnorm(xc, g_ffn), sh2c, sc2c).reshape(b * l_ctx, d)
        y = hier_moe(jnp.concatenate([h2, h2c], axis=0), w_group, b_group, w_erouter, b_erouter, w1, w3, w2)
        x = x + gt2 * y[:b * n].reshape(b, n, d)
        xc = xc + gt2c * y[b * n:].reshape(b, l_ctx, d)
    else:
        y = hier_moe(h2, w_group, b_group, w_erouter, b_erouter, w1, w3, w2)
        x = x + gt2 * y.reshape(b, n, d)
    return x, xc


def setup_inputs(seed: int = 0) -> dict:
    key = jax.random.key(seed)
    ks = jax.random.split(key, 24)
    f32 = jnp.float32
    d = D_MODEL

    def nrm(k, shape, s):
        return jax.random.normal(k, shape, f32) * s

    base_logit = jnp.log(2.0 ** (5.0 + jnp.arange(R_HEADS, dtype=f32)) - 1.0)
    return {
        'x': nrm(ks[0], (BATCH, SEQ, d), 1.0),
        'c': nrm(ks[1], (BATCH, d), 1.0),
        'ctx': nrm(ks[2], (BATCH, CTX_LEN, d), 1.0),
        'c_ctx': nrm(ks[3], (d,), 1.0),
        'w_ada': nrm(ks[4], (DEPTH, d, 6 * d), 0.5 * d ** -0.5),
        'b_ada': nrm(ks[5], (DEPTH, 6 * d), 0.02),
        'g_mix': 1.0 + nrm(ks[6], (DEPTH, d), 0.02),
        'g_ffn': 1.0 + nrm(ks[7], (DEPTH, d), 0.02),
        'w_in': nrm(ks[8], (DEPTH, d, D_IN), d ** -0.5),
        'w_s': nrm(ks[9], (DEPTH, A_GROUPS, CHUNK, CHUNK), CHUNK ** -0.5),
        'b_s': 1.0 + nrm(ks[10], (DEPTH, A_GROUPS, CHUNK), 0.02),
        'g_sgu': 1.0 + nrm(ks[11], (DEPTH, D_A), 0.02),
        'decay_logit': base_logit + nrm(ks[12], (DEPTH, 2, R_HEADS), 0.1),
        'w_pa': nrm(ks[13], (DEPTH, D_A, d), D_A ** -0.5),
        'w_pb': nrm(ks[14], (DEPTH, D_RV, d), D_RV ** -0.5),
        'w_out': nrm(ks[15], (DEPTH, d, d), d ** -0.5),
        'w_group': nrm(ks[16], (DEPTH, d, MOE_GROUPS), d ** -0.5),
        'b_group': nrm(ks[17], (DEPTH, MOE_GROUPS), 0.01),
        'w_erouter': nrm(ks[18], (DEPTH, d, N_EXPERTS), d ** -0.5),
        'b_erouter': nrm(ks[19], (DEPTH, N_EXPERTS), 0.01),
        'w1': nrm(ks[20], (DEPTH, N_EXPERTS, d, EXPERT_FF), d ** -0.5),
        'w3': nrm(ks[21], (DEPTH, N_EXPERTS, d, EXPERT_FF), d ** -0.5),
        'w2': nrm(ks[22], (DEPTH, N_EXPERTS, EXPERT_FF, d), EXPERT_FF ** -0.5),
        'g_final': 1.0 + nrm(ks[23], (d,), 0.02),
    }


def reference(x, c, ctx, c_ctx, w_ada, b_ada, g_mix, g_ffn, w_in, w_s, b_s, g_sgu, decay_logit,
              w_pa, w_pb, w_out, w_group, b_group, w_erouter, b_erouter, w1, w3, w2, g_final):
    rows = x.shape[1] // GRID_W
    rotary = grid_rotary(rows)
    xc = ctx
    for l in range(DEPTH):
        x, xc = hybrid_layer(x, xc, c, c_ctx, w_ada[l], b_ada[l], g_mix[l], g_ffn[l], w_in[l], w_s[l], b_s[l],
                             g_sgu[l], decay_logit[l], w_pa[l], w_pb[l], w_out[l], w_group[l], b_group[l],
                             w_erouter[l], b_erouter[l], w1[l], w3[l], w2[l], rotary, l < DEPTH - 1)
    return rmsnorm(x, g_final)
```

```python
import functools

import numpy as np
import jax
import jax.numpy as jnp
from jax import lax
from jax.experimental import pallas as pl
from jax.experimental.pallas import tpu as pltpu

D_MODEL = 1024
DEPTH = 2
GRID_W = 64
CHUNK = 128
A_GROUP_DIM = 128
A_GROUPS = D_MODEL // A_GROUP_DIM
D_A = A_GROUPS * A_GROUP_DIM
R_HEADS = 4
R_QK_DIM = D_MODEL // 8
R_V_DIM = D_MODEL // 4
D_RQK = R_HEADS * R_QK_DIM
D_RV = R_HEADS * R_V_DIM
D_IN = 2 * D_A + 2 * D_RQK + 2 * D_RV + 2 * D_MODEL
KV_START = 2 * D_A + D_RQK
KV_END = KV_START + D_RQK + D_RV
ROPE_BASE = 10000.0
MOE_GROUPS = 4
EXPERTS_PER_GROUP = 8
N_EXPERTS = MOE_GROUPS * EXPERTS_PER_GROUP
EXPERT_FF = D_MODEL // 2
EPS = 1e-6

LANES = 128
SUBLANES = 8
MOE_ROWS = 256
VMEM_LIMIT = 56 * 1024 * 1024
COND_ROWS = 16

_U_BLK, _VA_BLK = 0, 1
_Q_BLK, _K_BLK = 2 * D_A // D_RQK, 2 * D_A // D_RQK + 1
_VR_BLK = (2 * D_A + 2 * D_RQK) // D_RV
_GR_BLK, _GA_BLK, _GB_BLK = _VR_BLK + 1, _VR_BLK + 2, _VR_BLK + 3

_F32 = jnp.float32
_BF16 = jnp.bfloat16


def _cparams(*sem):
    return pltpu.CompilerParams(dimension_semantics=sem, vmem_limit_bytes=VMEM_LIMIT)


def _dot(a, b):
    return jnp.dot(a, b, preferred_element_type=_F32)


def _adaln_kernel(cond_ref, w_ref, b_ref, o_ref):
    cnd = cond_ref[...]
    s = cnd * jax.nn.sigmoid(cnd)
    o_ref[0] = jnp.dot(s, w_ref[0], preferred_element_type=_F32, precision=lax.Precision.HIGHEST) + b_ref[0]


def _adaln(cond, w_ada, b_ada):
    depth, d, d6 = w_ada.shape
    tn = 1024
    return pl.pallas_call(
        _adaln_kernel,
        grid=(depth, d6 // tn),
        in_specs=[
            pl.BlockSpec((COND_ROWS, d), lambda l, j: (0, 0)),
            pl.BlockSpec((1, d, tn), lambda l, j: (l, 0, j)),
            pl.BlockSpec((1, 1, tn), lambda l, j: (l, 0, j)),
        ],
        out_specs=pl.BlockSpec((1, COND_ROWS, tn), lambda l, j: (l, 0, j)),
        out_shape=jax.ShapeDtypeStruct((depth, COND_ROWS, d6), _F32),
        compiler_params=_cparams("parallel", "parallel"),
        name="adaln",
    )(cond, w_ada, b_ada.reshape(depth, 1, d6))


def _modnorm(x, g, shift, scale):
    y = x * lax.rsqrt(jnp.mean(x * x, axis=-1, keepdims=True) + EPS)
    return (y * g) * (1.0 + scale) + shift


def _proj_kernel(x_ref, g_ref, sh_ref, sc_ref, w_ref, o_ref, h_ref):
    @pl.when(pl.program_id(2) == 0)
    def _():
        h_ref[...] = _modnorm(x_ref[0], g_ref[...], sh_ref[0], sc_ref[0]).astype(_BF16)

    o_ref[0] = _dot(h_ref[...], w_ref[...]).astype(_BF16)


def _proj_in(x, g, shift, scale, w):
    b, n, d = x.shape
    dout = w.shape[1]
    tm = min(n, 1024)
    tn = 1024 if dout % 1024 == 0 else 512
    return pl.pallas_call(
        _proj_kernel,
        grid=(b, n // tm, dout // tn),
        in_specs=[
            pl.BlockSpec((1, tm, d), lambda i, r, j: (i, r, 0)),
            pl.BlockSpec((1, d), lambda i, r, j: (0, 0)),
            pl.BlockSpec((1, 1, d), lambda i, r, j: (i, 0, 0)),
            pl.BlockSpec((1, 1, d), lambda i, r, j: (i, 0, 0)),
            pl.BlockSpec((d, tn), lambda i, r, j: (0, j)),
        ],
        out_specs=pl.BlockSpec((1, tm, tn), lambda i, r, j: (i, r, j)),
        out_shape=jax.ShapeDtypeStruct((b, n, dout), _BF16),
        scratch_shapes=[pltpu.VMEM((tm, d), _BF16)],
        compiler_params=_cparams("parallel", "parallel", "arbitrary"),
        name="proj_in",
    )(x, g.reshape(1, d), shift, scale, w)


def _rotate(t, cos2, sin2):
    return t * cos2 + pltpu.roll(t, R_QK_DIM // 2, 1) * sin2


def _dot_t(a, b):
    return lax.dot_general(a, b, (((0,), (0,)), ((), ())), preferred_element_type=_F32)


def _ctx_state_kernel(k_ref, v_ref, wf_ref, wb_ref, sf_ref, sb_ref):
    for h in range(R_HEADS):
        kh = k_ref[0, :, h * R_QK_DIM:(h + 1) * R_QK_DIM].astype(_F32) * (R_QK_DIM ** -0.5)
        vh = v_ref[0, :, h * R_V_DIM:(h + 1) * R_V_DIM]
        sf_ref[0, h] = _dot_t((kh * wf_ref[h]).astype(_BF16), vh)
        sb_ref[0, h] = _dot_t((kh * wb_ref[h]).astype(_BF16), vh)


def _ctx_states(k, v, wf, wb):
    b, l, _ = k.shape
    st = jax.ShapeDtypeStruct((b, R_HEADS, R_QK_DIM, R_V_DIM), _F32)
    st_spec = pl.BlockSpec((1, R_HEADS, R_QK_DIM, R_V_DIM), lambda i: (i, 0, 0, 0))
    return pl.pallas_call(
        _ctx_state_kernel,
        grid=(b,),
        in_specs=[
            pl.BlockSpec((1, l, D_RQK), lambda i: (i, 0, 0)),
            pl.BlockSpec((1, l, D_RV), lambda i: (i, 0, 0)),
            pl.BlockSpec((R_HEADS, l, LANES), lambda i: (0, 0, 0)),
            pl.BlockSpec((R_HEADS, l, LANES), lambda i: (0, 0, 0)),
        ],
        out_specs=[st_spec, st_spec],
        out_shape=[st, st],
        compiler_params=_cparams("parallel"),
        name="ctx_states",
    )(k, v, wf, wb)


def _bwd_state_kernel(cdec_ref, k_ref, v_ref, cos_ref, sin_ref, kdb_ref, s0_ref, o_ref, s_ref):
    @pl.when(pl.program_id(1) == 0)
    def _():
        s_ref[...] = s0_ref[0]

    cos2, sin2 = cos_ref[...], sin_ref[...]
    for h in range(R_HEADS):
        s = s_ref[h]
        o_ref[0, 0, h] = s.astype(_BF16)
        kh = _rotate(k_ref[0, :, h * R_QK_DIM:(h + 1) * R_QK_DIM].astype(_F32), cos2, sin2) * (R_QK_DIM ** -0.5)
        vh = v_ref[0, :, h * R_V_DIM:(h + 1) * R_V_DIM]
        s_ref[h] = s * cdec_ref[1, h] + _dot_t((kh * kdb_ref[h]).astype(_BF16), vh)


def _bwd_states(p, cos2, sin2, kdb, cdec, s0_b):
    b, n, _ = p.shape
    nc = n // CHUNK
    rev = lambda i, c: nc - 1 - c
    return pl.pallas_call(
        _bwd_state_kernel,
        grid=(b, nc),
        in_specs=[
            pl.BlockSpec(memory_space=pltpu.SMEM),
            pl.BlockSpec((1, CHUNK, D_RQK), lambda i, c: (i, rev(i, c), _K_BLK)),
            pl.BlockSpec((1, CHUNK, D_RV), lambda i, c: (i, rev(i, c), _VR_BLK)),
            pl.BlockSpec((CHUNK, LANES), lambda i, c: (rev(i, c), 0)),
            pl.BlockSpec((CHUNK, LANES), lambda i, c: (rev(i, c), 0)),
            pl.BlockSpec((R_HEADS, CHUNK, LANES), lambda i, c: (0, 0, 0)),
            pl.BlockSpec((1, R_HEADS, R_QK_DIM, R_V_DIM), lambda i, c: (i, 0, 0, 0)),
        ],
        out_specs=pl.BlockSpec((1, 1, R_HEADS, R_QK_DIM, R_V_DIM), lambda i, c: (i, rev(i, c), 0, 0, 0)),
        out_shape=jax.ShapeDtypeStruct((b, nc, R_HEADS, R_QK_DIM, R_V_DIM), _BF16),
        scratch_shapes=[pltpu.VMEM((R_HEADS, R_QK_DIM, R_V_DIM), _F32)],
        compiler_params=_cparams("parallel", "arbitrary"),
        name="bwd_states",
    )(cdec, p, p, cos2, sin2, kdb, s0_b)


def _layernorm(t):
    mu = jnp.mean(t, axis=-1, keepdims=True)
    tc = t - mu
    return tc * lax.rsqrt(jnp.mean(tc * tc, axis=-1, keepdims=True) + EPS)


def _mix_kernel(n_chunks, cdec_ref, u_ref, va_ref, q_ref, k_ref, vr_ref, gr_ref, ga_ref, gb_ref, x_ref, gt_ref,
                cos_ref, sin_ref, dmat_ref, qdf_ref, qdb_ref, kdf_ref, sb_ref, s0_ref,
                ws_ref, bs_ref, gsgu_ref, wpa_ref, wpb_ref, wout_ref, o_ref, sf_ref, ya_ref, yb_ref):
    @pl.when(pl.program_id(1) == 0)
    def _():
        sf_ref[...] = s0_ref[0]

    def chunk(c, carry):
        r0 = pl.multiple_of(c * CHUNK, CHUNK)
        rows = pl.ds(r0, CHUNK)
        gu = jax.nn.gelu(u_ref[0, rows, :].astype(_F32))
        vn = (_layernorm(jax.nn.gelu(va_ref[0, rows, :].astype(_F32))) * gsgu_ref[...]).astype(_BF16)
        for g in range(A_GROUPS):
            cols = slice(g * A_GROUP_DIM, (g + 1) * A_GROUP_DIM)
            s = _dot(ws_ref[g], vn[:, cols]) + bs_ref[g]
            ya_ref[rows, cols] = (gu[:, cols] * s).astype(_BF16)
        cos2, sin2 = cos_ref[rows, :], sin_ref[rows, :]
        for h in range(R_HEADS):
            qk = slice(h * R_QK_DIM, (h + 1) * R_QK_DIM)
            vv = slice(h * R_V_DIM, (h + 1) * R_V_DIM)
            qh = _rotate(q_ref[0, rows, qk].astype(_F32), cos2, sin2)
            kh = _rotate(k_ref[0, rows, qk].astype(_F32), cos2, sin2) * (R_QK_DIM ** -0.5)
            vh = vr_ref[0, rows, vv]
            scores = lax.dot_general(qh.astype(_BF16), kh.astype(_BF16), (((1,), (1,)), ((), ())),
                                     preferred_element_type=_F32) * dmat_ref[h]
            o = _dot(scores.astype(_BF16), vh)
            s_f = sf_ref[h]
            o = o + _dot((qh * qdf_ref[h]).astype(_BF16), s_f.astype(_BF16))
            o = o + _dot((qh * qdb_ref[h]).astype(_BF16), sb_ref[0, c, h])
            gr = gr_ref[0, rows, vv].astype(_F32)
            yb_ref[rows, vv] = (gr * jax.nn.sigmoid(gr) * _layernorm(o)).astype(_BF16)
            sf_ref[h] = s_f * cdec_ref[0, h] + _dot_t((kh * kdf_ref[h]).astype(_BF16), vh)
        return carry

    lax.fori_loop(0, n_chunks, chunk, 0)
    a = _dot(ya_ref[...], wpa_ref[...])
    b = _dot(yb_ref[...], wpb_ref[...])
    merged = jax.nn.sigmoid(ga_ref[0].astype(_F32)) * a + jax.nn.sigmoid(gb_ref[0].astype(_F32)) * b
    o_ref[0] = x_ref[0] + gt_ref[0] * _dot(merged.astype(_BF16), wout_ref[...])


def _token_mix(p, x, gt, cos2, sin2, tabs, sb, s0_f, ws, bsb, gsgu, wpa, wpb, wout):
    b, n, d = x.shape
    rows = min(n, 512)
    ncb = rows // CHUNK
    pspec = lambda width, blk: pl.BlockSpec((1, rows, width), lambda i, r: (i, r, blk))
    const = lambda shape: pl.BlockSpec(shape, lambda i, r: (0,) * len(shape))
    tab = const((R_HEADS, CHUNK, LANES))
    return pl.pallas_call(
        functools.partial(_mix_kernel, ncb),
        grid=(b, n // rows),
        in_specs=[
            pl.BlockSpec(memory_space=pltpu.SMEM),
            pspec(D_A, _U_BLK), pspec(D_A, _VA_BLK), pspec(D_RQK, _Q_BLK), pspec(D_RQK, _K_BLK),
            pspec(D_RV, _VR_BLK), pspec(D_RV, _GR_BLK), pspec(d, _GA_BLK), pspec(d, _GB_BLK),
            pl.BlockSpec((1, rows, d), lambda i, r: (i, r, 0)),
            pl.BlockSpec((1, 1, d), lambda i, r: (i, 0, 0)),
            pl.BlockSpec((rows, LANES), lambda i, r: (r, 0)),
            pl.BlockSpec((rows, LANES), lambda i, r: (r, 0)),
            tab, tab, tab, tab,
            pl.BlockSpec((1, ncb, R_HEADS, R_QK_DIM, R_V_DIM), lambda i, r: (i, r, 0, 0, 0)),
            pl.BlockSpec((1, R_HEADS, R_QK_DIM, R_V_DIM), lambda i, r: (i, 0, 0, 0)),
            const((A_GROUPS, CHUNK, CHUNK)), const((A_GROUPS, CHUNK, LANES)), const((1, D_A)),
            const((D_A, d)), const((D_RV, d)), const((d, d)),
        ],
        out_specs=pl.BlockSpec((1, rows, d), lambda i, r: (i, r, 0)),
        out_shape=jax.ShapeDtypeStruct((b, n, d), _F32),
        scratch_shapes=[
            pltpu.VMEM((R_HEADS, R_QK_DIM, R_V_DIM), _F32),
            pltpu.VMEM((rows, D_A), _BF16),
            pltpu.VMEM((rows, D_RV), _BF16),
        ],
        compiler_params=_cparams("parallel", "arbitrary"),
        name="token_mix",
    )(tabs["cdec"], p, p, p, p, p, p, p, p, x, gt, cos2, sin2, tabs["dmat"], tabs["qdf"], tabs["qdb"], tabs["kdf"],
      sb, s0_f, ws, bsb, gsgu, wpa, wpb, wout)


ROUTE_COLS = LANES


def _router_kernel(x_ref, g_ref, sh_ref, sc_ref, wr_ref, br_ref, h_ref, id_ref, wt_ref):
    h = _modnorm(x_ref[...], g_ref[...], sh_ref[0], sc_ref[0])
    h_ref[...] = h
    logits = jnp.dot(h, wr_ref[...], preferred_element_type=_F32, precision=lax.Precision.HIGHEST) + br_ref[...]
    lane = lax.broadcasted_iota(jnp.int32, logits.shape, 1)
    neg = -jnp.inf
    gl = jnp.where(lane < MOE_GROUPS, logits, neg)
    g_top = jnp.max(gl, axis=-1, keepdims=True)
    g_sel = jnp.min(jnp.where(gl == g_top, lane, ROUTE_COLS), axis=-1, keepdims=True)
    lse = g_top + jnp.log(jnp.sum(jnp.exp(gl - g_top), axis=-1, keepdims=True))
    p_group = jnp.exp(g_top - lse)
    lo = MOE_GROUPS + g_sel * EXPERTS_PER_GROUP
    el = jnp.where((lane >= lo) & (lane < lo + EXPERTS_PER_GROUP), logits, neg)
    e1 = jnp.max(el, axis=-1, keepdims=True)
    i1 = jnp.min(jnp.where(el == e1, lane, ROUTE_COLS), axis=-1, keepdims=True)
    el2 = jnp.where(lane == i1, neg, el)
    e2 = jnp.max(el2, axis=-1, keepdims=True)
    i2 = jnp.min(jnp.where(el2 == e2, lane, ROUTE_COLS), axis=-1, keepdims=True)
    x2 = jnp.exp(e2 - e1)
    den = 1.0 + x2
    id_ref[...] = jnp.where(lane == 0, i1 - MOE_GROUPS, jnp.where(lane == 1, i2 - MOE_GROUPS, 0))
    wt_ref[...] = jnp.where(lane == 0, p_group * (1.0 / den), jnp.where(lane == 1, p_group * (x2 / den), 0.0))


def _router(x, g, shift, scale, wr, br):
    b, n, d = x.shape
    t = b * n
    tm = min(n, 512)
    per_b = n // tm
    row = pl.BlockSpec((tm, d), lambda i: (i, 0))
    mod = pl.BlockSpec((1, 1, d), lambda i: (i // per_b, 0, 0))
    lane_out = pl.BlockSpec((tm, ROUTE_COLS), lambda i: (i, 0))
    return pl.pallas_call(
        _router_kernel,
        grid=(t // tm,),
        in_specs=[row, pl.BlockSpec((1, d), lambda i: (0, 0)), mod, mod,
                  pl.BlockSpec((d, ROUTE_COLS), lambda i: (0, 0)), pl.BlockSpec((1, ROUTE_COLS), lambda i: (0, 0))],
        out_specs=[row, lane_out, lane_out],
        out_shape=[jax.ShapeDtypeStruct((t, d), _F32), jax.ShapeDtypeStruct((t, ROUTE_COLS), jnp.int32),
                   jax.ShapeDtypeStruct((t, ROUTE_COLS), _F32)],
        compiler_params=_cparams("parallel"),
        name="router",
    )(x.reshape(t, d), g.reshape(1, d), shift, scale, wr, br)


def _expert_kernel(blk_e_ref, n_used_ref, cnt_ref, tok_ref, tok_next_ref, dst_ref, bw_ref, w1_ref, w3_ref, w2_ref,
                   h_hbm, y_hbm, xbuf, ybuf, gsem, ssem):
    i = pl.program_id(0)
    n_used = n_used_ref[0]
    slot = i % 2

    def gather(idx_ref, s):
        def body(r, carry):
            pltpu.make_async_copy(h_hbm.at[pl.ds(idx_ref[0, 0, r], 1), :], xbuf.at[s, pl.ds(r, 1), :],
                                  gsem.at[s]).start()
            return carry
        lax.fori_loop(0, MOE_ROWS, body, 0)

    def wait_gather(s):
        pltpu.make_async_copy(xbuf.at[s], xbuf.at[s], gsem.at[s]).wait()

    def wait_scatter(s, rows):
        def wait_rows(n_rows):
            pltpu.make_async_copy(ybuf.at[s, pl.ds(0, n_rows), :], ybuf.at[s, pl.ds(0, n_rows), :], ssem.at[s]).wait()

        tiles = rows // SUBLANES

        @pl.when(tiles > 0)
        def _():
            wait_rows(pl.multiple_of(tiles * SUBLANES, SUBLANES))

        for part in (4, 2, 1):
            @pl.when((rows & part) != 0)
            def _():
                wait_rows(part)

    @pl.when(i == 0)
    def _():
        gather(tok_ref, 0)

    @pl.when(i + 1 < n_used)
    def _():
        gather(tok_next_ref, 1 - slot)

    @pl.when(i < n_used)
    def _():
        wait_gather(slot)

        @pl.when(i >= 2)
        def _():
            wait_scatter(slot, cnt_ref[jnp.maximum(i - 2, 0)])

        xb = xbuf[slot].astype(_BF16)
        h1 = _dot(xb, w1_ref[0])
        h3 = _dot(xb, w3_ref[0])
        act = (h1 * jax.nn.sigmoid(h1) * h3).astype(_BF16)
        ybuf[slot] = _dot(act, w2_ref[0]) * bw_ref[...]

        def scatter(r, carry):
            pltpu.make_async_copy(ybuf.at[slot, pl.ds(r, 1), :], y_hbm.at[pl.ds(dst_ref[0, 0, r], 1), :],
                                  ssem.at[slot]).start()
            return carry
        lax.fori_loop(0, cnt_ref[i], scatter, 0)

        @pl.when(i == n_used - 1)
        def _():
            wait_scatter(slot, cnt_ref[i])

            @pl.when(i >= 1)
            def _():
                wait_scatter(1 - slot, cnt_ref[jnp.maximum(i - 1, 0)])


def _experts(h2, blk_e, n_used, cnt, tok, dst, buf_w, w1, w3, w2):
    t, d = h2.shape
    n_blocks = tok.shape[0]
    last = n_blocks - 1
    idx_spec = lambda step: pl.BlockSpec((1, 1, MOE_ROWS), lambda i, be, nu, cn: (jnp.minimum(i + step, last), 0, 0),
                                         memory_space=pltpu.SMEM)
    grid_spec = pltpu.PrefetchScalarGridSpec(
        num_scalar_prefetch=3,
        grid=(n_blocks,),
        in_specs=[
            idx_spec(0), idx_spec(1), idx_spec(0),
            pl.BlockSpec((MOE_ROWS, 1), lambda i, be, nu, cn: (i, 0)),
            pl.BlockSpec((1, d, EXPERT_FF), lambda i, be, nu, cn: (be[i], 0, 0)),
            pl.BlockSpec((1, d, EXPERT_FF), lambda i, be, nu, cn: (be[i], 0, 0)),
            pl.BlockSpec((1, EXPERT_FF, d), lambda i, be, nu, cn: (be[i], 0, 0)),
            pl.BlockSpec(memory_space=pl.ANY),
        ],
        out_specs=pl.BlockSpec(memory_space=pl.ANY),
        scratch_shapes=[
            pltpu.VMEM((2, MOE_ROWS, d), _F32),
            pltpu.VMEM((2, MOE_ROWS, d), _F32),
            pltpu.SemaphoreType.DMA((2,)),
            pltpu.SemaphoreType.DMA((2,)),
        ],
    )
    return pl.pallas_call(
        _expert_kernel,
        grid_spec=grid_spec,
        out_shape=jax.ShapeDtypeStruct((2 * t, d), _F32),
        compiler_params=_cparams("arbitrary"),
        name="experts",
    )(blk_e, n_used, cnt, tok, tok, dst, buf_w, w1, w3, w2, h2)


def _dispatch_plan(ids, wts):
    t = ids.shape[0]
    a = 2 * t
    e_flat = ids.reshape(a)
    onehot = (e_flat[:, None] == jnp.arange(N_EXPERTS, dtype=jnp.int32)[None, :]).astype(jnp.int32)
    csum = jnp.cumsum(onehot, axis=0)
    rank = jnp.sum(csum * onehot, axis=1) - 1
    counts = csum[-1]
    padded = (counts + MOE_ROWS - 1) // MOE_ROWS * MOE_ROWS
    pend = jnp.cumsum(padded)
    dest = (pend - padded)[e_flat] + rank
    n_blocks = (a + N_EXPERTS * (MOE_ROWS - 1) + MOE_ROWS - 1) // MOE_ROWS
    p = n_blocks * MOE_ROWS
    tok_flat = jnp.arange(a, dtype=jnp.int32) // 2
    k_flat = jnp.arange(a, dtype=jnp.int32) % 2
    buf_tok = jnp.zeros((p,), jnp.int32).at[dest].set(tok_flat)
    dst = jnp.zeros((p,), jnp.int32).at[dest].set(k_flat * t + tok_flat)
    cnt = jnp.zeros((p,), jnp.int32).at[dest].set(1).reshape(n_blocks, MOE_ROWS).sum(axis=1)
    buf_w = jnp.zeros((p,), _F32).at[dest].set(wts.reshape(a))
    blk_e = jnp.minimum(jnp.searchsorted(pend, jnp.arange(n_blocks, dtype=jnp.int32) * MOE_ROWS, side="right"),
                        N_EXPERTS - 1).astype(jnp.int32)
    n_used = (pend[-1] // MOE_ROWS).astype(jnp.int32).reshape(1)
    shape3 = (n_blocks, 1, MOE_ROWS)
    return blk_e, n_used, cnt, buf_tok.reshape(shape3), dst.reshape(shape3), buf_w.reshape(p, 1)


def _combine_kernel(final, x_ref, y0_ref, y1_ref, gt_ref, g_ref, o_ref):
    x = x_ref[...] + gt_ref[0] * (y0_ref[...] + y1_ref[...])
    if final:
        x = (x * lax.rsqrt(jnp.mean(x * x, axis=-1, keepdims=True) + EPS)) * g_ref[...]
    o_ref[...] = x


def _combine(x, y, gt, g_final, tok0, t_all, final):
    b, n, d = x.shape
    t = b * n
    tm = min(n, 512)
    per_b = n // tm
    off0, off1 = tok0 // tm, (t_all + tok0) // tm
    row = pl.BlockSpec((tm, d), lambda i: (i, 0))
    out = pl.pallas_call(
        functools.partial(_combine_kernel, final),
        grid=(t // tm,),
        in_specs=[row, pl.BlockSpec((tm, d), lambda i: (off0 + i, 0)), pl.BlockSpec((tm, d), lambda i: (off1 + i, 0)),
                  pl.BlockSpec((1, 1, d), lambda i: (i // per_b, 0, 0)), pl.BlockSpec((1, d), lambda i: (0, 0))],
        out_specs=row,
        out_shape=jax.ShapeDtypeStruct((t, d), _F32),
        compiler_params=_cparams("parallel"),
        name="combine",
    )(x.reshape(t, d), y, y, gt, g_final.reshape(1, d))
    return out.reshape(b, n, d)


def _pair_split_perm():
    perm = np.arange(D_IN)
    head = np.concatenate([np.arange(0, R_QK_DIM, 2), np.arange(1, R_QK_DIM, 2)])
    for base in (2 * D_A, 2 * D_A + D_RQK):
        for h in range(R_HEADS):
            perm[base + h * R_QK_DIM: base + (h + 1) * R_QK_DIM] = base + h * R_QK_DIM + head
    return perm


def _rotary_tables(n):
    n_freq = R_QK_DIM // 4
    inv = ROPE_BASE ** (-jnp.arange(n_freq, dtype=_F32) / n_freq)
    rows = n // GRID_W
    row = jnp.repeat(jnp.arange(rows, dtype=_F32), GRID_W)
    col = jnp.tile(jnp.arange(GRID_W, dtype=_F32), rows)
    ang = jnp.concatenate([row[:, None] * inv, col[:, None] * inv], axis=-1)
    cos, sin = jnp.cos(ang), jnp.sin(ang)
    return jnp.concatenate([cos, cos], axis=-1), jnp.concatenate([-sin, sin], axis=-1)


def _decay_tables(decay_logit, l_ctx):
    log_g = jax.nn.log_sigmoid(decay_logit.astype(_F32))
    lgf, lgb = log_g[0][:, None], log_g[1][:, None]
    j = jnp.arange(CHUNK, dtype=_F32)
    diff = j[:, None] - j[None, :]
    dmat = jnp.where(diff[None] >= 0, jnp.exp(lgf[:, :, None] * jnp.maximum(diff, 0.0)[None]),
                     jnp.exp(lgb[:, :, None] * jnp.maximum(-diff, 0.0)[None]))
    lanes = lambda t: jnp.broadcast_to(t[:, :, None], t.shape + (LANES,))
    m = jnp.arange(l_ctx, dtype=_F32)
    return {
        "dmat": dmat,
        "qdf": lanes(jnp.exp(lgf * (j + 1.0))), "qdb": lanes(jnp.exp(lgb * (CHUNK - j))),
        "kdf": lanes(jnp.exp(lgf * (CHUNK - 1.0 - j))), "kdb": lanes(jnp.exp(lgb * j)),
        "cdec": jnp.exp(log_g * CHUNK),
        "ctx_f": lanes(jnp.exp(lgf * ((l_ctx - 1.0) - m))), "ctx_b": lanes(jnp.exp(lgb * m)),
    }


def kernel(x, c, ctx, c_ctx, w_ada, b_ada, g_mix, g_ffn, w_in, w_s, b_s, g_sgu, decay_logit, w_pa, w_pb, w_out,
           w_group, b_group, w_erouter, b_erouter, w1, w3, w2, g_final):
    bsz, n, d = x.shape
    l_ctx = ctx.shape[1]
    t_main, t_ctx = bsz * n, bsz * l_ctx

    cond = jnp.zeros((COND_ROWS, d), _F32).at[:bsz].set(c).at[bsz].set(c_ctx)
    mods = _adaln(cond, w_ada, b_ada)
    perm = _pair_split_perm()
    cos2, sin2 = _rotary_tables(n)
    ones2, zeros2 = jnp.ones((l_ctx, LANES), _F32), jnp.zeros((l_ctx, LANES), _F32)
    zero_state = jnp.zeros((bsz, R_HEADS, R_QK_DIM, R_V_DIM), _F32)

    xc = ctx
    for l in range(DEPTH):
        update_ctx = l < DEPTH - 1
        m6 = mods[l].reshape(COND_ROWS, 6, d)
        main = [m6[:bsz, i][:, None, :] for i in range(6)]
        cmod = [jnp.broadcast_to(m6[bsz, i][None, None, :], (bsz, 1, d)) for i in range(6)]
        tabs = _decay_tables(decay_logit[l], l_ctx)
        w_in_b = w_in[l][:, perm].astype(_BF16)
        mix_w = (w_s[l].astype(_BF16), jnp.broadcast_to(b_s[l][:, :, None], (A_GROUPS, CHUNK, LANES)),
                 g_sgu[l].reshape(1, D_A), w_pa[l].astype(_BF16), w_pb[l].astype(_BF16), w_out[l].astype(_BF16))

        p = _proj_in(x, g_mix[l], main[0], main[1], w_in_b)
        if update_ctx:
            pc = _proj_in(xc, g_mix[l], cmod[0], cmod[1], w_in_b)
            kv_c = pc[..., KV_START:KV_END]
        else:
            kv_c = _proj_in(xc, g_mix[l], cmod[0], cmod[1], w_in_b[:, KV_START:KV_END])
        s_ctx_f, s_ctx_b = _ctx_states(kv_c[..., :D_RQK], kv_c[..., D_RQK:], tabs["ctx_f"], tabs["ctx_b"])
        sb = _bwd_states(p, cos2, sin2, tabs["kdb"], tabs["cdec"], s_ctx_b)
        x = _token_mix(p, x, main[2], cos2, sin2, tabs, sb, s_ctx_f, *mix_w)
        if update_ctx:
            sbc = _bwd_states(pc, ones2, zeros2, tabs["kdb"], tabs["cdec"], zero_state)
            xc = _token_mix(pc, xc, cmod[2], ones2, zeros2, tabs, sbc, zero_state, *mix_w)

        wr = jnp.zeros((d, ROUTE_COLS), _F32).at[:, :MOE_GROUPS].set(w_group[l])
        wr = wr.at[:, MOE_GROUPS:MOE_GROUPS + N_EXPERTS].set(w_erouter[l])
        br = jnp.zeros((1, ROUTE_COLS), _F32).at[0, :MOE_GROUPS].set(b_group[l])
        br = br.at[0, MOE_GROUPS:MOE_GROUPS + N_EXPERTS].set(b_erouter[l])
        h2, ids, wts = _router(x, g_ffn[l], main[3], main[4], wr, br)
        if update_ctx:
            h2c, idsc, wtsc = _router(xc, g_ffn[l], cmod[3], cmod[4], wr, br)
            h2, ids, wts = (jnp.concatenate(pair, axis=0) for pair in ((h2, h2c), (ids, idsc), (wts, wtsc)))
        t_all = h2.shape[0]
        plan = _dispatch_plan(ids[:, :2], wts[:, :2])
        y = _experts(h2, *plan, w1[l].astype(_BF16), w3[l].astype(_BF16), w2[l].astype(_BF16))
        final = l == DEPTH - 1
        x = _combine(x, y, main[5], g_final, 0, t_all, final)
        if update_ctx:
            xc = _combine(xc, y, cmod[5], g_final, t_main, t_all, False)
    return x
```

```python
import functools

import numpy as np
import jax
import jax.numpy as jnp
from jax import lax
from jax.experimental import pallas as pl
from jax.experimental.pallas import tpu as pltpu

D_MODEL = 1024
DEPTH = 2
GRID_W = 64
CHUNK = 128
A_GROUP_DIM = 128
A_GROUPS = D_MODEL // A_GROUP_DIM
D_A = A_GROUPS * A_GROUP_DIM
R_HEADS = 4
R_QK_DIM = D_MODEL // 8
R_V_DIM = D_MODEL // 4
D_RQK = R_HEADS * R_QK_DIM
D_RV = R_HEADS * R_V_DIM
D_IN = 2 * D_A + 2 * D_RQK + 2 * D_RV + 2 * D_MODEL
KV_START = 2 * D_A + D_RQK
KV_END = KV_START + D_RQK + D_RV
ROPE_BASE = 10000.0
MOE_GROUPS = 4
EXPERTS_PER_GROUP = 8
N_EXPERTS = MOE_GROUPS * EXPERTS_PER_GROUP
EXPERT_FF = D_MODEL // 2
EPS = 1e-6

LANES = 128
SUBLANES = 8
MOE_ROWS = 256
VMEM_LIMIT = 56 * 1024 * 1024
COND_ROWS = 16

_U_BLK, _VA_BLK = 0, 1
_Q_BLK, _K_BLK = 2 * D_A // D_RQK, 2 * D_A // D_RQK + 1
_VR_BLK = (2 * D_A + 2 * D_RQK) // D_RV
_GR_BLK, _GA_BLK, _GB_BLK = _VR_BLK + 1, _VR_BLK + 2, _VR_BLK + 3

_F32 = jnp.float32
_BF16 = jnp.bfloat16


def _cparams(*sem):
    return pltpu.CompilerParams(dimension_semantics=sem, vmem_limit_bytes=VMEM_LIMIT)


def _dot(a, b):
    return jnp.dot(a, b, preferred_element_type=_F32)


def _adaln_kernel(cond_ref, w_ref, b_ref, o_ref):
    cnd = cond_ref[...]
    s = cnd * jax.nn.sigmoid(cnd)
    o_ref[0] = jnp.dot(s, w_ref[0], preferred_element_type=_F32, precision=lax.Precision.HIGHEST) + b_ref[0]


def _adaln(cond, w_ada, b_ada):
    depth, d, d6 = w_ada.shape
    tn = 1024
    return pl.pallas_call(
        _adaln_kernel,
        grid=(depth, d6 // tn),
        in_specs=[
            pl.BlockSpec((COND_ROWS, d), lambda l, j: (0, 0)),
            pl.BlockSpec((1, d, tn), lambda l, j: (l, 0, j)),
            pl.BlockSpec((1, 1, tn), lambda l, j: (l, 0, j)),
        ],
        out_specs=pl.BlockSpec((1, COND_ROWS, tn), lambda l, j: (l, 0, j)),
        out_shape=jax.ShapeDtypeStruct((depth, COND_ROWS, d6), _F32),
        compiler_params=_cparams("parallel", "parallel"),
        name="adaln",
    )(cond, w_ada, b_ada.reshape(depth, 1, d6))


def _modnorm(x, g, shift, scale):
    y = x * lax.rsqrt(jnp.mean(x * x, axis=-1, keepdims=True) + EPS)
    return (y * g) * (1.0 + scale) + shift


def _proj_kernel(x_ref, g_ref, sh_ref, sc_ref, w_ref, o_ref, h_ref):
    @pl.when(pl.program_id(2) == 0)
    def _():
        h_ref[...] = _modnorm(x_ref[0], g_ref[...], sh_ref[0], sc_ref[0]).astype(_BF16)

    o_ref[0] = _dot(h_ref[...], w_ref[...]).astype(_BF16)


def _proj_in(x, g, shift, scale, w):
    b, n, d = x.shape
    dout = w.shape[1]
    tm = min(n, 1024)
    tn = 1024 if dout % 1024 == 0 else 512
    return pl.pallas_call(
        _proj_kernel,
        grid=(b, n // tm, dout // tn),
        in_specs=[
            pl.BlockSpec((1, tm, d), lambda i, r, j: (i, r, 0)),
            pl.BlockSpec((1, d), lambda i, r, j: (0, 0)),
            pl.BlockSpec((1, 1, d), lambda i, r, j: (i, 0, 0)),
            pl.BlockSpec((1, 1, d), lambda i, r, j: (i, 0, 0)),
            pl.BlockSpec((d, tn), lambda i, r, j: (0, j)),
        ],
        out_specs=pl.BlockSpec((1, tm, tn), lambda i, r, j: (i, r, j)),
        out_shape=jax.ShapeDtypeStruct((b, n, dout), _BF16),
        scratch_shapes=[pltpu.VMEM((tm, d), _BF16)],
        compiler_params=_cparams("parallel", "parallel", "arbitrary"),
        name="proj_in",
    )(x, g.reshape(1, d), shift, scale, w)


def _rotate(t, cos2, sin2):
    return t * cos2 + pltpu.roll(t, R_QK_DIM // 2, 1) * sin2


def _dot_t(a, b):
    return lax.dot_general(a, b, (((0,), (0,)), ((), ())), preferred_element_type=_F32)


def _ctx_state_kernel(k_ref, v_ref, wf_ref, wb_ref, sf_ref, sb_ref):
    for h in range(R_HEADS):
        kh = k_ref[0, :, h * R_QK_DIM:(h + 1) * R_QK_DIM].astype(_F32) * (R_QK_DIM ** -0.5)
        vh = v_ref[0, :, h * R_V_DIM:(h + 1) * R_V_DIM]
        sf_ref[0, h] = _dot_t((kh * wf_ref[h]).astype(_BF16), vh)
        sb_ref[0, h] = _dot_t((kh * wb_ref[h]).astype(_BF16), vh)


def _ctx_states(k, v, wf, wb):
    b, l, _ = k.shape
    st = jax.ShapeDtypeStruct((b, R_HEADS, R_QK_DIM, R_V_DIM), _F32)
    st_spec = pl.BlockSpec((1, R_HEADS, R_QK_DIM, R_V_DIM), lambda i: (i, 0, 0, 0))
    return pl.pallas_call(
        _ctx_state_kernel,
        grid=(b,),
        in_specs=[
            pl.BlockSpec((1, l, D_RQK), lambda i: (i, 0, 0)),
            pl.BlockSpec((1, l, D_RV), lambda i: (i, 0, 0)),
            pl.BlockSpec((R_HEADS, l, LANES), lambda i: (0, 0, 0)),
            pl.BlockSpec((R_HEADS, l, LANES), lambda i: (0, 0, 0)),
        ],
        out_specs=[st_spec, st_spec],
        out_shape=[st, st],
        compiler_params=_cparams("parallel"),
        name="ctx_states",
    )(k, v, wf, wb)


def _bwd_state_kernel(cdec_ref, k_ref, v_ref, cos_ref, sin_ref, kdb_ref, s0_ref, o_ref, s_ref):
    @pl.when(pl.program_id(1) == 0)
    def _():
        s_ref[...] = s0_ref[0]

    cos2, sin2 = cos_ref[...], sin_ref[...]
    for h in range(R_HEADS):
        s = s_ref[h]
        o_ref[0, 0, h] = s.astype(_BF16)
        kh = _rotate(k_ref[0, :, h * R_QK_DIM:(h + 1) * R_QK_DIM].astype(_F32), cos2, sin2) * (R_QK_DIM ** -0.5)
        vh = v_ref[0, :, h * R_V_DIM:(h + 1) * R_V_DIM]
        s_ref[h] = s * cdec_ref[1, h] + _dot_t((kh * kdb_ref[h]).astype(_BF16), vh)


def _bwd_states(p, cos2, sin2, kdb, cdec, s0_b):
    b, n, _ = p.shape
    nc = n // CHUNK
    rev = lambda i, c: nc - 1 - c
    return pl.pallas_call(
        _bwd_state_kernel,
        grid=(b, nc),
        in_specs=[
            pl.BlockSpec(memory_space=pltpu.SMEM),
            pl.BlockSpec((1, CHUNK, D_RQK), lambda i, c: (i, rev(i, c), _K_BLK)),
            pl.BlockSpec((1, CHUNK, D_RV), lambda i, c: (i, rev(i, c), _VR_BLK)),
            pl.BlockSpec((CHUNK, LANES), lambda i, c: (rev(i, c), 0)),
            pl.BlockSpec((CHUNK, LANES), lambda i, c: (rev(i, c), 0)),
            pl.BlockSpec((R_HEADS, CHUNK, LANES), lambda i, c: (0, 0, 0)),
            pl.BlockSpec((1, R_HEADS, R_QK_DIM, R_V_DIM), lambda i, c: (i, 0, 0, 0)),
        ],
        out_specs=pl.BlockSpec((1, 1, R_HEADS, R_QK_DIM, R_V_DIM), lambda i, c: (i, rev(i, c), 0, 0, 0)),
        out_shape=jax.ShapeDtypeStruct((b, nc, R_HEADS, R_QK_DIM, R_V_DIM), _BF16),
        scratch_shapes=[pltpu.VMEM((R_HEADS, R_QK_DIM, R_V_DIM), _F32)],
        compiler_params=_cparams("parallel", "arbitrary"),
        name="bwd_states",
    )(cdec, p, p, cos2, sin2, kdb, s0_b)


def _layernorm(t):
    mu = jnp.mean(t, axis=-1, keepdims=True)
    tc = t - mu
    return tc * lax.rsqrt(jnp.mean(tc * tc, axis=-1, keepdims=True) + EPS)


def _mix_kernel(n_chunks, cdec_ref, u_ref, va_ref, q_ref, k_ref, vr_ref, gr_ref, ga_ref, gb_ref, x_ref, gt_ref,
                cos_ref, sin_ref, dmat_ref, qdf_ref, qdb_ref, kdf_ref, sb_ref, s0_ref,
                ws_ref, bs_ref, gsgu_ref, wpa_ref, wpb_ref, wout_ref, o_ref, sf_ref, ya_ref, yb_ref):
    @pl.when(pl.program_id(1) == 0)
    def _():
        sf_ref[...] = s0_ref[0]

    def chunk(c, carry):
        r0 = pl.multiple_of(c * CHUNK, CHUNK)
        rows = pl.ds(r0, CHUNK)
        gu = jax.nn.gelu(u_ref[0, rows, :].astype(_F32))
        vn = (_layernorm(jax.nn.gelu(va_ref[0, rows, :].astype(_F32))) * gsgu_ref[...]).astype(_BF16)
        for g in range(A_GROUPS):
            cols = slice(g * A_GROUP_DIM, (g + 1) * A_GROUP_DIM)
            s = _dot(ws_ref[g], vn[:, cols]) + bs_ref[g]
            ya_ref[rows, cols] = (gu[:, cols] * s).astype(_BF16)
        cos2, sin2 = cos_ref[rows, :], sin_ref[rows, :]
        for h in range(R_HEADS):
            qk = slice(h * R_QK_DIM, (h + 1) * R_QK_DIM)
            vv = slice(h * R_V_DIM, (h + 1) * R_V_DIM)
            qh = _rotate(q_ref[0, rows, qk].astype(_F32), cos2, sin2)
            kh = _rotate(k_ref[0, rows, qk].astype(_F32), cos2, sin2) * (R_QK_DIM ** -0.5)
            vh = vr_ref[0, rows, vv]
            scores = lax.dot_general(qh.astype(_BF16), kh.astype(_BF16), (((1,), (1,)), ((), ())),
                                     preferred_element_type=_F32) * dmat_ref[h]
            o = _dot(scores.astype(_BF16), vh)
            s_f = sf_ref[h]
            o = o + _dot((qh * qdf_ref[h]).astype(_BF16), s_f.astype(_BF16))
            o = o + _dot((qh * qdb_ref[h]).astype(_BF16), sb_ref[0, c, h])
            gr = gr_ref[0, rows, vv].astype(_F32)
            yb_ref[rows, vv] = (gr * jax.nn.sigmoid(gr) * _layernorm(o)).astype(_BF16)
            sf_ref[h] = s_f * cdec_ref[0, h] + _dot_t((kh * kdf_ref[h]).astype(_BF16), vh)
        return carry

    lax.fori_loop(0, n_chunks, chunk, 0)
    a = _dot(ya_ref[...], wpa_ref[...])
    b = _dot(yb_ref[...], wpb_ref[...])
    merged = jax.nn.sigmoid(ga_ref[0].astype(_F32)) * a + jax.nn.sigmoid(gb_ref[0].astype(_F32)) * b
    o_ref[0] = x_ref[0] + gt_ref[0] * _dot(merged.astype(_BF16), wout_ref[...])


def _token_mix(p, x, gt, cos2, sin2, tabs, sb, s0_f, ws, bsb, gsgu, wpa, wpb, wout):
    b, n, d = x.shape
    rows = min(n, 512)
    ncb = rows // CHUNK
    pspec = lambda width, blk: pl.BlockSpec((1, rows, width), lambda i, r: (i, r, blk))
    const = lambda shape: pl.BlockSpec(shape, lambda i, r: (0,) * len(shape))
    tab = const((R_HEADS, CHUNK, LANES))
    return pl.pallas_call(
        functools.partial(_mix_kernel, ncb),
        grid=(b, n // rows),
        in_specs=[
            pl.BlockSpec(memory_space=pltpu.SMEM),
            pspec(D_A, _U_BLK), pspec(D_A, _VA_BLK), pspec(D_RQK, _Q_BLK), pspec(D_RQK, _K_BLK),
            pspec(D_RV, _VR_BLK), pspec(D_RV, _GR_BLK), pspec(d, _GA_BLK), pspec(d, _GB_BLK),
            pl.BlockSpec((1, rows, d), lambda i, r: (i, r, 0)),
            pl.BlockSpec((1, 1, d), lambda i, r: (i, 0, 0)),
            pl.BlockSpec((rows, LANES), lambda i, r: (r, 0)),
            pl.BlockSpec((rows, LANES), lambda i, r: (r, 0)),
            tab, tab, tab, tab,
            pl.BlockSpec((1, ncb, R_HEADS, R_QK_DIM, R_V_DIM), lambda i, r: (i, r, 0, 0, 0)),
            pl.BlockSpec((1, R_HEADS, R_QK_DIM, R_V_DIM), lambda i, r: (i, 0, 0, 0)),
            const((A_GROUPS, CHUNK, CHUNK)), const((A_GROUPS, CHUNK, LANES)), const((1, D_A)),
            const((D_A, d)), const((D_RV, d)), const((d, d)),
        ],
        out_specs=pl.BlockSpec((1, rows, d), lambda i, r: (i, r, 0)),
        out_shape=jax.ShapeDtypeStruct((b, n, d), _F32),
        scratch_shapes=[
            pltpu.VMEM((R_HEADS, R_QK_DIM, R_V_DIM), _F32),
            pltpu.VMEM((rows, D_A), _BF16),
            pltpu.VMEM((rows, D_RV), _BF16),
        ],
        compiler_params=_cparams("parallel", "arbitrary"),
        name="token_mix",
    )(tabs["cdec"], p, p, p, p, p, p, p, p, x, gt, cos2, sin2, tabs["dmat"], tabs["qdf"], tabs["qdb"], tabs["kdf"],
      sb, s0_f, ws, bsb, gsgu, wpa, wpb, wout)


ROUTE_COLS = LANES
MOE_TILE = 256
MOE_SLOTS = 2 * MOE_TILE


def _router_kernel(x_ref, g_ref, sh_ref, sc_ref, wr_ref, br_ref, h_ref, id_ref, wt_ref, cnt_ref):
    h = _modnorm(x_ref[...], g_ref[...], sh_ref[0], sc_ref[0])
    h_ref[...] = h.astype(_BF16)
    logits = jnp.dot(h, wr_ref[...], preferred_element_type=_F32, precision=lax.Precision.HIGHEST) + br_ref[...]
    lane = lax.broadcasted_iota(jnp.int32, logits.shape, 1)
    neg = -jnp.inf
    gl = jnp.where(lane < MOE_GROUPS, logits, neg)
    g_top = jnp.max(gl, axis=-1, keepdims=True)
    g_sel = jnp.min(jnp.where(gl == g_top, lane, ROUTE_COLS), axis=-1, keepdims=True)
    lse = g_top + jnp.log(jnp.sum(jnp.exp(gl - g_top), axis=-1, keepdims=True))
    p_group = jnp.exp(g_top - lse)
    lo = MOE_GROUPS + g_sel * EXPERTS_PER_GROUP
    el = jnp.where((lane >= lo) & (lane < lo + EXPERTS_PER_GROUP), logits, neg)
    e1 = jnp.max(el, axis=-1, keepdims=True)
    i1 = jnp.min(jnp.where(el == e1, lane, ROUTE_COLS), axis=-1, keepdims=True)
    el2 = jnp.where(lane == i1, neg, el)
    e2 = jnp.max(el2, axis=-1, keepdims=True)
    i2 = jnp.min(jnp.where(el2 == e2, lane, ROUTE_COLS), axis=-1, keepdims=True)
    x2 = jnp.exp(e2 - e1)
    den = 1.0 + x2
    id1, id2 = i1 - MOE_GROUPS, i2 - MOE_GROUPS
    id_ref[...] = jnp.where(lane == 0, id1, jnp.where(lane == 1, id2, 0))
    wt_ref[...] = jnp.where(lane == 0, p_group * (1.0 / den), jnp.where(lane == 1, p_group * (x2 / den), 0.0))
    hits = jnp.where((lane == id1) | (lane == id2), 1, 0)
    for t in range(cnt_ref.shape[0]):
        tile_hits = jnp.sum(hits[t * MOE_TILE:(t + 1) * MOE_TILE], axis=0, keepdims=True)
        cnt_ref[t] = jnp.broadcast_to(tile_hits, (SUBLANES, ROUTE_COLS))


def _router(x, g, shift, scale, wr, br):
    b, n, d = x.shape
    t = b * n
    tm = min(n, 512)
    per_b = n // tm
    tiles = tm // MOE_TILE
    row = pl.BlockSpec((tm, d), lambda i: (i, 0))
    mod = pl.BlockSpec((1, 1, d), lambda i: (i // per_b, 0, 0))
    lane_out = pl.BlockSpec((tm, ROUTE_COLS), lambda i: (i, 0))
    return pl.pallas_call(
        _router_kernel,
        grid=(t // tm,),
        in_specs=[row, pl.BlockSpec((1, d), lambda i: (0, 0)), mod, mod,
                  pl.BlockSpec((d, ROUTE_COLS), lambda i: (0, 0)), pl.BlockSpec((1, ROUTE_COLS), lambda i: (0, 0))],
        out_specs=[row, lane_out, lane_out, pl.BlockSpec((tiles, SUBLANES, ROUTE_COLS), lambda i: (i, 0, 0))],
        out_shape=[jax.ShapeDtypeStruct((t, d), _BF16), jax.ShapeDtypeStruct((t, ROUTE_COLS), jnp.int32),
                   jax.ShapeDtypeStruct((t, ROUTE_COLS), _F32),
                   jax.ShapeDtypeStruct((t // MOE_TILE, SUBLANES, ROUTE_COLS), jnp.int32)],
        compiler_params=_cparams("parallel"),
        name="router",
    )(x.reshape(t, d), g.reshape(1, d), shift, scale, wr, br)


def _dispatch_plan(cnt):
    n_tiles = cnt.shape[0]
    a = n_tiles * MOE_SLOTS
    tot = jnp.sum(cnt, axis=0)
    padded = (tot + MOE_ROWS - 1) // MOE_ROWS * MOE_ROWS
    pend = jnp.cumsum(padded)
    pstart = pend - padded
    off = pstart[None, :] + jnp.cumsum(cnt, axis=0) - cnt
    n_blocks = (a + N_EXPERTS * (MOE_ROWS - 1) + MOE_ROWS - 1) // MOE_ROWS
    blk_start = jnp.arange(n_blocks, dtype=jnp.int32) * MOE_ROWS
    blk_e = jnp.minimum(jnp.sum((pend[None, :] <= blk_start[:, None]).astype(jnp.int32), axis=1), N_EXPERTS - 1)
    n_used = (pend[-1] // MOE_ROWS).reshape(1)
    pad_start = jnp.concatenate([pstart + tot, blk_start])
    pad_len = jnp.concatenate([padded - tot, jnp.where(blk_start >= pend[-1], MOE_ROWS, 0)])
    i32 = lambda v: v.astype(jnp.int32)
    return (i32(off).reshape(-1), i32(cnt).reshape(-1), i32(pad_start), i32(pad_len)), i32(blk_e), i32(n_used), n_blocks


def _copy_runs(cnt_ref, off_ref, tile, make_copy):
    def body(e, src):
        n = cnt_ref[tile * N_EXPERTS + e]

        @pl.when(n > 0)
        def _():
            make_copy(src, off_ref[tile * N_EXPERTS + e], n).start()
        return src + n
    lax.fori_loop(0, N_EXPERTS, body, 0)


def _wait_slot(stage, sem, s):
    pltpu.make_async_copy(stage.at[s], stage.at[s], sem.at[s]).wait()


def _dispatch_kernel(off_ref, cnt_ref, pad_start_ref, pad_len_ref, h_ref, id_ref, xs_hbm, pos_ref, stage, sem,
                     zeros, zsem):
    i = pl.program_id(0)
    slot = i % 2

    @pl.when(i == 0)
    def _():
        zeros[...] = jnp.zeros(zeros.shape, zeros.dtype)

        def pad_copy(r):
            n = pad_len_ref[r]
            return n, pltpu.make_async_copy(zeros.at[pl.ds(0, n)], xs_hbm.at[pl.ds(pad_start_ref[r], n)], zsem)

        def start(r, carry):
            n, copy = pad_copy(r)

            @pl.when(n > 0)
            def _():
                copy.start()
            return carry

        def wait(r, carry):
            n, copy = pad_copy(r)

            @pl.when(n > 0)
            def _():
                copy.wait()
            return carry

        lax.fori_loop(0, pad_len_ref.shape[0], start, 0)
        lax.fori_loop(0, pad_len_ref.shape[0], wait, 0)

    ids = id_ref[...]
    lane = lax.broadcasted_iota(jnp.int32, ids.shape, 1)
    e0 = jnp.where(lane == ids[:, 0:1], 1.0, 0.0)
    e1 = jnp.where(lane == ids[:, 1:2], 1.0, 0.0)
    tt = lax.broadcasted_iota(jnp.int32, (MOE_TILE, MOE_TILE), 0)
    earlier = jnp.where(lax.broadcasted_iota(jnp.int32, (MOE_TILE, MOE_TILE), 1) < tt, 1.0, 0.0).astype(_BF16)
    c0 = _dot(earlier, e0.astype(_BF16))
    c1 = _dot(earlier, e1.astype(_BF16))
    tot0 = jnp.sum(e0, axis=0, keepdims=True)
    ee = lax.broadcasted_iota(jnp.int32, (ROUTE_COLS, ROUTE_COLS), 0)
    lower = jnp.where(ee < lax.broadcasted_iota(jnp.int32, (ROUTE_COLS, ROUTE_COLS), 1), 1.0, 0.0).astype(_BF16)
    base = jnp.sum(_dot((e0 + e1).astype(_BF16), lower), axis=0, keepdims=True)
    pos0 = jnp.sum(e0 * (base + c0), axis=1, keepdims=True)
    pos1 = jnp.sum(e1 * (base + tot0 + c1), axis=1, keepdims=True)
    pos_ref[...] = jnp.where(lane == 0, pos0, jnp.where(lane == 1, pos1, 0.0))
    slot_id = lax.broadcasted_iota(jnp.int32, (MOE_TILE, MOE_SLOTS), 1).astype(_F32)
    pick = jnp.where((slot_id == pos0) | (slot_id == pos1), 1.0, 0.0).astype(_BF16)
    rows = _dot_t(pick, h_ref[...])

    @pl.when(i >= 2)
    def _():
        _wait_slot(stage, sem, slot)

    for s in range(stage.shape[2]):
        stage[slot, :, s, :] = rows[:, s * LANES:(s + 1) * LANES]
    _copy_runs(cnt_ref, off_ref, i, lambda src, dst, n: pltpu.make_async_copy(
        stage.at[slot, pl.ds(src, n)], xs_hbm.at[pl.ds(dst, n)], sem.at[slot]))

    @pl.when(i == pl.num_programs(0) - 1)
    def _():
        _wait_slot(stage, sem, slot)

        @pl.when(i >= 1)
        def _():
            _wait_slot(stage, sem, 1 - slot)


def _dispatch(h2, ids, plan, n_blocks):
    t, d = h2.shape
    n_tiles = t // MOE_TILE
    grid_spec = pltpu.PrefetchScalarGridSpec(
        num_scalar_prefetch=4,
        grid=(n_tiles,),
        in_specs=[pl.BlockSpec((MOE_TILE, d), lambda i, *_: (i, 0)),
                  pl.BlockSpec((MOE_TILE, ROUTE_COLS), lambda i, *_: (i, 0))],
        out_specs=[pl.BlockSpec(memory_space=pl.ANY), pl.BlockSpec((MOE_TILE, ROUTE_COLS), lambda i, *_: (i, 0))],
        scratch_shapes=[pltpu.VMEM((2, MOE_SLOTS, d // LANES, LANES), _F32), pltpu.SemaphoreType.DMA((2,)),
                        pltpu.VMEM((MOE_ROWS, d // LANES, LANES), _F32), pltpu.SemaphoreType.DMA(())],
    )
    return pl.pallas_call(
        _dispatch_kernel,
        grid_spec=grid_spec,
        out_shape=[jax.ShapeDtypeStruct((n_blocks * MOE_ROWS, d // LANES, LANES), _F32),
                   jax.ShapeDtypeStruct((t, ROUTE_COLS), _F32)],
        compiler_params=_cparams("arbitrary"),
        name="moe_dispatch",
    )(*plan, h2, ids)


def _expert_kernel(blk_e_ref, n_used_ref, x_ref, w1_ref, w3_ref, w2_ref, o_ref):
    i = pl.program_id(0)

    @pl.when(i < n_used_ref[0])
    def _():
        xb = jnp.concatenate([x_ref[:, s, :] for s in range(x_ref.shape[1])], axis=1).astype(_BF16)
        h1 = _dot(xb, w1_ref[0])
        h3 = _dot(xb, w3_ref[0])
        act = (h1 * jax.nn.sigmoid(h1) * h3).astype(_BF16)
        y = _dot(act, w2_ref[0])
        for s in range(o_ref.shape[1]):
            o_ref[:, s, :] = y[:, s * LANES:(s + 1) * LANES]

    @pl.when(i >= n_used_ref[0])
    def _():
        o_ref[...] = jnp.zeros(o_ref.shape, o_ref.dtype)


def _experts(xs, blk_e, n_used, w1, w3, w2):
    p_rows, s8, _ = xs.shape
    d = s8 * LANES
    n_blocks = p_rows // MOE_ROWS
    grid_spec = pltpu.PrefetchScalarGridSpec(
        num_scalar_prefetch=2,
        grid=(n_blocks,),
        in_specs=[
            pl.BlockSpec((MOE_ROWS, s8, LANES), lambda i, be, nu: (jnp.minimum(i, nu[0] - 1), 0, 0)),
            pl.BlockSpec((1, d, EXPERT_FF), lambda i, be, nu: (be[i], 0, 0)),
            pl.BlockSpec((1, d, EXPERT_FF), lambda i, be, nu: (be[i], 0, 0)),
            pl.BlockSpec((1, EXPERT_FF, d), lambda i, be, nu: (be[i], 0, 0)),
        ],
        out_specs=pl.BlockSpec((MOE_ROWS, s8, LANES), lambda i, be, nu: (i, 0, 0)),
    )
    return pl.pallas_call(
        _expert_kernel,
        grid_spec=grid_spec,
        out_shape=jax.ShapeDtypeStruct(xs.shape, _F32),
        compiler_params=_cparams("arbitrary"),
        name="experts",
    )(blk_e, n_used, xs, w1, w3, w2)


def _combine_kernel(final, tile0, off_ref, cnt_ref, x_ref, pos_ref, wt_ref, gt_ref, g_ref, ys_hbm, o_ref, stage, sem):
    i = pl.program_id(0)
    slot = i % 2

    def fetch(tile, s):
        _copy_runs(cnt_ref, off_ref, tile, lambda src, dst, n: pltpu.make_async_copy(
            ys_hbm.at[pl.ds(dst, n)], stage.at[s, pl.ds(src, n)], sem.at[s]))

    @pl.when(i == 0)
    def _():
        fetch(tile0, 0)

    @pl.when(i + 1 < pl.num_programs(0))
    def _():
        fetch(tile0 + i + 1, 1 - slot)

    _wait_slot(stage, sem, slot)
    y = jnp.concatenate([stage[slot, :, s, :] for s in range(stage.shape[2])], axis=1).astype(_BF16)
    slot_id = lax.broadcasted_iota(jnp.int32, (MOE_TILE, MOE_SLOTS), 1).astype(_F32)
    mix = (jnp.where(slot_id == pos_ref[:, 0:1], wt_ref[:, 0:1], 0.0)
           + jnp.where(slot_id == pos_ref[:, 1:2], wt_ref[:, 1:2], 0.0)).astype(_BF16)
    x = x_ref[...] + gt_ref[0] * _dot(mix, y)
    if final:
        x = (x * lax.rsqrt(jnp.mean(x * x, axis=-1, keepdims=True) + EPS)) * g_ref[...]
    o_ref[...] = x


def _combine(x, ys, pos, wts, off, cnt, gt, g_final, tok0, final):
    b, n, d = x.shape
    t = b * n
    per_b = n // MOE_TILE
    tile0 = tok0 // MOE_TILE
    row = pl.BlockSpec((MOE_TILE, d), lambda i, o, c: (i, 0))
    lanes = pl.BlockSpec((MOE_TILE, ROUTE_COLS), lambda i, o, c: (tile0 + i, 0))
    grid_spec = pltpu.PrefetchScalarGridSpec(
        num_scalar_prefetch=2,
        grid=(t // MOE_TILE,),
        in_specs=[row, lanes, lanes, pl.BlockSpec((1, 1, d), lambda i, o, c: (i // per_b, 0, 0)),
                  pl.BlockSpec((1, d), lambda i, o, c: (0, 0)), pl.BlockSpec(memory_space=pl.ANY)],
        out_specs=row,
        scratch_shapes=[pltpu.VMEM((2, MOE_SLOTS, d // LANES, LANES), _F32), pltpu.SemaphoreType.DMA((2,))],
    )
    out = pl.pallas_call(
        functools.partial(_combine_kernel, final, tile0),
        grid_spec=grid_spec,
        out_shape=jax.ShapeDtypeStruct((t, d), _F32),
        compiler_params=_cparams("arbitrary"),
        name="moe_combine",
    )(off, cnt, x.reshape(t, d), pos, wts, gt, g_final.reshape(1, d), ys)
    return out.reshape(b, n, d)


def _pair_split_perm():
    perm = np.arange(D_IN)
    head = np.concatenate([np.arange(0, R_QK_DIM, 2), np.arange(1, R_QK_DIM, 2)])
    for base in (2 * D_A, 2 * D_A + D_RQK):
        for h in range(R_HEADS):
            perm[base + h * R_QK_DIM: base + (h + 1) * R_QK_DIM] = base + h * R_QK_DIM + head
    return perm


def _rotary_tables(n):
    n_freq = R_QK_DIM // 4
    inv = ROPE_BASE ** (-jnp.arange(n_freq, dtype=_F32) / n_freq)
    rows = n // GRID_W
    row = jnp.repeat(jnp.arange(rows, dtype=_F32), GRID_W)
    col = jnp.tile(jnp.arange(GRID_W, dtype=_F32), rows)
    ang = jnp.concatenate([row[:, None] * inv, col[:, None] * inv], axis=-1)
    cos, sin = jnp.cos(ang), jnp.sin(ang)
    return jnp.concatenate([cos, cos], axis=-1), jnp.concatenate([-sin, sin], axis=-1)


def _decay_tables(decay_logit, l_ctx):
    log_g = jax.nn.log_sigmoid(decay_logit.astype(_F32))
    lgf, lgb = log_g[0][:, None], log_g[1][:, None]
    j = jnp.arange(CHUNK, dtype=_F32)
    diff = j[:, None] - j[None, :]
    dmat = jnp.where(diff[None] >= 0, jnp.exp(lgf[:, :, None] * jnp.maximum(diff, 0.0)[None]),
                     jnp.exp(lgb[:, :, None] * jnp.maximum(-diff, 0.0)[None]))
    lanes = lambda t: jnp.broadcast_to(t[:, :, None], t.shape + (LANES,))
    m = jnp.arange(l_ctx, dtype=_F32)
    return {
        "dmat": dmat,
        "qdf": lanes(jnp.exp(lgf * (j + 1.0))), "qdb": lanes(jnp.exp(lgb * (CHUNK - j))),
        "kdf": lanes(jnp.exp(lgf * (CHUNK - 1.0 - j))), "kdb": lanes(jnp.exp(lgb * j)),
        "cdec": jnp.exp(log_g * CHUNK),
        "ctx_f": lanes(jnp.exp(lgf * ((l_ctx - 1.0) - m))), "ctx_b": lanes(jnp.exp(lgb * m)),
    }


def kernel(x, c, ctx, c_ctx, w_ada, b_ada, g_mix, g_ffn, w_in, w_s, b_s, g_sgu, decay_logit, w_pa, w_pb, w_out,
           w_group, b_group, w_erouter, b_erouter, w1, w3, w2, g_final):
    bsz, n, d = x.shape
    l_ctx = ctx.shape[1]
    t_main = bsz * n

    cond = jnp.zeros((COND_ROWS, d), _F32).at[:bsz].set(c).at[bsz].set(c_ctx)
    mods = _adaln(cond, w_ada, b_ada)
    perm = _pair_split_perm()
    cos2, sin2 = _rotary_tables(n)
    ones2, zeros2 = jnp.ones((l_ctx, LANES), _F32), jnp.zeros((l_ctx, LANES), _F32)
    zero_state = jnp.zeros((bsz, R_HEADS, R_QK_DIM, R_V_DIM), _F32)

    xc = ctx
    for l in range(DEPTH):
        update_ctx = l < DEPTH - 1
        m6 = mods[l].reshape(COND_ROWS, 6, d)
        main = [m6[:bsz, i][:, None, :] for i in range(6)]
        cmod = [jnp.broadcast_to(m6[bsz, i][None, None, :], (bsz, 1, d)) for i in range(6)]
        tabs = _decay_tables(decay_logit[l], l_ctx)
        w_in_b = w_in[l][:, perm].astype(_BF16)
        mix_w = (w_s[l].astype(_BF16), jnp.broadcast_to(b_s[l][:, :, None], (A_GROUPS, CHUNK, LANES)),
                 g_sgu[l].reshape(1, D_A), w_pa[l].astype(_BF16), w_pb[l].astype(_BF16), w_out[l].astype(_BF16))

        p = _proj_in(x, g_mix[l], main[0], main[1], w_in_b)
        if update_ctx:
            pc = _proj_in(xc, g_mix[l], cmod[0], cmod[1], w_in_b)
            kv_c = pc[..., KV_START:KV_END]
        else:
            kv_c = _proj_in(xc, g_mix[l], cmod[0], cmod[1], w_in_b[:, KV_START:KV_END])
        s_ctx_f, s_ctx_b = _ctx_states(kv_c[..., :D_RQK], kv_c[..., D_RQK:], tabs["ctx_f"], tabs["ctx_b"])
        sb = _bwd_states(p, cos2, sin2, tabs["kdb"], tabs["cdec"], s_ctx_b)
        x = _token_mix(p, x, main[2], cos2, sin2, tabs, sb, s_ctx_f, *mix_w)
        if update_ctx:
            sbc = _bwd_states(pc, ones2, zeros2, tabs["kdb"], tabs["cdec"], zero_state)
            xc = _token_mix(pc, xc, cmod[2], ones2, zeros2, tabs, sbc, zero_state, *mix_w)

        wr = jnp.zeros((d, ROUTE_COLS), _F32).at[:, :MOE_GROUPS].set(w_group[l])
        wr = wr.at[:, MOE_GROUPS:MOE_GROUPS + N_EXPERTS].set(w_erouter[l])
        br = jnp.zeros((1, ROUTE_COLS), _F32).at[0, :MOE_GROUPS].set(b_group[l])
        br = br.at[0, MOE_GROUPS:MOE_GROUPS + N_EXPERTS].set(b_erouter[l])
        h2, ids, wts, cnt = _router(x, g_ffn[l], main[3], main[4], wr, br)
        if update_ctx:
            routed_c = _router(xc, g_ffn[l], cmod[3], cmod[4], wr, br)
            h2, ids, wts, cnt = (jnp.concatenate(pair, axis=0) for pair in zip((h2, ids, wts, cnt), routed_c))
        plan, blk_e, n_used, n_blocks = _dispatch_plan(cnt[:, 0, :N_EXPERTS])
        off, cnt = plan[:2]
        xs, pos = _dispatch(h2, ids, plan, n_blocks)
        ys = _experts(xs, blk_e, n_used, w1[l].astype(_BF16), w3[l].astype(_BF16), w2[l].astype(_BF16))
        final = l == DEPTH - 1
        x = _combine(x, ys, pos, wts, off, cnt, main[5], g_final, 0, final)
        if update_ctx:
            xc = _combine(xc, ys, pos, wts, off, cnt, cmod[5], g_final, t_main, False)
    return x
```

```python
import functools

import numpy as np
import jax
import jax.numpy as jnp
from jax import lax
from jax.experimental import pallas as pl
from jax.experimental.pallas import tpu as pltpu

D_MODEL = 1024
DEPTH = 2
GRID_W = 64
CHUNK = 128
A_GROUP_DIM = 128
A_GROUPS = D_MODEL // A_GROUP_DIM
D_A = A_GROUPS * A_GROUP_DIM
R_HEADS = 4
R_QK_DIM = D_MODEL // 8
R_V_DIM = D_MODEL // 4
D_RQK = R_HEADS * R_QK_DIM
D_RV = R_HEADS * R_V_DIM
D_IN = 2 * D_A + 2 * D_RQK + 2 * D_RV + 2 * D_MODEL
KV_START = 2 * D_A + D_RQK
KV_END = KV_START + D_RQK + D_RV
ROPE_BASE = 10000.0
MOE_GROUPS = 4
EXPERTS_PER_GROUP = 8
N_EXPERTS = MOE_GROUPS * EXPERTS_PER_GROUP
EXPERT_FF = D_MODEL // 2
EPS = 1e-6

LANES = 128
SUBLANES = 8
MOE_ROWS = 256
VMEM_LIMIT = 56 * 1024 * 1024
COND_ROWS = 16

_U_BLK, _VA_BLK = 0, 1
_Q_BLK, _K_BLK = 2 * D_A // D_RQK, 2 * D_A // D_RQK + 1
_VR_BLK = (2 * D_A + 2 * D_RQK) // D_RV
_GR_BLK, _GA_BLK, _GB_BLK = _VR_BLK + 1, _VR_BLK + 2, _VR_BLK + 3

_F32 = jnp.float32
_BF16 = jnp.bfloat16


def _cparams(*sem):
    return pltpu.CompilerParams(dimension_semantics=sem, vmem_limit_bytes=VMEM_LIMIT)


def _dot(a, b):
    return jnp.dot(a, b, preferred_element_type=_F32)


def _adaln_kernel(cond_ref, w_ref, b_ref, o_ref):
    cnd = cond_ref[...]
    s = cnd * jax.nn.sigmoid(cnd)
    o_ref[0] = jnp.dot(s, w_ref[0], preferred_element_type=_F32, precision=lax.Precision.HIGHEST) + b_ref[0]


def _adaln(cond, w_ada, b_ada):
    depth, d, d6 = w_ada.shape
    tn = 1024
    return pl.pallas_call(
        _adaln_kernel,
        grid=(depth, d6 // tn),
        in_specs=[
            pl.BlockSpec((COND_ROWS, d), lambda l, j: (0, 0)),
            pl.BlockSpec((1, d, tn), lambda l, j: (l, 0, j)),
            pl.BlockSpec((1, 1, tn), lambda l, j: (l, 0, j)),
        ],
        out_specs=pl.BlockSpec((1, COND_ROWS, tn), lambda l, j: (l, 0, j)),
        out_shape=jax.ShapeDtypeStruct((depth, COND_ROWS, d6), _F32),
        compiler_params=_cparams("parallel", "parallel"),
        name="adaln",
    )(cond, w_ada, b_ada.reshape(depth, 1, d6))


def _modnorm(x, g, shift, scale):
    y = x * lax.rsqrt(jnp.mean(x * x, axis=-1, keepdims=True) + EPS)
    return (y * g) * (1.0 + scale) + shift


def _proj_kernel(tn, x_ref, g_ref, sh_ref, sc_ref, w_ref, o_ref, h_ref):
    h_ref[...] = _modnorm(x_ref[0], g_ref[...], sh_ref[0], sc_ref[0]).astype(_BF16)
    for j in range(w_ref.shape[1] // tn):
        cols = slice(j * tn, (j + 1) * tn)
        o_ref[0, :, cols] = _dot(h_ref[...], w_ref[:, cols]).astype(_BF16)


def _proj_in(x, g, shift, scale, w):
    b, n, d = x.shape
    dout = w.shape[1]
    tm = min(n, 512)
    tn = 1024 if dout % 1024 == 0 else 512
    return pl.pallas_call(
        functools.partial(_proj_kernel, tn),
        grid=(b, n // tm),
        in_specs=[
            pl.BlockSpec((1, tm, d), lambda i, r: (i, r, 0)),
            pl.BlockSpec((1, d), lambda i, r: (0, 0)),
            pl.BlockSpec((1, 1, d), lambda i, r: (i, 0, 0)),
            pl.BlockSpec((1, 1, d), lambda i, r: (i, 0, 0)),
            pl.BlockSpec((d, dout), lambda i, r: (0, 0), pipeline_mode=pl.Buffered(1)),
        ],
        out_specs=pl.BlockSpec((1, tm, dout), lambda i, r: (i, r, 0)),
        out_shape=jax.ShapeDtypeStruct((b, n, dout), _BF16),
        scratch_shapes=[pltpu.VMEM((tm, d), _BF16)],
        compiler_params=_cparams("parallel", "parallel"),
        name="proj_in",
    )(x, g.reshape(1, d), shift, scale, w)


def _rotate(t, cos2, sin2):
    return t * cos2 + pltpu.roll(t, R_QK_DIM // 2, 1) * sin2


def _dot_t(a, b):
    return lax.dot_general(a, b, (((0,), (0,)), ((), ())), preferred_element_type=_F32)


def _ctx_state_kernel(k_ref, v_ref, wf_ref, wb_ref, sf_ref, sb_ref):
    for h in range(R_HEADS):
        kh = k_ref[0, :, h * R_QK_DIM:(h + 1) * R_QK_DIM].astype(_F32) * (R_QK_DIM ** -0.5)
        vh = v_ref[0, :, h * R_V_DIM:(h + 1) * R_V_DIM]
        sf_ref[0, h] = _dot_t((kh * wf_ref[h]).astype(_BF16), vh)
        sb_ref[0, h] = _dot_t((kh * wb_ref[h]).astype(_BF16), vh)


def _ctx_states(pc, k_blk, v_blk, wf, wb):
    b, l, _ = pc.shape
    st = jax.ShapeDtypeStruct((b, R_HEADS, R_QK_DIM, R_V_DIM), _F32)
    st_spec = pl.BlockSpec((1, R_HEADS, R_QK_DIM, R_V_DIM), lambda i: (i, 0, 0, 0))
    return pl.pallas_call(
        _ctx_state_kernel,
        grid=(b,),
        in_specs=[
            pl.BlockSpec((1, l, D_RQK), lambda i: (i, 0, k_blk)),
            pl.BlockSpec((1, l, D_RV), lambda i: (i, 0, v_blk)),
            pl.BlockSpec((R_HEADS, l, LANES), lambda i: (0, 0, 0)),
            pl.BlockSpec((R_HEADS, l, LANES), lambda i: (0, 0, 0)),
        ],
        out_specs=[st_spec, st_spec],
        out_shape=[st, st],
        compiler_params=_cparams("parallel"),
        name="ctx_states",
    )(pc, pc, wf, wb)


def _bwd_state_kernel(n_chunks, cdec_ref, k_ref, v_ref, cos_ref, sin_ref, kdb_ref, s0_ref, o_ref, s_ref):
    @pl.when(pl.program_id(1) == 0)
    def _():
        s_ref[...] = s0_ref[0]

    def chunk(step, carry):
        c = n_chunks - 1 - step
        rows = pl.ds(pl.multiple_of(c * CHUNK, CHUNK), CHUNK)
        cos2, sin2 = cos_ref[rows, :], sin_ref[rows, :]
        for h in range(R_HEADS):
            s = s_ref[h]
            o_ref[0, c, h] = s.astype(_BF16)
            kh = _rotate(k_ref[0, rows, h * R_QK_DIM:(h + 1) * R_QK_DIM].astype(_F32), cos2, sin2) * (R_QK_DIM ** -0.5)
            vh = v_ref[0, rows, h * R_V_DIM:(h + 1) * R_V_DIM]
            s_ref[h] = s * cdec_ref[1, h] + _dot_t((kh * kdb_ref[h]).astype(_BF16), vh)
        return carry

    lax.fori_loop(0, n_chunks, chunk, 0)


def _bwd_states(p, cos2, sin2, kdb, cdec, s0_b):
    b, n, _ = p.shape
    rows = min(n, 1024)
    ncb, nblk = rows // CHUNK, n // rows
    rev = lambda r: nblk - 1 - r
    return pl.pallas_call(
        functools.partial(_bwd_state_kernel, ncb),
        grid=(b, nblk),
        in_specs=[
            pl.BlockSpec(memory_space=pltpu.SMEM),
            pl.BlockSpec((1, rows, D_RQK), lambda i, r: (i, rev(r), _K_BLK)),
            pl.BlockSpec((1, rows, D_RV), lambda i, r: (i, rev(r), _VR_BLK)),
            pl.BlockSpec((rows, LANES), lambda i, r: (rev(r), 0)),
            pl.BlockSpec((rows, LANES), lambda i, r: (rev(r), 0)),
            pl.BlockSpec((R_HEADS, CHUNK, LANES), lambda i, r: (0, 0, 0)),
            pl.BlockSpec((1, R_HEADS, R_QK_DIM, R_V_DIM), lambda i, r: (i, 0, 0, 0)),
        ],
        out_specs=pl.BlockSpec((1, ncb, R_HEADS, R_QK_DIM, R_V_DIM), lambda i, r: (i, rev(r), 0, 0, 0)),
        out_shape=jax.ShapeDtypeStruct((b, n // CHUNK, R_HEADS, R_QK_DIM, R_V_DIM), _BF16),
        scratch_shapes=[pltpu.VMEM((R_HEADS, R_QK_DIM, R_V_DIM), _F32)],
        compiler_params=_cparams("parallel", "arbitrary"),
        name="bwd_states",
    )(cdec, p, p, cos2, sin2, kdb, s0_b)


def _layernorm(t):
    mu = jnp.mean(t, axis=-1, keepdims=True)
    tc = t - mu
    return tc * lax.rsqrt(jnp.mean(tc * tc, axis=-1, keepdims=True) + EPS)


def _mix_kernel(n_chunks, cdec_ref, u_ref, va_ref, q_ref, k_ref, vr_ref, gr_ref, ga_ref, gb_ref, x_ref, gt_ref,
                cos_ref, sin_ref, dmat_ref, qdf_ref, qdb_ref, kdf_ref, sb_ref, s0_ref,
                ws_ref, bs_ref, gsgu_ref, wpa_ref, wpb_ref, wout_ref, o_ref, sf_ref, ya_ref, yb_ref):
    @pl.when(pl.program_id(1) == 0)
    def _():
        sf_ref[...] = s0_ref[0]

    def chunk(c, carry):
        r0 = pl.multiple_of(c * CHUNK, CHUNK)
        rows = pl.ds(r0, CHUNK)
        gu = jax.nn.gelu(u_ref[0, rows, :].astype(_F32))
        vn = (_layernorm(jax.nn.gelu(va_ref[0, rows, :].astype(_F32))) * gsgu_ref[...]).astype(_BF16)
        for g in range(A_GROUPS):
            cols = slice(g * A_GROUP_DIM, (g + 1) * A_GROUP_DIM)
            s = _dot(ws_ref[g], vn[:, cols]) + bs_ref[g]
            ya_ref[rows, cols] = (gu[:, cols] * s).astype(_BF16)
        cos2, sin2 = cos_ref[rows, :], sin_ref[rows, :]
        for h in range(R_HEADS):
            qk = slice(h * R_QK_DIM, (h + 1) * R_QK_DIM)
            vv = slice(h * R_V_DIM, (h + 1) * R_V_DIM)
            qh = _rotate(q_ref[0, rows, qk].astype(_F32), cos2, sin2)
            kh = _rotate(k_ref[0, rows, qk].astype(_F32), cos2, sin2) * (R_QK_DIM ** -0.5)
            vh = vr_ref[0, rows, vv]
            scores = lax.dot_general(qh.astype(_BF16), kh.astype(_BF16), (((1,), (1,)), ((), ())),
                                     preferred_element_type=_F32) * dmat_ref[h]
            o = _dot(scores.astype(_BF16), vh)
            s_f = sf_ref[h]
            o = o + _dot((qh * qdf_ref[h]).astype(_BF16), s_f.astype(_BF16))
            o = o + _dot((qh * qdb_ref[h]).astype(_BF16), sb_ref[0, c, h])
            gr = gr_ref[0, rows, vv].astype(_F32)
            yb_ref[rows, vv] = (gr * jax.nn.sigmoid(gr) * _layernorm(o)).astype(_BF16)
            sf_ref[h] = s_f * cdec_ref[0, h] + _dot_t((kh * kdf_ref[h]).astype(_BF16), vh)
        return carry

    lax.fori_loop(0, n_chunks, chunk, 0)
    a = _dot(ya_ref[...], wpa_ref[...])
    b = _dot(yb_ref[...], wpb_ref[...])
    merged = jax.nn.sigmoid(ga_ref[0].astype(_F32)) * a + jax.nn.sigmoid(gb_ref[0].astype(_F32)) * b
    o_ref[0] = x_ref[0] + gt_ref[0] * _dot(merged.astype(_BF16), wout_ref[...])


def _token_mix(p, x, gt, cos2, sin2, tabs, sb, s0_f, ws, bsb, gsgu, wpa, wpb, wout):
    b, n, d = x.shape
    rows = min(n, 512)
    ncb = rows // CHUNK
    pspec = lambda width, blk: pl.BlockSpec((1, rows, width), lambda i, r: (i, r, blk))
    const = lambda shape: pl.BlockSpec(shape, lambda i, r: (0,) * len(shape))
    tab = const((R_HEADS, CHUNK, LANES))
    return pl.pallas_call(
        functools.partial(_mix_kernel, ncb),
        grid=(b, n // rows),
        in_specs=[
            pl.BlockSpec(memory_space=pltpu.SMEM),
            pspec(D_A, _U_BLK), pspec(D_A, _VA_BLK), pspec(D_RQK, _Q_BLK), pspec(D_RQK, _K_BLK),
            pspec(D_RV, _VR_BLK), pspec(D_RV, _GR_BLK), pspec(d, _GA_BLK), pspec(d, _GB_BLK),
            pl.BlockSpec((1, rows, d), lambda i, r: (i, r, 0)),
            pl.BlockSpec((1, 1, d), lambda i, r: (i, 0, 0)),
            pl.BlockSpec((rows, LANES), lambda i, r: (r, 0)),
            pl.BlockSpec((rows, LANES), lambda i, r: (r, 0)),
            tab, tab, tab, tab,
            pl.BlockSpec((1, ncb, R_HEADS, R_QK_DIM, R_V_DIM), lambda i, r: (i, r, 0, 0, 0)),
            pl.BlockSpec((1, R_HEADS, R_QK_DIM, R_V_DIM), lambda i, r: (i, 0, 0, 0)),
            const((A_GROUPS, CHUNK, CHUNK)), const((A_GROUPS, CHUNK, LANES)), const((1, D_A)),
            const((D_A, d)), const((D_RV, d)), const((d, d)),
        ],
        out_specs=pl.BlockSpec((1, rows, d), lambda i, r: (i, r, 0)),
        out_shape=jax.ShapeDtypeStruct((b, n, d), _F32),
        scratch_shapes=[
            pltpu.VMEM((R_HEADS, R_QK_DIM, R_V_DIM), _F32),
            pltpu.VMEM((rows, D_A), _BF16),
            pltpu.VMEM((rows, D_RV), _BF16),
        ],
        compiler_params=_cparams("parallel", "arbitrary"),
        name="token_mix",
    )(tabs["cdec"], p, p, p, p, p, p, p, p, x, gt, cos2, sin2, tabs["dmat"], tabs["qdf"], tabs["qdb"], tabs["kdf"],
      sb, s0_f, ws, bsb, gsgu, wpa, wpb, wout)


ROUTE_COLS = LANES
MOE_TILE = 256
MOE_SLOTS = 2 * MOE_TILE
SLOT_SUBROWS = D_MODEL // LANES


def _router_kernel(x_ref, g_ref, sh_ref, sc_ref, wr_ref, br_ref, h_ref, id_ref, wt_ref, cnt_ref):
    h = _modnorm(x_ref[...], g_ref[...], sh_ref[0], sc_ref[0])
    h_hi = h.astype(_BF16)
    h_ref[...] = h_hi
    h_lo = (h - h_hi.astype(_F32)).astype(_BF16)
    r_hi = _dot(h_hi, wr_ref[...])
    logits = r_hi + pltpu.roll(r_hi, ROUTE_COLS // 2, 1) + _dot(h_lo, wr_ref[...]) + br_ref[...]
    lane = lax.broadcasted_iota(jnp.int32, logits.shape, 1)
    neg = -jnp.inf
    gl = jnp.where(lane < MOE_GROUPS, logits, neg)
    g_top = jnp.max(gl, axis=-1, keepdims=True)
    g_sel = jnp.min(jnp.where(gl == g_top, lane, ROUTE_COLS), axis=-1, keepdims=True)
    lse = g_top + jnp.log(jnp.sum(jnp.exp(gl - g_top), axis=-1, keepdims=True))
    p_group = jnp.exp(g_top - lse)
    lo = MOE_GROUPS + g_sel * EXPERTS_PER_GROUP
    el = jnp.where((lane >= lo) & (lane < lo + EXPERTS_PER_GROUP), logits, neg)
    e1 = jnp.max(el, axis=-1, keepdims=True)
    i1 = jnp.min(jnp.where(el == e1, lane, ROUTE_COLS), axis=-1, keepdims=True)
    el2 = jnp.where(lane == i1, neg, el)
    e2 = jnp.max(el2, axis=-1, keepdims=True)
    i2 = jnp.min(jnp.where(el2 == e2, lane, ROUTE_COLS), axis=-1, keepdims=True)
    x2 = jnp.exp(e2 - e1)
    den = 1.0 + x2
    id1, id2 = i1 - MOE_GROUPS, i2 - MOE_GROUPS
    id_ref[...] = jnp.where(lane == 0, id1, jnp.where(lane == 1, id2, 0))
    wt_ref[...] = jnp.where(lane == 0, p_group * (1.0 / den), jnp.where(lane == 1, p_group * (x2 / den), 0.0))
    hits = jnp.where((lane == id1) | (lane == id2), 1, 0)
    for t in range(cnt_ref.shape[0]):
        tile_hits = jnp.sum(hits[t * MOE_TILE:(t + 1) * MOE_TILE], axis=0, keepdims=True)
        cnt_ref[t] = jnp.broadcast_to(tile_hits, (SUBLANES, ROUTE_COLS))


def _router(x, g, shift, scale, wr, br):
    b, n, d = x.shape
    t = b * n
    tm = min(n, 512)
    per_b = n // tm
    tiles = tm // MOE_TILE
    row = pl.BlockSpec((tm, d), lambda i: (i, 0))
    mod = pl.BlockSpec((1, 1, d), lambda i: (i // per_b, 0, 0))
    lane_out = pl.BlockSpec((tm, ROUTE_COLS), lambda i: (i, 0))
    return pl.pallas_call(
        _router_kernel,
        grid=(t // tm,),
        in_specs=[row, pl.BlockSpec((1, d), lambda i: (0, 0)), mod, mod,
                  pl.BlockSpec((d, ROUTE_COLS), lambda i: (0, 0)), pl.BlockSpec((1, ROUTE_COLS), lambda i: (0, 0))],
        out_specs=[row, lane_out, lane_out, pl.BlockSpec((tiles, SUBLANES, ROUTE_COLS), lambda i: (i, 0, 0))],
        out_shape=[jax.ShapeDtypeStruct((t, d), _BF16), jax.ShapeDtypeStruct((t, ROUTE_COLS), jnp.int32),
                   jax.ShapeDtypeStruct((t, ROUTE_COLS), _F32),
                   jax.ShapeDtypeStruct((t // MOE_TILE, SUBLANES, ROUTE_COLS), jnp.int32)],
        compiler_params=_cparams("parallel"),
        name="router",
    )(x.reshape(t, d), g.reshape(1, d), shift, scale, wr, br)


def _dispatch_plan(cnt):
    n_tiles = cnt.shape[0]
    a = n_tiles * MOE_SLOTS
    tot = jnp.sum(cnt, axis=0)
    padded = (tot + MOE_ROWS - 1) // MOE_ROWS * MOE_ROWS
    pend = jnp.cumsum(padded)
    pstart = pend - padded
    off = pstart[None, :] + jnp.cumsum(cnt, axis=0) - cnt
    n_blocks = (a + N_EXPERTS * (MOE_ROWS - 1) + MOE_ROWS - 1) // MOE_ROWS
    blk_start = jnp.arange(n_blocks, dtype=jnp.int32) * MOE_ROWS
    blk_e = jnp.minimum(jnp.sum((pend[None, :] <= blk_start[:, None]).astype(jnp.int32), axis=1), N_EXPERTS - 1)
    n_used = (pend[-1] // MOE_ROWS).reshape(1)
    pad_start = jnp.concatenate([pstart + tot, blk_start])
    pad_len = jnp.concatenate([padded - tot, jnp.where(blk_start >= pend[-1], MOE_ROWS, 0)])
    i32 = lambda v: v.astype(jnp.int32)
    return (i32(off).reshape(-1), i32(cnt).reshape(-1), i32(pad_start), i32(pad_len)), i32(blk_e), i32(n_used), n_blocks


def _copy_runs(cnt_ref, off_ref, tile, make_copy):
    def body(e, src):
        n = cnt_ref[tile * N_EXPERTS + e]

        @pl.when(n > 0)
        def _():
            make_copy(src, off_ref[tile * N_EXPERTS + e], n).start()
        return src + n
    lax.fori_loop(0, N_EXPERTS, body, 0)


def _slot_rows(first, n):
    s = SLOT_SUBROWS
    return pl.ds(pl.multiple_of(first * s, s), pl.multiple_of(n * s, s))


def _load_slot_major(ref, n_slots):
    return jnp.concatenate([ref[pl.ds(s, n_slots, stride=SLOT_SUBROWS), :] for s in range(SLOT_SUBROWS)], axis=1)


def _store_slot_major(ref, val):
    for s in range(SLOT_SUBROWS):
        ref[pl.ds(s, val.shape[0], stride=SLOT_SUBROWS), :] = val[:, s * LANES:(s + 1) * LANES]


def _wait_slot(stage, sem, s):
    pltpu.make_async_copy(stage.at[s], stage.at[s], sem.at[s]).wait()


def _dispatch_kernel(off_ref, cnt_ref, pad_start_ref, pad_len_ref, h_ref, id_ref, xs_hbm, pos_ref, stage, sem,
                     zeros, zsem):
    i = pl.program_id(0)
    slot = i % 2

    @pl.when(i == 0)
    def _():
        zeros[...] = jnp.zeros(zeros.shape, zeros.dtype)

        def pad_copy(r):
            n = pad_len_ref[r]
            return n, pltpu.make_async_copy(zeros.at[_slot_rows(0, n)], xs_hbm.at[_slot_rows(pad_start_ref[r], n)], zsem)

        def start(r, carry):
            n, copy = pad_copy(r)

            @pl.when(n > 0)
            def _():
                copy.start()
            return carry

        def wait(r, carry):
            n, copy = pad_copy(r)

            @pl.when(n > 0)
            def _():
                copy.wait()
            return carry

        lax.fori_loop(0, pad_len_ref.shape[0], start, 0)
        lax.fori_loop(0, pad_len_ref.shape[0], wait, 0)

    ids = id_ref[...]
    lane = lax.broadcasted_iota(jnp.int32, ids.shape, 1)
    e0 = jnp.where(lane == ids[:, 0:1], 1.0, 0.0)
    e1 = jnp.where(lane == ids[:, 1:2], 1.0, 0.0)
    tt = lax.broadcasted_iota(jnp.int32, (MOE_TILE, MOE_TILE), 0)
    earlier = jnp.where(lax.broadcasted_iota(jnp.int32, (MOE_TILE, MOE_TILE), 1) < tt, 1.0, 0.0).astype(_BF16)
    c0 = _dot(earlier, e0.astype(_BF16))
    c1 = _dot(earlier, e1.astype(_BF16))
    tot0 = jnp.sum(e0, axis=0, keepdims=True)
    ee = lax.broadcasted_iota(jnp.int32, (ROUTE_COLS, ROUTE_COLS), 0)
    lower = jnp.where(ee < lax.broadcasted_iota(jnp.int32, (ROUTE_COLS, ROUTE_COLS), 1), 1.0, 0.0).astype(_BF16)
    base = jnp.sum(_dot((e0 + e1).astype(_BF16), lower), axis=0, keepdims=True)
    pos0 = jnp.sum(e0 * (base + c0), axis=1, keepdims=True)
    pos1 = jnp.sum(e1 * (base + tot0 + c1), axis=1, keepdims=True)
    pos = jnp.where(lane == 0, pos0, jnp.where(lane == 1, pos1, 0.0))
    pos_ref[...] = pos
    pos_t = pos.T
    slot_id = lax.broadcasted_iota(jnp.int32, (MOE_SLOTS, MOE_TILE), 0).astype(_F32)
    pick = jnp.where((slot_id == pos_t[0:1, :]) | (slot_id == pos_t[1:2, :]), 1.0, 0.0).astype(_BF16)
    rows = _dot(pick, h_ref[...])

    @pl.when(i >= 2)
    def _():
        _wait_slot(stage, sem, slot)

    _store_slot_major(stage.at[slot], rows)
    _copy_runs(cnt_ref, off_ref, i, lambda src, dst, n: pltpu.make_async_copy(
        stage.at[slot, _slot_rows(src, n)], xs_hbm.at[_slot_rows(dst, n)], sem.at[slot]))

    @pl.when(i == pl.num_programs(0) - 1)
    def _():
        _wait_slot(stage, sem, slot)

        @pl.when(i >= 1)
        def _():
            _wait_slot(stage, sem, 1 - slot)


def _dispatch(h2, ids, plan, n_blocks):
    t, d = h2.shape
    n_tiles = t // MOE_TILE
    grid_spec = pltpu.PrefetchScalarGridSpec(
        num_scalar_prefetch=4,
        grid=(n_tiles,),
        in_specs=[pl.BlockSpec((MOE_TILE, d), lambda i, *_: (i, 0)),
                  pl.BlockSpec((MOE_TILE, ROUTE_COLS), lambda i, *_: (i, 0))],
        out_specs=[pl.BlockSpec(memory_space=pl.ANY), pl.BlockSpec((MOE_TILE, ROUTE_COLS), lambda i, *_: (i, 0))],
        scratch_shapes=[pltpu.VMEM((2, MOE_SLOTS * SLOT_SUBROWS, LANES), _F32), pltpu.SemaphoreType.DMA((2,)),
                        pltpu.VMEM((MOE_ROWS * SLOT_SUBROWS, LANES), _F32), pltpu.SemaphoreType.DMA(())],
    )
    return pl.pallas_call(
        _dispatch_kernel,
        grid_spec=grid_spec,
        out_shape=[jax.ShapeDtypeStruct((n_blocks * MOE_ROWS * SLOT_SUBROWS, LANES), _F32),
                   jax.ShapeDtypeStruct((t, ROUTE_COLS), _F32)],
        compiler_params=_cparams("arbitrary"),
        name="moe_dispatch",
    )(*plan, h2, ids)


def _expert_kernel(blk_e_ref, n_used_ref, x_ref, w1_ref, w3_ref, w2_ref, o_ref, w1b, w3b, w2b):
    i = pl.program_id(0)

    @pl.when((i == 0) | (blk_e_ref[i] != blk_e_ref[jnp.maximum(i - 1, 0)]))
    def _():
        w1b[...] = w1_ref[0].astype(_BF16)
        w3b[...] = w3_ref[0].astype(_BF16)
        w2b[...] = w2_ref[0].astype(_BF16)

    @pl.when(i < n_used_ref[0])
    def _():
        xb = _load_slot_major(x_ref, MOE_ROWS).astype(_BF16)
        h1 = _dot(xb, w1b[...])
        h3 = _dot(xb, w3b[...])
        act = (h1 * jax.nn.sigmoid(h1) * h3).astype(_BF16)
        _store_slot_major(o_ref, _dot(act, w2b[...]))

    @pl.when(i >= n_used_ref[0])
    def _():
        o_ref[...] = jnp.zeros(o_ref.shape, o_ref.dtype)


def _experts(xs, blk_e, n_used, w1, w3, w2):
    d = w1.shape[1]
    blk = MOE_ROWS * SLOT_SUBROWS
    n_blocks = xs.shape[0] // blk
    grid_spec = pltpu.PrefetchScalarGridSpec(
        num_scalar_prefetch=2,
        grid=(n_blocks,),
        in_specs=[
            pl.BlockSpec((blk, LANES), lambda i, be, nu: (jnp.maximum(jnp.minimum(i, nu[0] - 1), 0), 0)),
            pl.BlockSpec((1, d, EXPERT_FF), lambda i, be, nu: (be[i], 0, 0)),
            pl.BlockSpec((1, d, EXPERT_FF), lambda i, be, nu: (be[i], 0, 0)),
            pl.BlockSpec((1, EXPERT_FF, d), lambda i, be, nu: (be[i], 0, 0)),
        ],
        out_specs=pl.BlockSpec((blk, LANES), lambda i, be, nu: (i, 0)),
        scratch_shapes=[pltpu.VMEM((d, EXPERT_FF), _BF16), pltpu.VMEM((d, EXPERT_FF), _BF16),
                        pltpu.VMEM((EXPERT_FF, d), _BF16)],
    )
    return pl.pallas_call(
        _expert_kernel,
        grid_spec=grid_spec,
        out_shape=jax.ShapeDtypeStruct(xs.shape, _F32),
        compiler_params=_cparams("arbitrary"),
        name="experts",
    )(blk_e, n_used, xs, w1, w3, w2)


def _combine_kernel(final, tile0, off_ref, cnt_ref, x_ref, pos_ref, wt_ref, gt_ref, g_ref, ys_hbm, o_ref, stage, sem):
    i = pl.program_id(0)
    slot = i % 2

    def fetch(tile, s):
        _copy_runs(cnt_ref, off_ref, tile, lambda src, dst, n: pltpu.make_async_copy(
            ys_hbm.at[_slot_rows(dst, n)], stage.at[s, _slot_rows(src, n)], sem.at[s]))

    @pl.when(i == 0)
    def _():
        fetch(tile0, 0)

    @pl.when(i + 1 < pl.num_programs(0))
    def _():
        fetch(tile0 + i + 1, 1 - slot)

    _wait_slot(stage, sem, slot)
    y = _load_slot_major(stage.at[slot], MOE_SLOTS).astype(_BF16)
    slot_id = lax.broadcasted_iota(jnp.int32, (MOE_TILE, MOE_SLOTS), 1).astype(_F32)
    mix = (jnp.where(slot_id == pos_ref[:, 0:1], wt_ref[:, 0:1], 0.0)
           + jnp.where(slot_id == pos_ref[:, 1:2], wt_ref[:, 1:2], 0.0)).astype(_BF16)
    x = x_ref[...] + gt_ref[0] * _dot(mix, y)
    if final:
        x = (x * lax.rsqrt(jnp.mean(x * x, axis=-1, keepdims=True) + EPS)) * g_ref[...]
    o_ref[...] = x


def _combine(x, ys, pos, wts, off, cnt, gt, g_final, tok0, final):
    b, n, d = x.shape
    t = b * n
    per_b = n // MOE_TILE
    tile0 = tok0 // MOE_TILE
    row = pl.BlockSpec((MOE_TILE, d), lambda i, o, c: (i, 0))
    lanes = pl.BlockSpec((MOE_TILE, ROUTE_COLS), lambda i, o, c: (tile0 + i, 0))
    grid_spec = pltpu.PrefetchScalarGridSpec(
        num_scalar_prefetch=2,
        grid=(t // MOE_TILE,),
        in_specs=[row, lanes, lanes, pl.BlockSpec((1, 1, d), lambda i, o, c: (i // per_b, 0, 0)),
                  pl.BlockSpec((1, d), lambda i, o, c: (0, 0)), pl.BlockSpec(memory_space=pl.ANY)],
        out_specs=row,
        scratch_shapes=[pltpu.VMEM((2, MOE_SLOTS * SLOT_SUBROWS, LANES), _F32), pltpu.SemaphoreType.DMA((2,))],
    )
    out = pl.pallas_call(
        functools.partial(_combine_kernel, final, tile0),
        grid_spec=grid_spec,
        out_shape=jax.ShapeDtypeStruct((t, d), _F32),
        compiler_params=_cparams("arbitrary"),
        name="moe_combine",
    )(off, cnt, x.reshape(t, d), pos, wts, gt, g_final.reshape(1, d), ys)
    return out.reshape(b, n, d)


def _pair_split_perm():
    perm = np.arange(D_IN)
    head = np.concatenate([np.arange(0, R_QK_DIM, 2), np.arange(1, R_QK_DIM, 2)])
    for base in (2 * D_A, 2 * D_A + D_RQK):
        for h in range(R_HEADS):
            perm[base + h * R_QK_DIM: base + (h + 1) * R_QK_DIM] = base + h * R_QK_DIM + head
    return perm


def _rotary_tables(n):
    n_freq = R_QK_DIM // 4
    inv = ROPE_BASE ** (-jnp.arange(n_freq, dtype=_F32) / n_freq)
    rows = n // GRID_W
    row = jnp.repeat(jnp.arange(rows, dtype=_F32), GRID_W)
    col = jnp.tile(jnp.arange(GRID_W, dtype=_F32), rows)
    ang = jnp.concatenate([row[:, None] * inv, col[:, None] * inv], axis=-1)
    cos, sin = jnp.cos(ang), jnp.sin(ang)
    return jnp.concatenate([cos, cos], axis=-1), jnp.concatenate([-sin, sin], axis=-1)


def _decay_tables(decay_logit, l_ctx):
    log_g = jax.nn.log_sigmoid(decay_logit.astype(_F32))
    lgf, lgb = log_g[0][:, None], log_g[1][:, None]
    j = jnp.arange(CHUNK, dtype=_F32)
    diff = j[:, None] - j[None, :]
    dmat = jnp.where(diff[None] >= 0, jnp.exp(lgf[:, :, None] * jnp.maximum(diff, 0.0)[None]),
                     jnp.exp(lgb[:, :, None] * jnp.maximum(-diff, 0.0)[None]))
    lanes = lambda t: jnp.broadcast_to(t[:, :, None], t.shape + (LANES,))
    m = jnp.arange(l_ctx, dtype=_F32)
    return {
        "dmat": dmat,
        "qdf": lanes(jnp.exp(lgf * (j + 1.0))), "qdb": lanes(jnp.exp(lgb * (CHUNK - j))),
        "kdf": lanes(jnp.exp(lgf * (CHUNK - 1.0 - j))), "kdb": lanes(jnp.exp(lgb * j)),
        "cdec": jnp.exp(log_g * CHUNK),
        "ctx_f": lanes(jnp.exp(lgf * ((l_ctx - 1.0) - m))), "ctx_b": lanes(jnp.exp(lgb * m)),
    }


def kernel(x, c, ctx, c_ctx, w_ada, b_ada, g_mix, g_ffn, w_in, w_s, b_s, g_sgu, decay_logit, w_pa, w_pb, w_out,
           w_group, b_group, w_erouter, b_erouter, w1, w3, w2, g_final):
    bsz, n, d = x.shape
    l_ctx = ctx.shape[1]
    t_main = bsz * n

    cond = jnp.zeros((COND_ROWS, d), _F32).at[:bsz].set(c).at[bsz].set(c_ctx)
    mods = _adaln(cond, w_ada, b_ada)
    perm = _pair_split_perm()
    cos2, sin2 = _rotary_tables(n)
    ones2, zeros2 = jnp.ones((l_ctx, LANES), _F32), jnp.zeros((l_ctx, LANES), _F32)
    zero_state = jnp.zeros((bsz, R_HEADS, R_QK_DIM, R_V_DIM), _F32)

    xc = ctx
    for l in range(DEPTH):
        update_ctx = l < DEPTH - 1
        m6 = mods[l].reshape(COND_ROWS, 6, d)
        main = [m6[:bsz, i][:, None, :] for i in range(6)]
        cmod = [jnp.broadcast_to(m6[bsz, i][None, None, :], (bsz, 1, d)) for i in range(6)]
        tabs = _decay_tables(decay_logit[l], l_ctx)
        w_in_b = w_in[l][:, perm].astype(_BF16)
        mix_w = (w_s[l].astype(_BF16), jnp.broadcast_to(b_s[l][:, :, None], (A_GROUPS, CHUNK, LANES)),
                 g_sgu[l].reshape(1, D_A), w_pa[l].astype(_BF16), w_pb[l].astype(_BF16), w_out[l].astype(_BF16))

        p = _proj_in(x, g_mix[l], main[0], main[1], w_in_b)
        if update_ctx:
            pc = _proj_in(xc, g_mix[l], cmod[0], cmod[1], w_in_b)
            s_ctx_f, s_ctx_b = _ctx_states(pc, _K_BLK, _VR_BLK, tabs["ctx_f"], tabs["ctx_b"])
        else:
            w_vk = jnp.concatenate([w_in_b[:, KV_START + D_RQK:KV_END], w_in_b[:, KV_START:KV_START + D_RQK]], axis=1)
            vk_c = _proj_in(xc, g_mix[l], cmod[0], cmod[1], w_vk)
            s_ctx_f, s_ctx_b = _ctx_states(vk_c, D_RV // D_RQK, 0, tabs["ctx_f"], tabs["ctx_b"])
        sb = _bwd_states(p, cos2, sin2, tabs["kdb"], tabs["cdec"], s_ctx_b)
        x = _token_mix(p, x, main[2], cos2, sin2, tabs, sb, s_ctx_f, *mix_w)
        if update_ctx:
            sbc = _bwd_states(pc, ones2, zeros2, tabs["kdb"], tabs["cdec"], zero_state)
            xc = _token_mix(pc, xc, cmod[2], ones2, zeros2, tabs, sbc, zero_state, *mix_w)

        wr = jnp.zeros((d, ROUTE_COLS // 2), _F32).at[:, :MOE_GROUPS].set(w_group[l])
        wr = wr.at[:, MOE_GROUPS:MOE_GROUPS + N_EXPERTS].set(w_erouter[l])
        wr_hi = wr.astype(_BF16)
        wr = jnp.concatenate([wr_hi, (wr - wr_hi.astype(_F32)).astype(_BF16)], axis=1)
        br = jnp.zeros((1, ROUTE_COLS), _F32).at[0, :MOE_GROUPS].set(b_group[l])
        br = br.at[0, MOE_GROUPS:MOE_GROUPS + N_EXPERTS].set(b_erouter[l])
        h2, ids, wts, cnt = _router(x, g_ffn[l], main[3], main[4], wr, br)
        if update_ctx:
            routed_c = _router(xc, g_ffn[l], cmod[3], cmod[4], wr, br)
            h2, ids, wts, cnt = (jnp.concatenate(pair, axis=0) for pair in zip((h2, ids, wts, cnt), routed_c))
        plan, blk_e, n_used, n_blocks = _dispatch_plan(cnt[:, 0, :N_EXPERTS])
        off, cnt = plan[:2]
        xs, pos = _dispatch(h2, ids, plan, n_blocks)
        ys = _experts(xs, blk_e, n_used, w1[l], w3[l], w2[l])
        final = l == DEPTH - 1
        x = _combine(x, ys, pos, wts, off, cnt, main[5], g_final, 0, final)
        if update_ctx:
            xc = _combine(xc, ys, pos, wts, off, cnt, cmod[5], g_final, t_main, False)
    return x
```

```python
import functools

import numpy as np
import jax
import jax.numpy as jnp
from jax import lax
from jax.experimental import pallas as pl
from jax.experimental.pallas import tpu as pltpu

D_MODEL = 1024
DEPTH = 2
GRID_W = 64
CHUNK = 128
A_GROUP_DIM = 128
A_GROUPS = D_MODEL // A_GROUP_DIM
D_A = A_GROUPS * A_GROUP_DIM
R_HEADS = 4
R_QK_DIM = D_MODEL // 8
R_V_DIM = D_MODEL // 4
D_RQK = R_HEADS * R_QK_DIM
D_RV = R_HEADS * R_V_DIM
D_IN = 2 * D_A + 2 * D_RQK + 2 * D_RV + 2 * D_MODEL
KV_START = 2 * D_A + D_RQK
KV_END = KV_START + D_RQK + D_RV
ROPE_BASE = 10000.0
MOE_GROUPS = 4
EXPERTS_PER_GROUP = 8
N_EXPERTS = MOE_GROUPS * EXPERTS_PER_GROUP
EXPERT_FF = D_MODEL // 2
EPS = 1e-6

LANES = 128
SUBLANES = 8
MOE_ROWS = 512
MOE_HALF = MOE_ROWS // 2
VMEM_LIMIT = 56 * 1024 * 1024
COND_ROWS = 16

_U_BLK, _VA_BLK = 0, 1
_Q_BLK, _K_BLK = 2 * D_A // D_RQK, 2 * D_A // D_RQK + 1
_VR_BLK = (2 * D_A + 2 * D_RQK) // D_RV
_GR_BLK, _GA_BLK, _GB_BLK = _VR_BLK + 1, _VR_BLK + 2, _VR_BLK + 3

_F32 = jnp.float32
_BF16 = jnp.bfloat16


def _cparams(*sem):
    return pltpu.CompilerParams(dimension_semantics=sem, vmem_limit_bytes=VMEM_LIMIT)


def _dot(a, b):
    return jnp.dot(a, b, preferred_element_type=_F32)


def _adaln_kernel(cond_ref, w_ref, b_ref, o_ref):
    cnd = cond_ref[...]
    s = cnd * jax.nn.sigmoid(cnd)
    o_ref[0] = jnp.dot(s, w_ref[0], preferred_element_type=_F32, precision=lax.Precision.HIGHEST) + b_ref[0]


def _adaln(cond, w_ada, b_ada):
    depth, d, d6 = w_ada.shape
    tn = 1024
    return pl.pallas_call(
        _adaln_kernel,
        grid=(depth, d6 // tn),
        in_specs=[
            pl.BlockSpec((COND_ROWS, d), lambda l, j: (0, 0)),
            pl.BlockSpec((1, d, tn), lambda l, j: (l, 0, j)),
            pl.BlockSpec((1, 1, tn), lambda l, j: (l, 0, j)),
        ],
        out_specs=pl.BlockSpec((1, COND_ROWS, tn), lambda l, j: (l, 0, j)),
        out_shape=jax.ShapeDtypeStruct((depth, COND_ROWS, d6), _F32),
        compiler_params=_cparams("parallel", "parallel"),
        name="adaln",
    )(cond, w_ada, b_ada.reshape(depth, 1, d6))


def _modnorm(x, g, shift, scale):
    y = x * lax.rsqrt(jnp.mean(x * x, axis=-1, keepdims=True) + EPS)
    return (y * g) * (1.0 + scale) + shift


def _layernorm(t):
    mu = jnp.mean(t, axis=-1, keepdims=True)
    tc = t - mu
    return tc * lax.rsqrt(jnp.mean(tc * tc, axis=-1, keepdims=True) + EPS)


def _rotate_heads(t, cos_i, sin_i, scale):
    even = lax.broadcasted_iota(jnp.int32, cos_i.shape, 1) % 2 == 0
    out = []
    for h in range(t.shape[1] // R_QK_DIM):
        th = t[:, h * R_QK_DIM:(h + 1) * R_QK_DIM]
        partner = jnp.where(even, pltpu.roll(th, R_QK_DIM - 1, 1), pltpu.roll(th, 1, 1))
        out.append((th * cos_i + partner * sin_i) * scale)
    return jnp.concatenate(out, axis=1)


def _proj_kernel(segments, x_ref, g_ref, sh_ref, sc_ref, gsgu_ref, cos_ref, sin_ref, w_ref, o_ref, h_ref):
    h_ref[...] = _modnorm(x_ref[0], g_ref[...], sh_ref[0], sc_ref[0]).astype(_BF16)
    for start, width, kind in segments:
        cols = slice(start, start + width)
        acc = _dot(h_ref[...], w_ref[:, cols])
        if kind == "gelu":
            acc = jax.nn.gelu(acc)
        elif kind == "gelu_ln":
            acc = _layernorm(jax.nn.gelu(acc)) * gsgu_ref[...]
        elif kind == "rot_q":
            acc = _rotate_heads(acc, cos_ref[...], sin_ref[...], 1.0)
        elif kind == "rot_k":
            acc = _rotate_heads(acc, cos_ref[...], sin_ref[...], R_QK_DIM ** -0.5)
        elif kind == "silu":
            acc = acc * jax.nn.sigmoid(acc)
        elif kind == "sigmoid":
            acc = jax.nn.sigmoid(acc)
        o_ref[0, :, cols] = acc.astype(_BF16)


_MIX_SEGMENTS = ((0, D_A, "gelu"), (D_A, D_A, "gelu_ln"), (2 * D_A, D_RQK, "rot_q"), (2 * D_A + D_RQK, D_RQK, "rot_k"),
                 (KV_START + D_RQK, D_RV, "plain"), (KV_END, D_RV, "silu"), (KV_END + D_RV, D_MODEL, "sigmoid"),
                 (KV_END + D_RV + D_MODEL, D_MODEL, "sigmoid"))
_VK_SEGMENTS = ((0, D_RV, "plain"), (D_RV, D_RQK, "rot_k"))


def _proj_in(x, g, shift, scale, gsgu, cos_i, sin_i, w, segments):
    b, n, d = x.shape
    dout = w.shape[1]
    tm = min(n, 512)
    return pl.pallas_call(
        functools.partial(_proj_kernel, segments),
        grid=(b, n // tm),
        in_specs=[
            pl.BlockSpec((1, tm, d), lambda i, r: (i, r, 0)),
            pl.BlockSpec((1, d), lambda i, r: (0, 0)),
            pl.BlockSpec((1, 1, d), lambda i, r: (i, 0, 0)),
            pl.BlockSpec((1, 1, d), lambda i, r: (i, 0, 0)),
            pl.BlockSpec((1, D_A), lambda i, r: (0, 0)),
            pl.BlockSpec((tm, LANES), lambda i, r: (r, 0)),
            pl.BlockSpec((tm, LANES), lambda i, r: (r, 0)),
            pl.BlockSpec((d, dout), lambda i, r: (0, 0), pipeline_mode=pl.Buffered(1)),
        ],
        out_specs=pl.BlockSpec((1, tm, dout), lambda i, r: (i, r, 0)),
        out_shape=jax.ShapeDtypeStruct((b, n, dout), _BF16),
        scratch_shapes=[pltpu.VMEM((tm, d), _BF16)],
        compiler_params=_cparams("parallel", "parallel"),
        name="proj_in",
    )(x, g.reshape(1, d), shift, scale, gsgu, cos_i, sin_i, w)


def _dot_t(a, b):
    return lax.dot_general(a, b, (((0,), (0,)), ((), ())), preferred_element_type=_F32)


def _ctx_state_kernel(k_ref, v_ref, wf_ref, wb_ref, sf_ref, sb_ref):
    for h in range(R_HEADS):
        kh = k_ref[0, :, h * R_QK_DIM:(h + 1) * R_QK_DIM].astype(_F32)
        vh = v_ref[0, :, h * R_V_DIM:(h + 1) * R_V_DIM]
        sf_ref[0, h] = _dot_t((kh * wf_ref[h]).astype(_BF16), vh)
        sb_ref[0, h] = _dot_t((kh * wb_ref[h]).astype(_BF16), vh)


def _ctx_states(pc, k_blk, v_blk, wf, wb):
    b, l, _ = pc.shape
    st = jax.ShapeDtypeStruct((b, R_HEADS, R_QK_DIM, R_V_DIM), _F32)
    st_spec = pl.BlockSpec((1, R_HEADS, R_QK_DIM, R_V_DIM), lambda i: (i, 0, 0, 0))
    return pl.pallas_call(
        _ctx_state_kernel,
        grid=(b,),
        in_specs=[
            pl.BlockSpec((1, l, D_RQK), lambda i: (i, 0, k_blk)),
            pl.BlockSpec((1, l, D_RV), lambda i: (i, 0, v_blk)),
            pl.BlockSpec((R_HEADS, l, LANES), lambda i: (0, 0, 0)),
            pl.BlockSpec((R_HEADS, l, LANES), lambda i: (0, 0, 0)),
        ],
        out_specs=[st_spec, st_spec],
        out_shape=[st, st],
        compiler_params=_cparams("parallel"),
        name="ctx_states",
    )(pc, pc, wf, wb)


def _bwd_state_kernel(n_chunks, cdec_ref, k_ref, v_ref, kdb_ref, s0_ref, o_ref, s_ref):
    @pl.when(pl.program_id(1) == 0)
    def _():
        s_ref[...] = s0_ref[0]

    def chunk(step, carry):
        c = n_chunks - 1 - step
        rows = pl.ds(pl.multiple_of(c * CHUNK, CHUNK), CHUNK)
        for h in range(R_HEADS):
            s = s_ref[h]
            o_ref[0, c, h] = s.astype(_BF16)
            kh = k_ref[0, rows, h * R_QK_DIM:(h + 1) * R_QK_DIM].astype(_F32)
            vh = v_ref[0, rows, h * R_V_DIM:(h + 1) * R_V_DIM]
            s_ref[h] = s * cdec_ref[1, h] + _dot_t((kh * kdb_ref[h]).astype(_BF16), vh)
        return carry

    lax.fori_loop(0, n_chunks, chunk, 0)


def _bwd_states(p, kdb, cdec, s0_b):
    b, n, _ = p.shape
    rows = min(n, 1024)
    ncb, nblk = rows // CHUNK, n // rows
    rev = lambda r: nblk - 1 - r
    return pl.pallas_call(
        functools.partial(_bwd_state_kernel, ncb),
        grid=(b, nblk),
        in_specs=[
            pl.BlockSpec(memory_space=pltpu.SMEM),
            pl.BlockSpec((1, rows, D_RQK), lambda i, r: (i, rev(r), _K_BLK)),
            pl.BlockSpec((1, rows, D_RV), lambda i, r: (i, rev(r), _VR_BLK)),
            pl.BlockSpec((R_HEADS, CHUNK, LANES), lambda i, r: (0, 0, 0)),
            pl.BlockSpec((1, R_HEADS, R_QK_DIM, R_V_DIM), lambda i, r: (i, 0, 0, 0)),
        ],
        out_specs=pl.BlockSpec((1, ncb, R_HEADS, R_QK_DIM, R_V_DIM), lambda i, r: (i, rev(r), 0, 0, 0)),
        out_shape=jax.ShapeDtypeStruct((b, n // CHUNK, R_HEADS, R_QK_DIM, R_V_DIM), _BF16),
        scratch_shapes=[pltpu.VMEM((R_HEADS, R_QK_DIM, R_V_DIM), _F32)],
        compiler_params=_cparams("parallel", "arbitrary"),
        name="bwd_states",
    )(cdec, p, p, kdb, s0_b)


def _mix_kernel(n_chunks, cdec_ref, u_ref, va_ref, q_ref, k_ref, vr_ref, gr_ref, ga_ref, gb_ref, x_ref, gt_ref,
                dmat_ref, qdf_ref, qdb_ref, kdf_ref, sb_ref, s0_ref,
                ws_ref, bs_ref, wpa_ref, wpb_ref, wout_ref, o_ref, sf_ref, ya_ref, yb_ref):
    @pl.when(pl.program_id(1) == 0)
    def _():
        sf_ref[...] = s0_ref[0]

    def chunk(c, carry):
        r0 = pl.multiple_of(c * CHUNK, CHUNK)
        rows = pl.ds(r0, CHUNK)
        gu = u_ref[0, rows, :].astype(_F32)
        vn = va_ref[0, rows, :]
        for g in range(A_GROUPS):
            cols = slice(g * A_GROUP_DIM, (g + 1) * A_GROUP_DIM)
            s = _dot(ws_ref[g], vn[:, cols]) + bs_ref[g]
            ya_ref[rows, cols] = (gu[:, cols] * s).astype(_BF16)
        for h in range(R_HEADS):
            qk = slice(h * R_QK_DIM, (h + 1) * R_QK_DIM)
            vv = slice(h * R_V_DIM, (h + 1) * R_V_DIM)
            qb, kb = q_ref[0, rows, qk], k_ref[0, rows, qk]
            qh, kh = qb.astype(_F32), kb.astype(_F32)
            vh = vr_ref[0, rows, vv]
            scores = lax.dot_general(qb, kb, (((1,), (1,)), ((), ())), preferred_element_type=_F32) * dmat_ref[h]
            o = _dot(scores.astype(_BF16), vh)
            s_f = sf_ref[h]
            o = o + _dot((qh * qdf_ref[h]).astype(_BF16), s_f.astype(_BF16))
            o = o + _dot((qh * qdb_ref[h]).astype(_BF16), sb_ref[0, c, h])
            yb_ref[rows, vv] = (gr_ref[0, rows, vv].astype(_F32) * _layernorm(o)).astype(_BF16)
            sf_ref[h] = s_f * cdec_ref[0, h] + _dot_t((kh * kdf_ref[h]).astype(_BF16), vh)
        return carry

    lax.fori_loop(0, n_chunks, chunk, 0)
    a = _dot(ya_ref[...], wpa_ref[...])
    b = _dot(yb_ref[...], wpb_ref[...])
    merged = ga_ref[0].astype(_F32) * a + gb_ref[0].astype(_F32) * b
    o_ref[0] = x_ref[0] + gt_ref[0] * _dot(merged.astype(_BF16), wout_ref[...])


def _token_mix(p, x, gt, tabs, sb, s0_f, ws, bsb, wpa, wpb, wout):
    b, n, d = x.shape
    rows = min(n, 512)
    ncb = rows // CHUNK
    pspec = lambda width, blk: pl.BlockSpec((1, rows, width), lambda i, r: (i, r, blk))
    const = lambda shape: pl.BlockSpec(shape, lambda i, r: (0,) * len(shape))
    tab = const((R_HEADS, CHUNK, LANES))
    return pl.pallas_call(
        functools.partial(_mix_kernel, ncb),
        grid=(b, n // rows),
        in_specs=[
            pl.BlockSpec(memory_space=pltpu.SMEM),
            pspec(D_A, _U_BLK), pspec(D_A, _VA_BLK), pspec(D_RQK, _Q_BLK), pspec(D_RQK, _K_BLK),
            pspec(D_RV, _VR_BLK), pspec(D_RV, _GR_BLK), pspec(d, _GA_BLK), pspec(d, _GB_BLK),
            pl.BlockSpec((1, rows, d), lambda i, r: (i, r, 0)),
            pl.BlockSpec((1, 1, d), lambda i, r: (i, 0, 0)),
            tab, tab, tab, tab,
            pl.BlockSpec((1, ncb, R_HEADS, R_QK_DIM, R_V_DIM), lambda i, r: (i, r, 0, 0, 0)),
            pl.BlockSpec((1, R_HEADS, R_QK_DIM, R_V_DIM), lambda i, r: (i, 0, 0, 0)),
            const((A_GROUPS, CHUNK, CHUNK)), const((A_GROUPS, CHUNK, LANES)),
            const((D_A, d)), const((D_RV, d)), const((d, d)),
        ],
        out_specs=pl.BlockSpec((1, rows, d), lambda i, r: (i, r, 0)),
        out_shape=jax.ShapeDtypeStruct((b, n, d), _F32),
        scratch_shapes=[
            pltpu.VMEM((R_HEADS, R_QK_DIM, R_V_DIM), _F32),
            pltpu.VMEM((rows, D_A), _BF16),
            pltpu.VMEM((rows, D_RV), _BF16),
        ],
        compiler_params=_cparams("parallel", "arbitrary"),
        name="token_mix",
    )(tabs["cdec"], p, p, p, p, p, p, p, p, x, gt, tabs["dmat"], tabs["qdf"], tabs["qdb"], tabs["kdf"],
      sb, s0_f, ws, bsb, wpa, wpb, wout)


ROUTE_COLS = LANES
MOE_TILE = 256
MOE_SLOTS = 2 * MOE_TILE
SLOT_SUBROWS = D_MODEL // LANES


def _router_kernel(n_main, x_ref, xc_ref, g_ref, sh_ref, sc_ref, wr_ref, br_ref, h_ref, id_ref, wt_ref, cnt_ref):
    x = x_ref[...] if xc_ref is None else jnp.where(pl.program_id(0) < n_main, x_ref[...], xc_ref[...])
    h = _modnorm(x, g_ref[...], sh_ref[0], sc_ref[0])
    h_hi = h.astype(_BF16)
    h_ref[...] = h_hi
    h_lo = (h - h_hi.astype(_F32)).astype(_BF16)
    r_hi = _dot(h_hi, wr_ref[...])
    logits = r_hi + pltpu.roll(r_hi, ROUTE_COLS // 2, 1) + _dot(h_lo, wr_ref[...]) + br_ref[...]
    lane = lax.broadcasted_iota(jnp.int32, logits.shape, 1)
    neg = -jnp.inf
    gl = jnp.where(lane < MOE_GROUPS, logits, neg)
    g_top = jnp.max(gl, axis=-1, keepdims=True)
    g_sel = jnp.min(jnp.where(gl == g_top, lane, ROUTE_COLS), axis=-1, keepdims=True)
    lse = g_top + jnp.log(jnp.sum(jnp.exp(gl - g_top), axis=-1, keepdims=True))
    p_group = jnp.exp(g_top - lse)
    lo = MOE_GROUPS + g_sel * EXPERTS_PER_GROUP
    el = jnp.where((lane >= lo) & (lane < lo + EXPERTS_PER_GROUP), logits, neg)
    e1 = jnp.max(el, axis=-1, keepdims=True)
    i1 = jnp.min(jnp.where(el == e1, lane, ROUTE_COLS), axis=-1, keepdims=True)
    el2 = jnp.where(lane == i1, neg, el)
    e2 = jnp.max(el2, axis=-1, keepdims=True)
    i2 = jnp.min(jnp.where(el2 == e2, lane, ROUTE_COLS), axis=-1, keepdims=True)
    x2 = jnp.exp(e2 - e1)
    den = 1.0 + x2
    id1, id2 = i1 - MOE_GROUPS, i2 - MOE_GROUPS
    id_ref[...] = jnp.where(lane == 0, id1, jnp.where(lane == 1, id2, 0))
    wt_ref[...] = jnp.where(lane == 0, p_group * (1.0 / den), jnp.where(lane == 1, p_group * (x2 / den), 0.0))
    hits = jnp.where((lane == id1) | (lane == id2), 1, 0)
    for t in range(cnt_ref.shape[0]):
        tile_hits = jnp.sum(hits[t * MOE_TILE:(t + 1) * MOE_TILE], axis=0, keepdims=True)
        cnt_ref[t] = jnp.broadcast_to(tile_hits, (SUBLANES, ROUTE_COLS))


def _router(x, xc, g, shift, scale, wr, br):
    b, n, d = x.shape
    tm = 512
    per_b = n // tm
    n_main = b * per_b
    n_ctx = 0 if xc is None else xc.shape[0] * xc.shape[1] // tm
    t_all = (n_main + n_ctx) * tm
    tiles = tm // MOE_TILE
    row = lambda f: pl.BlockSpec((tm, d), lambda i: (f(i), 0))
    mod = pl.BlockSpec((1, 1, d), lambda i: (jnp.where(i < n_main, i // per_b, b), 0, 0))
    lane_out = pl.BlockSpec((tm, ROUTE_COLS), lambda i: (i, 0))
    streams = [x.reshape(b * n, d)] + ([] if xc is None else [xc.reshape(-1, d)])
    stream_specs = [row(lambda i: jnp.minimum(i, n_main - 1))] + ([] if xc is None else [row(lambda i: jnp.maximum(i - n_main, 0))])
    body = functools.partial(_router_kernel, n_main) if xc is not None else (
        lambda x_ref, *rest: _router_kernel(n_main, x_ref, None, *rest))
    return pl.pallas_call(
        body,
        grid=(n_main + n_ctx,),
        in_specs=stream_specs + [pl.BlockSpec((1, d), lambda i: (0, 0)), mod, mod,
                                 pl.BlockSpec((d, ROUTE_COLS), lambda i: (0, 0)),
                                 pl.BlockSpec((1, ROUTE_COLS), lambda i: (0, 0))],
        out_specs=[row(lambda i: i), lane_out, lane_out,
                   pl.BlockSpec((tiles, SUBLANES, ROUTE_COLS), lambda i: (i, 0, 0))],
        out_shape=[jax.ShapeDtypeStruct((t_all, d), _BF16), jax.ShapeDtypeStruct((t_all, ROUTE_COLS), jnp.int32),
                   jax.ShapeDtypeStruct((t_all, ROUTE_COLS), _F32),
                   jax.ShapeDtypeStruct((t_all // MOE_TILE, SUBLANES, ROUTE_COLS), jnp.int32)],
        compiler_params=_cparams("parallel"),
        name="router",
    )(*streams, g.reshape(1, d), shift, scale, wr, br)


def _dispatch_plan(cnt):
    n_tiles = cnt.shape[0]
    a = n_tiles * MOE_SLOTS
    tot = jnp.sum(cnt, axis=0)
    padded = (tot + MOE_ROWS - 1) // MOE_ROWS * MOE_ROWS
    pend = jnp.cumsum(padded)
    pstart = pend - padded
    off = pstart[None, :] + jnp.cumsum(cnt, axis=0) - cnt
    n_blocks = (a + N_EXPERTS * (MOE_ROWS - 1) + MOE_ROWS - 1) // MOE_ROWS
    blk_start = jnp.arange(n_blocks, dtype=jnp.int32) * MOE_ROWS
    blk_e = jnp.minimum(jnp.sum((pend[None, :] <= blk_start[:, None]).astype(jnp.int32), axis=1), N_EXPERTS - 1)
    blk_rows = jnp.clip(tot[blk_e] - (blk_start - pstart[blk_e]), 0, MOE_ROWS)
    pad_start = jnp.concatenate([pstart + tot, blk_start])
    pad_len = jnp.concatenate([padded - tot, jnp.where(blk_start >= pend[-1], MOE_ROWS, 0)])
    i32 = lambda v: v.astype(jnp.int32)
    return (i32(off).reshape(-1), i32(cnt).reshape(-1), i32(pad_start), i32(pad_len)), i32(blk_e), i32(blk_rows), n_blocks


def _copy_runs(cnt_ref, off_ref, tile, make_copy):
    def body(e, src):
        n = cnt_ref[tile * N_EXPERTS + e]

        @pl.when(n > 0)
        def _():
            make_copy(src, off_ref[tile * N_EXPERTS + e], n).start()
        return src + n
    lax.fori_loop(0, N_EXPERTS, body, 0)


def _slot_rows(first, n):
    s = SLOT_SUBROWS
    return pl.ds(pl.multiple_of(first * s, s), pl.multiple_of(n * s, s))


def _load_slot_major(ref, n_slots):
    return jnp.concatenate([ref[pl.ds(s, n_slots, stride=SLOT_SUBROWS), :] for s in range(SLOT_SUBROWS)], axis=1)


def _store_slot_major(ref, val):
    for s in range(SLOT_SUBROWS):
        ref[pl.ds(s, val.shape[0], stride=SLOT_SUBROWS), :] = val[:, s * LANES:(s + 1) * LANES]


def _wait_slot(stage, sem, s):
    pltpu.make_async_copy(stage.at[s], stage.at[s], sem.at[s]).wait()


def _dispatch_kernel(off_ref, cnt_ref, pad_start_ref, pad_len_ref, h_ref, id_ref, xs_hbm, pos_ref, stage, sem,
                     zeros, zsem):
    i = pl.program_id(0)
    slot = i % 2

    @pl.when(i == 0)
    def _():
        zeros[...] = jnp.zeros(zeros.shape, zeros.dtype)

        def pad_copy(r):
            n = pad_len_ref[r]
            return n, pltpu.make_async_copy(zeros.at[_slot_rows(0, n)], xs_hbm.at[_slot_rows(pad_start_ref[r], n)], zsem)

        def start(r, carry):
            n, copy = pad_copy(r)

            @pl.when(n > 0)
            def _():
                copy.start()
            return carry

        def wait(r, carry):
            n, copy = pad_copy(r)

            @pl.when(n > 0)
            def _():
                copy.wait()
            return carry

        lax.fori_loop(0, pad_len_ref.shape[0], start, 0)
        lax.fori_loop(0, pad_len_ref.shape[0], wait, 0)

    ids = id_ref[...]
    lane = lax.broadcasted_iota(jnp.int32, ids.shape, 1)
    e0 = jnp.where(lane == ids[:, 0:1], 1.0, 0.0)
    e1 = jnp.where(lane == ids[:, 1:2], 1.0, 0.0)
    tt = lax.broadcasted_iota(jnp.int32, (MOE_TILE, MOE_TILE), 0)
    earlier = jnp.where(lax.broadcasted_iota(jnp.int32, (MOE_TILE, MOE_TILE), 1) < tt, 1.0, 0.0).astype(_BF16)
    c0 = _dot(earlier, e0.astype(_BF16))
    c1 = _dot(earlier, e1.astype(_BF16))
    tot0 = jnp.sum(e0, axis=0, keepdims=True)
    ee = lax.broadcasted_iota(jnp.int32, (ROUTE_COLS, ROUTE_COLS), 0)
    lower = jnp.where(ee < lax.broadcasted_iota(jnp.int32, (ROUTE_COLS, ROUTE_COLS), 1), 1.0, 0.0).astype(_BF16)
    base = jnp.sum(_dot((e0 + e1).astype(_BF16), lower), axis=0, keepdims=True)
    pos0 = jnp.sum(e0 * (base + c0), axis=1, keepdims=True)
    pos1 = jnp.sum(e1 * (base + tot0 + c1), axis=1, keepdims=True)
    pos = jnp.where(lane == 0, pos0, jnp.where(lane == 1, pos1, 0.0))
    pos_ref[...] = pos
    pos_t = pos.T
    slot_id = lax.broadcasted_iota(jnp.int32, (MOE_SLOTS, MOE_TILE), 0).astype(_F32)
    pick = jnp.where((slot_id == pos_t[0:1, :]) | (slot_id == pos_t[1:2, :]), 1.0, 0.0).astype(_BF16)
    rows = _dot(pick, h_ref[...])

    @pl.when(i >= 2)
    def _():
        _wait_slot(stage, sem, slot)

    _store_slot_major(stage.at[slot], rows)
    _copy_runs(cnt_ref, off_ref, i, lambda src, dst, n: pltpu.make_async_copy(
        stage.at[slot, _slot_rows(src, n)], xs_hbm.at[_slot_rows(dst, n)], sem.at[slot]))

    @pl.when(i == pl.num_programs(0) - 1)
    def _():
        _wait_slot(stage, sem, slot)

        @pl.when(i >= 1)
        def _():
            _wait_slot(stage, sem, 1 - slot)


def _dispatch(h2, ids, plan, n_blocks):
    t, d = h2.shape
    n_tiles = t // MOE_TILE
    grid_spec = pltpu.PrefetchScalarGridSpec(
        num_scalar_prefetch=4,
        grid=(n_tiles,),
        in_specs=[pl.BlockSpec((MOE_TILE, d), lambda i, *_: (i, 0)),
                  pl.BlockSpec((MOE_TILE, ROUTE_COLS), lambda i, *_: (i, 0))],
        out_specs=[pl.BlockSpec(memory_space=pl.ANY), pl.BlockSpec((MOE_TILE, ROUTE_COLS), lambda i, *_: (i, 0))],
        scratch_shapes=[pltpu.VMEM((2, MOE_SLOTS * SLOT_SUBROWS, LANES), _F32), pltpu.SemaphoreType.DMA((2,)),
                        pltpu.VMEM((MOE_ROWS * SLOT_SUBROWS, LANES), _F32), pltpu.SemaphoreType.DMA(())],
    )
    return pl.pallas_call(
        _dispatch_kernel,
        grid_spec=grid_spec,
        out_shape=[jax.ShapeDtypeStruct((n_blocks * MOE_ROWS * SLOT_SUBROWS, LANES), _F32),
                   jax.ShapeDtypeStruct((t, ROUTE_COLS), _F32)],
        compiler_params=_cparams("arbitrary"),
        name="moe_dispatch",
    )(*plan, h2, ids)


def _expert_kernel(blk_e_ref, rows_ref, x_ref, w1_ref, w3_ref, w2_ref, o_ref, w1b, w3b, w2b):
    i = pl.program_id(0)
    rows = rows_ref[i]

    @pl.when((i == 0) | (blk_e_ref[i] != blk_e_ref[jnp.maximum(i - 1, 0)]))
    def _():
        w1b[...] = w1_ref[0, 0].astype(_BF16)
        w3b[...] = w3_ref[0, 0].astype(_BF16)
        w2b[...] = w2_ref[0, 0].astype(_BF16)

    def ffn(first, n_slots):
        span = pl.ds(first * SLOT_SUBROWS, n_slots * SLOT_SUBROWS)
        xb = _load_slot_major(x_ref.at[span], n_slots).astype(_BF16)
        h1 = _dot(xb, w1b[...])
        h3 = _dot(xb, w3b[...])
        act = (h1 * jax.nn.sigmoid(h1) * h3).astype(_BF16)
        _store_slot_major(o_ref.at[span], _dot(act, w2b[...]))

    def clear(first, n_slots):
        o_ref[pl.ds(first * SLOT_SUBROWS, n_slots * SLOT_SUBROWS), :] = jnp.zeros((n_slots * SLOT_SUBROWS, LANES), _F32)

    @pl.when(rows > MOE_HALF)
    def _():
        ffn(0, MOE_ROWS)

    @pl.when((rows > 0) & (rows <= MOE_HALF))
    def _():
        ffn(0, MOE_HALF)
        clear(MOE_HALF, MOE_HALF)

    @pl.when(rows == 0)
    def _():
        clear(0, MOE_ROWS)


def _experts(xs, blk_e, blk_rows, layer, w1, w3, w2):
    d = w1.shape[2]
    blk = MOE_ROWS * SLOT_SUBROWS
    n_blocks = xs.shape[0] // blk
    grid_spec = pltpu.PrefetchScalarGridSpec(
        num_scalar_prefetch=2,
        grid=(n_blocks,),
        in_specs=[
            pl.BlockSpec((blk, LANES), lambda i, be, br: (i, 0)),
            pl.BlockSpec((1, 1, d, EXPERT_FF), lambda i, be, br: (layer, be[i], 0, 0)),
            pl.BlockSpec((1, 1, d, EXPERT_FF), lambda i, be, br: (layer, be[i], 0, 0)),
            pl.BlockSpec((1, 1, EXPERT_FF, d), lambda i, be, br: (layer, be[i], 0, 0)),
        ],
        out_specs=pl.BlockSpec((blk, LANES), lambda i, be, br: (i, 0)),
        scratch_shapes=[pltpu.VMEM((d, EXPERT_FF), _BF16), pltpu.VMEM((d, EXPERT_FF), _BF16),
                        pltpu.VMEM((EXPERT_FF, d), _BF16)],
    )
    return pl.pallas_call(
        _expert_kernel,
        grid_spec=grid_spec,
        out_shape=jax.ShapeDtypeStruct(xs.shape, _F32),
        compiler_params=_cparams("arbitrary"),
        name="experts",
    )(blk_e, blk_rows, xs, w1, w3, w2)


def _combine_kernel(final, tile0, off_ref, cnt_ref, x_ref, pos_ref, wt_ref, gt_ref, g_ref, ys_hbm, o_ref, stage, sem):
    i = pl.program_id(0)
    slot = i % 2

    def fetch(tile, s):
        _copy_runs(cnt_ref, off_ref, tile, lambda src, dst, n: pltpu.make_async_copy(
            ys_hbm.at[_slot_rows(dst, n)], stage.at[s, _slot_rows(src, n)], sem.at[s]))

    @pl.when(i == 0)
    def _():
        fetch(tile0, 0)

    @pl.when(i + 1 < pl.num_programs(0))
    def _():
        fetch(tile0 + i + 1, 1 - slot)

    _wait_slot(stage, sem, slot)
    y = _load_slot_major(stage.at[slot], MOE_SLOTS).astype(_BF16)
    slot_id = lax.broadcasted_iota(jnp.int32, (MOE_TILE, MOE_SLOTS), 1).astype(_F32)
    mix = (jnp.where(slot_id == pos_ref[:, 0:1], wt_ref[:, 0:1], 0.0)
           + jnp.where(slot_id == pos_ref[:, 1:2], wt_ref[:, 1:2], 0.0)).astype(_BF16)
    x = x_ref[...] + gt_ref[0] * _dot(mix, y)
    if final:
        x = (x * lax.rsqrt(jnp.mean(x * x, axis=-1, keepdims=True) + EPS)) * g_ref[...]
    o_ref[...] = x


def _combine(x, ys, pos, wts, off, cnt, gt, g_final, tok0, final):
    b, n, d = x.shape
    t = b * n
    per_b = n // MOE_TILE
    tile0 = tok0 // MOE_TILE
    row = pl.BlockSpec((MOE_TILE, d), lambda i, o, c: (i, 0))
    lanes = pl.BlockSpec((MOE_TILE, ROUTE_COLS), lambda i, o, c: (tile0 + i, 0))
    grid_spec = pltpu.PrefetchScalarGridSpec(
        num_scalar_prefetch=2,
        grid=(t // MOE_TILE,),
        in_specs=[row, lanes, lanes, pl.BlockSpec((1, 1, d), lambda i, o, c: (i // per_b, 0, 0)),
                  pl.BlockSpec((1, d), lambda i, o, c: (0, 0)), pl.BlockSpec(memory_space=pl.ANY)],
        out_specs=row,
        scratch_shapes=[pltpu.VMEM((2, MOE_SLOTS * SLOT_SUBROWS, LANES), _F32), pltpu.SemaphoreType.DMA((2,))],
    )
    out = pl.pallas_call(
        functools.partial(_combine_kernel, final, tile0),
        grid_spec=grid_spec,
        out_shape=jax.ShapeDtypeStruct((t, d), _F32),
        compiler_params=_cparams("arbitrary"),
        name="moe_combine",
    )(off, cnt, x.reshape(t, d), pos, wts, gt, g_final.reshape(1, d), ys)
    return out.reshape(b, n, d)


def _rotary_tables(n):
    n_freq = R_QK_DIM // 4
    inv = ROPE_BASE ** (-jnp.arange(n_freq, dtype=_F32) / n_freq)
    rows = n // GRID_W
    row = jnp.repeat(jnp.arange(rows, dtype=_F32), GRID_W)
    col = jnp.tile(jnp.arange(GRID_W, dtype=_F32), rows)
    ang = jnp.concatenate([row[:, None] * inv, col[:, None] * inv], axis=-1)
    cos, sin = jnp.cos(ang), jnp.sin(ang)
    return jnp.repeat(cos, 2, axis=-1), jnp.stack([-sin, sin], axis=-1).reshape(n, R_QK_DIM)


def _decay_tables(decay_logit, l_ctx):
    log_g = jax.nn.log_sigmoid(decay_logit.astype(_F32))
    lgf, lgb = log_g[0][:, None], log_g[1][:, None]
    j = jnp.arange(CHUNK, dtype=_F32)
    diff = j[:, None] - j[None, :]
    dmat = jnp.where(diff[None] >= 0, jnp.exp(lgf[:, :, None] * jnp.maximum(diff, 0.0)[None]),
                     jnp.exp(lgb[:, :, None] * jnp.maximum(-diff, 0.0)[None]))
    lanes = lambda t: jnp.broadcast_to(t[:, :, None], t.shape + (LANES,))
    m = jnp.arange(l_ctx, dtype=_F32)
    return {
        "dmat": dmat,
        "qdf": lanes(jnp.exp(lgf * (j + 1.0))), "qdb": lanes(jnp.exp(lgb * (CHUNK - j))),
        "kdf": lanes(jnp.exp(lgf * (CHUNK - 1.0 - j))), "kdb": lanes(jnp.exp(lgb * j)),
        "cdec": jnp.exp(log_g * CHUNK),
        "ctx_f": lanes(jnp.exp(lgf * ((l_ctx - 1.0) - m))), "ctx_b": lanes(jnp.exp(lgb * m)),
    }


def kernel(x, c, ctx, c_ctx, w_ada, b_ada, g_mix, g_ffn, w_in, w_s, b_s, g_sgu, decay_logit, w_pa, w_pb, w_out,
           w_group, b_group, w_erouter, b_erouter, w1, w3, w2, g_final):
    bsz, n, d = x.shape
    l_ctx = ctx.shape[1]
    t_main = bsz * n

    cond = jnp.zeros((COND_ROWS, d), _F32).at[:bsz].set(c).at[bsz].set(c_ctx)
    mods = _adaln(cond, w_ada, b_ada)
    rot = _rotary_tables(n)
    no_rot = (jnp.ones((l_ctx, LANES), _F32), jnp.zeros((l_ctx, LANES), _F32))
    zero_state = jnp.zeros((bsz, R_HEADS, R_QK_DIM, R_V_DIM), _F32)

    xc = ctx
    for l in range(DEPTH):
        update_ctx = l < DEPTH - 1
        m6 = mods[l].reshape(COND_ROWS, 6, d)
        main = [m6[:bsz, i][:, None, :] for i in range(6)]
        cmod = [jnp.broadcast_to(m6[bsz, i][None, None, :], (bsz, 1, d)) for i in range(6)]
        tabs = _decay_tables(decay_logit[l], l_ctx)
        w_in_b = w_in[l].astype(_BF16)
        gsgu = g_sgu[l].reshape(1, D_A)
        mix_w = (w_s[l].astype(_BF16), jnp.broadcast_to(b_s[l][:, :, None], (A_GROUPS, CHUNK, LANES)),
                 w_pa[l].astype(_BF16), w_pb[l].astype(_BF16), w_out[l].astype(_BF16))

        p = _proj_in(x, g_mix[l], main[0], main[1], gsgu, *rot, w_in_b, _MIX_SEGMENTS)
        if update_ctx:
            pc = _proj_in(xc, g_mix[l], cmod[0], cmod[1], gsgu, *no_rot, w_in_b, _MIX_SEGMENTS)
            s_ctx_f, s_ctx_b = _ctx_states(pc, _K_BLK, _VR_BLK, tabs["ctx_f"], tabs["ctx_b"])
        else:
            w_vk = jnp.concatenate([w_in_b[:, KV_START + D_RQK:KV_END], w_in_b[:, KV_START:KV_START + D_RQK]], axis=1)
            vk_c = _proj_in(xc, g_mix[l], cmod[0], cmod[1], gsgu, *no_rot, w_vk, _VK_SEGMENTS)
            s_ctx_f, s_ctx_b = _ctx_states(vk_c, D_RV // D_RQK, 0, tabs["ctx_f"], tabs["ctx_b"])
        sb = _bwd_states(p, tabs["kdb"], tabs["cdec"], s_ctx_b)
        x = _token_mix(p, x, main[2], tabs, sb, s_ctx_f, *mix_w)
        if update_ctx:
            sbc = _bwd_states(pc, tabs["kdb"], tabs["cdec"], zero_state)
            xc = _token_mix(pc, xc, cmod[2], tabs, sbc, zero_state, *mix_w)

        wr = jnp.zeros((d, ROUTE_COLS // 2), _F32).at[:, :MOE_GROUPS].set(w_group[l])
        wr = wr.at[:, MOE_GROUPS:MOE_GROUPS + N_EXPERTS].set(w_erouter[l])
        wr_hi = wr.astype(_BF16)
        wr = jnp.concatenate([wr_hi, (wr - wr_hi.astype(_F32)).astype(_BF16)], axis=1)
        br = jnp.zeros((1, ROUTE_COLS), _F32).at[0, :MOE_GROUPS].set(b_group[l])
        br = br.at[0, MOE_GROUPS:MOE_GROUPS + N_EXPERTS].set(b_erouter[l])
        shift2, scale2 = (jnp.concatenate([main[i], cmod[i][:1]], axis=0) for i in (3, 4))
        h2, ids, wts, cnt = _router(x, xc if update_ctx else None, g_ffn[l], shift2, scale2, wr, br)
        plan, blk_e, blk_rows, n_blocks = _dispatch_plan(cnt[:, 0, :N_EXPERTS])
        off, cnt = plan[:2]
        xs, pos = _dispatch(h2, ids, plan, n_blocks)
        ys = _experts(xs, blk_e, blk_rows, l, w1, w3, w2)
        final = l == DEPTH - 1
        x = _combine(x, ys, pos, wts, off, cnt, main[5], g_final, 0, final)
        if update_ctx:
            xc = _combine(xc, ys, pos, wts, off, cnt, cmod[5], g_final, t_main, False)
    return x
```

```python
import functools

import numpy as np
import jax
import jax.numpy as jnp
from jax import lax
from jax.experimental import pallas as pl
from jax.experimental.pallas import tpu as pltpu

D_MODEL = 1024
DEPTH = 2
GRID_W = 64
CHUNK = 128
A_GROUP_DIM = 128
A_GROUPS = D_MODEL // A_GROUP_DIM
D_A = A_GROUPS * A_GROUP_DIM
R_HEADS = 4
R_QK_DIM = D_MODEL // 8
R_V_DIM = D_MODEL // 4
D_RQK = R_HEADS * R_QK_DIM
D_RV = R_HEADS * R_V_DIM
D_IN = 2 * D_A + 2 * D_RQK + 2 * D_RV + 2 * D_MODEL
KV_START = 2 * D_A + D_RQK
KV_END = KV_START + D_RQK + D_RV
ROPE_BASE = 10000.0
MOE_GROUPS = 4
EXPERTS_PER_GROUP = 8
N_EXPERTS = MOE_GROUPS * EXPERTS_PER_GROUP
EXPERT_FF = D_MODEL // 2
EPS = 1e-6

LANES = 128
SUBLANES = 8
MOE_ROWS = 512
MOE_HALF = MOE_ROWS // 2
VMEM_LIMIT = 56 * 1024 * 1024
COND_ROWS = 16

_U_BLK, _VA_BLK = 0, 1
_Q_BLK, _K_BLK = 2 * D_A // D_RQK, 2 * D_A // D_RQK + 1
_VR_BLK = (2 * D_A + 2 * D_RQK) // D_RV
_GR_BLK, _GA_BLK, _GB_BLK = _VR_BLK + 1, _VR_BLK + 2, _VR_BLK + 3

_F32 = jnp.float32
_BF16 = jnp.bfloat16


def _cparams(*sem):
    return pltpu.CompilerParams(dimension_semantics=sem, vmem_limit_bytes=VMEM_LIMIT)


def _dot(a, b):
    return jnp.dot(a, b, preferred_element_type=_F32)


def _adaln_kernel(cond_ref, w_ref, b_ref, o_ref):
    cnd = cond_ref[...]
    s = cnd * jax.nn.sigmoid(cnd)
    o_ref[0] = jnp.dot(s, w_ref[0], preferred_element_type=_F32, precision=lax.Precision.HIGHEST) + b_ref[0]


def _adaln(cond, w_ada, b_ada):
    depth, d, d6 = w_ada.shape
    tn = 1024
    return pl.pallas_call(
        _adaln_kernel,
        grid=(depth, d6 // tn),
        in_specs=[
            pl.BlockSpec((COND_ROWS, d), lambda l, j: (0, 0)),
            pl.BlockSpec((1, d, tn), lambda l, j: (l, 0, j)),
            pl.BlockSpec((1, 1, tn), lambda l, j: (l, 0, j)),
        ],
        out_specs=pl.BlockSpec((1, COND_ROWS, tn), lambda l, j: (l, 0, j)),
        out_shape=jax.ShapeDtypeStruct((depth, COND_ROWS, d6), _F32),
        compiler_params=_cparams("parallel", "parallel"),
        name="adaln",
    )(cond, w_ada, b_ada.reshape(depth, 1, d6))


def _modnorm(x, g, shift, scale):
    y = x * lax.rsqrt(jnp.mean(x * x, axis=-1, keepdims=True) + EPS)
    return (y * g) * (1.0 + scale) + shift


def _layernorm(t):
    mu = jnp.mean(t, axis=-1, keepdims=True)
    tc = t - mu
    return tc * lax.rsqrt(jnp.mean(tc * tc, axis=-1, keepdims=True) + EPS)


def _rotate_heads(t, cos_i, sin_i, scale):
    even = lax.broadcasted_iota(jnp.int32, cos_i.shape, 1) % 2 == 0
    out = []
    for h in range(t.shape[1] // R_QK_DIM):
        th = t[:, h * R_QK_DIM:(h + 1) * R_QK_DIM]
        partner = jnp.where(even, pltpu.roll(th, R_QK_DIM - 1, 1), pltpu.roll(th, 1, 1))
        out.append((th * cos_i + partner * sin_i) * scale)
    return jnp.concatenate(out, axis=1)


def _proj_kernel(segments, x_ref, g_ref, sh_ref, sc_ref, gsgu_ref, cos_ref, sin_ref, w_ref, o_ref, h_ref):
    h_ref[...] = _modnorm(x_ref[0], g_ref[...], sh_ref[0], sc_ref[0]).astype(_BF16)
    for start, width, kind in segments:
        cols = slice(start, start + width)
        acc = _dot(h_ref[...], w_ref[:, cols])
        if kind == "gelu":
            acc = jax.nn.gelu(acc)
        elif kind == "gelu_ln":
            acc = _layernorm(jax.nn.gelu(acc)) * gsgu_ref[...]
        elif kind == "rot_q":
            acc = _rotate_heads(acc, cos_ref[...], sin_ref[...], 1.0)
        elif kind == "rot_k":
            acc = _rotate_heads(acc, cos_ref[...], sin_ref[...], R_QK_DIM ** -0.5)
        elif kind == "silu":
            acc = acc * jax.nn.sigmoid(acc)
        elif kind == "sigmoid":
            acc = jax.nn.sigmoid(acc)
        o_ref[0, :, cols] = acc.astype(_BF16)


_MIX_SEGMENTS = ((0, D_A, "gelu"), (D_A, D_A, "gelu_ln"), (2 * D_A, D_RQK, "rot_q"), (2 * D_A + D_RQK, D_RQK, "rot_k"),
                 (KV_START + D_RQK, D_RV, "plain"), (KV_END, D_RV, "silu"), (KV_END + D_RV, D_MODEL, "sigmoid"),
                 (KV_END + D_RV + D_MODEL, D_MODEL, "sigmoid"))
_VK_SEGMENTS = ((0, D_RV, "plain"), (D_RV, D_RQK, "rot_k"))


def _proj_in(x, g, shift, scale, gsgu, cos_i, sin_i, w, segments):
    b, n, d = x.shape
    dout = w.shape[1]
    tm = min(n, 512)
    return pl.pallas_call(
        functools.partial(_proj_kernel, segments),
        grid=(b, n // tm),
        in_specs=[
            pl.BlockSpec((1, tm, d), lambda i, r: (i, r, 0)),
            pl.BlockSpec((1, d), lambda i, r: (0, 0)),
            pl.BlockSpec((1, 1, d), lambda i, r: (i, 0, 0)),
            pl.BlockSpec((1, 1, d), lambda i, r: (i, 0, 0)),
            pl.BlockSpec((1, D_A), lambda i, r: (0, 0)),
            pl.BlockSpec((tm, LANES), lambda i, r: (r, 0)),
            pl.BlockSpec((tm, LANES), lambda i, r: (r, 0)),
            pl.BlockSpec((d, dout), lambda i, r: (0, 0), pipeline_mode=pl.Buffered(1)),
        ],
        out_specs=pl.BlockSpec((1, tm, dout), lambda i, r: (i, r, 0)),
        out_shape=jax.ShapeDtypeStruct((b, n, dout), _BF16),
        scratch_shapes=[pltpu.VMEM((tm, d), _BF16)],
        compiler_params=_cparams("parallel", "parallel"),
        name="proj_in",
    )(x, g.reshape(1, d), shift, scale, gsgu, cos_i, sin_i, w)


def _dot_t(a, b):
    return lax.dot_general(a, b, (((0,), (0,)), ((), ())), preferred_element_type=_F32)


def _ctx_state_kernel(k_ref, v_ref, wf_ref, wb_ref, sf_ref, sb_ref):
    for h in range(R_HEADS):
        kh = k_ref[0, :, h * R_QK_DIM:(h + 1) * R_QK_DIM].astype(_F32)
        vh = v_ref[0, :, h * R_V_DIM:(h + 1) * R_V_DIM]
        sf_ref[0, h] = _dot_t((kh * wf_ref[h]).astype(_BF16), vh)
        sb_ref[0, h] = _dot_t((kh * wb_ref[h]).astype(_BF16), vh)


def _ctx_states(pc, k_blk, v_blk, wf, wb):
    b, l, _ = pc.shape
    st = jax.ShapeDtypeStruct((b, R_HEADS, R_QK_DIM, R_V_DIM), _F32)
    st_spec = pl.BlockSpec((1, R_HEADS, R_QK_DIM, R_V_DIM), lambda i: (i, 0, 0, 0))
    return pl.pallas_call(
        _ctx_state_kernel,
        grid=(b,),
        in_specs=[
            pl.BlockSpec((1, l, D_RQK), lambda i: (i, 0, k_blk)),
            pl.BlockSpec((1, l, D_RV), lambda i: (i, 0, v_blk)),
            pl.BlockSpec((R_HEADS, l, LANES), lambda i: (0, 0, 0)),
            pl.BlockSpec((R_HEADS, l, LANES), lambda i: (0, 0, 0)),
        ],
        out_specs=[st_spec, st_spec],
        out_shape=[st, st],
        compiler_params=_cparams("parallel"),
        name="ctx_states",
    )(pc, pc, wf, wb)


def _bwd_state_kernel(n_chunks, cdec_ref, k_ref, v_ref, kdb_ref, s0_ref, o_ref, s_ref):
    @pl.when(pl.program_id(1) == 0)
    def _():
        s_ref[...] = s0_ref[0]

    def chunk(step, carry):
        c = n_chunks - 1 - step
        rows = pl.ds(pl.multiple_of(c * CHUNK, CHUNK), CHUNK)
        for h in range(R_HEADS):
            s = s_ref[h]
            o_ref[0, c, h] = s.astype(_BF16)
            kh = k_ref[0, rows, h * R_QK_DIM:(h + 1) * R_QK_DIM].astype(_F32)
            vh = v_ref[0, rows, h * R_V_DIM:(h + 1) * R_V_DIM]
            s_ref[h] = s * cdec_ref[1, h] + _dot_t((kh * kdb_ref[h]).astype(_BF16), vh)
        return carry

    lax.fori_loop(0, n_chunks, chunk, 0)


def _bwd_states(p, kdb, cdec, s0_b):
    b, n, _ = p.shape
    rows = min(n, 1024)
    ncb, nblk = rows // CHUNK, n // rows
    rev = lambda r: nblk - 1 - r
    return pl.pallas_call(
        functools.partial(_bwd_state_kernel, ncb),
        grid=(b, nblk),
        in_specs=[
            pl.BlockSpec(memory_space=pltpu.SMEM),
            pl.BlockSpec((1, rows, D_RQK), lambda i, r: (i, rev(r), _K_BLK)),
            pl.BlockSpec((1, rows, D_RV), lambda i, r: (i, rev(r), _VR_BLK)),
            pl.BlockSpec((R_HEADS, CHUNK, LANES), lambda i, r: (0, 0, 0)),
            pl.BlockSpec((1, R_HEADS, R_QK_DIM, R_V_DIM), lambda i, r: (i, 0, 0, 0)),
        ],
        out_specs=pl.BlockSpec((1, ncb, R_HEADS, R_QK_DIM, R_V_DIM), lambda i, r: (i, rev(r), 0, 0, 0)),
        out_shape=jax.ShapeDtypeStruct((b, n // CHUNK, R_HEADS, R_QK_DIM, R_V_DIM), _BF16),
        scratch_shapes=[pltpu.VMEM((R_HEADS, R_QK_DIM, R_V_DIM), _F32)],
        compiler_params=_cparams("parallel", "arbitrary"),
        name="bwd_states",
    )(cdec, p, p, kdb, s0_b)


def _mix_kernel(n_chunks, cdec_ref, u_ref, va_ref, q_ref, k_ref, vr_ref, gr_ref, ga_ref, gb_ref, x_ref, gt_ref,
                dmat_ref, qdf_ref, qdb_ref, kdf_ref, sb_ref, s0_ref,
                ws_ref, bs_ref, wpa_ref, wpb_ref, wout_ref, g2_ref, sh2_ref, sc2_ref, wr_ref, br_ref,
                o_ref, h_ref, id_ref, wt_ref, cnt_ref, sf_ref, ya_ref, yb_ref):
    @pl.when(pl.program_id(1) == 0)
    def _():
        sf_ref[...] = s0_ref[0]

    for c in range(n_chunks):
        rows = pl.ds(c * CHUNK, CHUNK)
        gu = u_ref[0, rows, :].astype(_F32)
        vn = va_ref[0, rows, :]
        for g in range(A_GROUPS):
            cols = slice(g * A_GROUP_DIM, (g + 1) * A_GROUP_DIM)
            s = _dot(ws_ref[g], vn[:, cols]) + bs_ref[g]
            ya_ref[rows, cols] = (gu[:, cols] * s).astype(_BF16)
        for h in range(R_HEADS):
            qk = slice(h * R_QK_DIM, (h + 1) * R_QK_DIM)
            vv = slice(h * R_V_DIM, (h + 1) * R_V_DIM)
            qb, kb = q_ref[0, rows, qk], k_ref[0, rows, qk]
            qh, kh = qb.astype(_F32), kb.astype(_F32)
            vh = vr_ref[0, rows, vv]
            scores = lax.dot_general(qb, kb, (((1,), (1,)), ((), ())), preferred_element_type=_F32) * dmat_ref[h]
            o = _dot(scores.astype(_BF16), vh)
            s_f = sf_ref[h]
            o = o + _dot((qh * qdf_ref[h]).astype(_BF16), s_f.astype(_BF16))
            o = o + _dot((qh * qdb_ref[h]).astype(_BF16), sb_ref[0, c, h])
            yb_ref[rows, vv] = (gr_ref[0, rows, vv].astype(_F32) * _layernorm(o)).astype(_BF16)
            sf_ref[h] = s_f * cdec_ref[0, h] + _dot_t((kh * kdf_ref[h]).astype(_BF16), vh)

    a = _dot(ya_ref[...], wpa_ref[...])
    b = _dot(yb_ref[...], wpb_ref[...])
    merged = ga_ref[0].astype(_F32) * a + gb_ref[0].astype(_F32) * b
    x_new = x_ref[0] + gt_ref[0] * _dot(merged.astype(_BF16), wout_ref[...])
    o_ref[0] = x_new
    _route(x_new, g2_ref, sh2_ref, sc2_ref, wr_ref, br_ref, h_ref, id_ref, wt_ref, cnt_ref)


def _token_mix(p, x, gt, tabs, sb, s0_f, ws, bsb, wpa, wpb, wout, g2, shift2, scale2, wr, br):
    b, n, d = x.shape
    t = b * n
    rows = min(n, 512)
    ncb = rows // CHUNK
    per_b = n // rows
    tiles = rows // MOE_TILE
    flat = lambda width: pl.BlockSpec((rows, width), lambda i, r: (i * per_b + r, 0))
    pspec = lambda width, blk: pl.BlockSpec((1, rows, width), lambda i, r: (i, r, blk))
    const = lambda shape: pl.BlockSpec(shape, lambda i, r: (0,) * len(shape))
    tab = const((R_HEADS, CHUNK, LANES))
    return pl.pallas_call(
        functools.partial(_mix_kernel, ncb),
        grid=(b, n // rows),
        in_specs=[
            pl.BlockSpec(memory_space=pltpu.SMEM),
            pspec(D_A, _U_BLK), pspec(D_A, _VA_BLK), pspec(D_RQK, _Q_BLK), pspec(D_RQK, _K_BLK),
            pspec(D_RV, _VR_BLK), pspec(D_RV, _GR_BLK), pspec(d, _GA_BLK), pspec(d, _GB_BLK),
            pl.BlockSpec((1, rows, d), lambda i, r: (i, r, 0)),
            pl.BlockSpec((1, 1, d), lambda i, r: (i, 0, 0)),
            tab, tab, tab, tab,
            pl.BlockSpec((1, ncb, R_HEADS, R_QK_DIM, R_V_DIM), lambda i, r: (i, r, 0, 0, 0)),
            pl.BlockSpec((1, R_HEADS, R_QK_DIM, R_V_DIM), lambda i, r: (i, 0, 0, 0)),
            const((A_GROUPS, CHUNK, CHUNK)), const((A_GROUPS, CHUNK, LANES)),
            const((D_A, d)), const((D_RV, d)), const((d, d)),
            const((1, d)), pl.BlockSpec((1, 1, d), lambda i, r: (i, 0, 0)), pl.BlockSpec((1, 1, d), lambda i, r: (i, 0, 0)),
            const((d, ROUTE_COLS)), const((1, ROUTE_COLS)),
        ],
        out_specs=[pl.BlockSpec((1, rows, d), lambda i, r: (i, r, 0)), flat(d), flat(ROUTE_COLS), flat(ROUTE_COLS),
                   pl.BlockSpec((tiles, SUBLANES, ROUTE_COLS), lambda i, r: (i * per_b + r, 0, 0))],
        out_shape=[jax.ShapeDtypeStruct((b, n, d), _F32), jax.ShapeDtypeStruct((t, d), _BF16),
                   jax.ShapeDtypeStruct((t, ROUTE_COLS), jnp.int32), jax.ShapeDtypeStruct((t, ROUTE_COLS), _F32),
                   jax.ShapeDtypeStruct((t // MOE_TILE, SUBLANES, ROUTE_COLS), jnp.int32)],
        scratch_shapes=[
            pltpu.VMEM((R_HEADS, R_QK_DIM, R_V_DIM), _F32),
            pltpu.VMEM((rows, D_A), _BF16),
            pltpu.VMEM((rows, D_RV), _BF16),
        ],
        compiler_params=_cparams("parallel", "arbitrary"),
        name="token_mix",
    )(tabs["cdec"], p, p, p, p, p, p, p, p, x, gt, tabs["dmat"], tabs["qdf"], tabs["qdb"], tabs["kdf"],
      sb, s0_f, ws, bsb, wpa, wpb, wout, g2.reshape(1, d), shift2, scale2, wr, br)


ROUTE_COLS = LANES
MOE_TILE = 256
MOE_SLOTS = 2 * MOE_TILE
SLOT_SUBROWS = D_MODEL // LANES


def _route(x, g_ref, sh_ref, sc_ref, wr_ref, br_ref, h_ref, id_ref, wt_ref, cnt_ref):
    h = _modnorm(x, g_ref[...], sh_ref[0], sc_ref[0])
    h_hi = h.astype(_BF16)
    h_ref[...] = h_hi
    h_lo = (h - h_hi.astype(_F32)).astype(_BF16)
    r_hi = _dot(h_hi, wr_ref[...])
    logits = r_hi + pltpu.roll(r_hi, ROUTE_COLS // 2, 1) + _dot(h_lo, wr_ref[...]) + br_ref[...]
    lane = lax.broadcasted_iota(jnp.int32, logits.shape, 1)
    neg = -jnp.inf
    gl = jnp.where(lane < MOE_GROUPS, logits, neg)
    g_top = jnp.max(gl, axis=-1, keepdims=True)
    g_sel = jnp.min(jnp.where(gl == g_top, lane, ROUTE_COLS), axis=-1, keepdims=True)
    lse = g_top + jnp.log(jnp.sum(jnp.exp(gl - g_top), axis=-1, keepdims=True))
    p_group = jnp.exp(g_top - lse)
    lo = MOE_GROUPS + g_sel * EXPERTS_PER_GROUP
    el = jnp.where((lane >= lo) & (lane < lo + EXPERTS_PER_GROUP), logits, neg)
    e1 = jnp.max(el, axis=-1, keepdims=True)
    i1 = jnp.min(jnp.where(el == e1, lane, ROUTE_COLS), axis=-1, keepdims=True)
    el2 = jnp.where(lane == i1, neg, el)
    e2 = jnp.max(el2, axis=-1, keepdims=True)
    i2 = jnp.min(jnp.where(el2 == e2, lane, ROUTE_COLS), axis=-1, keepdims=True)
    x2 = jnp.exp(e2 - e1)
    den = 1.0 + x2
    id1, id2 = i1 - MOE_GROUPS, i2 - MOE_GROUPS
    id_ref[...] = jnp.where(lane == 0, id1, jnp.where(lane == 1, id2, 0))
    wt_ref[...] = jnp.where(lane == 0, p_group * (1.0 / den), jnp.where(lane == 1, p_group * (x2 / den), 0.0))
    hits = jnp.where((lane == id1) | (lane == id2), 1, 0)
    for t in range(cnt_ref.shape[0]):
        tile_hits = jnp.sum(hits[t * MOE_TILE:(t + 1) * MOE_TILE], axis=0, keepdims=True)
        cnt_ref[t] = jnp.broadcast_to(tile_hits, (SUBLANES, ROUTE_COLS))


def _dispatch_plan(cnt):
    n_tiles = cnt.shape[0]
    a = n_tiles * MOE_SLOTS
    tot = jnp.sum(cnt, axis=0)
    padded = (tot + MOE_ROWS - 1) // MOE_ROWS * MOE_ROWS
    pend = jnp.cumsum(padded)
    pstart = pend - padded
    off = pstart[None, :] + jnp.cumsum(cnt, axis=0) - cnt
    n_blocks = (a + N_EXPERTS * (MOE_ROWS - 1) + MOE_ROWS - 1) // MOE_ROWS
    blk_start = jnp.arange(n_blocks, dtype=jnp.int32) * MOE_ROWS
    blk_e = jnp.minimum(jnp.sum((pend[None, :] <= blk_start[:, None]).astype(jnp.int32), axis=1), N_EXPERTS - 1)
    blk_rows = jnp.clip(tot[blk_e] - (blk_start - pstart[blk_e]), 0, MOE_ROWS)
    pad_start = jnp.concatenate([pstart + tot, blk_start])
    pad_len = jnp.concatenate([padded - tot, jnp.where(blk_start >= pend[-1], MOE_ROWS, 0)])
    i32 = lambda v: v.astype(jnp.int32)
    return (i32(off).reshape(-1), i32(cnt).reshape(-1), i32(pad_start), i32(pad_len)), i32(blk_e), i32(blk_rows), n_blocks


def _copy_runs(cnt_ref, off_ref, tile, make_copy):
    def body(e, src):
        n = cnt_ref[tile * N_EXPERTS + e]

        @pl.when(n > 0)
        def _():
            make_copy(src, off_ref[tile * N_EXPERTS + e], n).start()
        return src + n
    lax.fori_loop(0, N_EXPERTS, body, 0)


def _slot_rows(first, n):
    s = SLOT_SUBROWS
    return pl.ds(pl.multiple_of(first * s, s), pl.multiple_of(n * s, s))


def _load_slot_major(ref, n_slots):
    return jnp.concatenate([ref[pl.ds(s, n_slots, stride=SLOT_SUBROWS), :] for s in range(SLOT_SUBROWS)], axis=1)


def _store_slot_major(ref, val):
    for s in range(SLOT_SUBROWS):
        ref[pl.ds(s, val.shape[0], stride=SLOT_SUBROWS), :] = val[:, s * LANES:(s + 1) * LANES]


def _wait_slot(stage, sem, s):
    pltpu.make_async_copy(stage.at[s], stage.at[s], sem.at[s]).wait()


def _dispatch_kernel(n_main, off_ref, cnt_ref, pad_start_ref, pad_len_ref, h_ref, id_ref, hc_ref, idc_ref,
                     xs_hbm, pos_ref, stage, sem, zeros, zsem):
    i = pl.program_id(0)
    slot = i % 2
    from_main = i < n_main
    ids = id_ref[...] if idc_ref is None else jnp.where(from_main, id_ref[...], idc_ref[...])
    h = h_ref[...] if hc_ref is None else jnp.where(from_main, h_ref[...], hc_ref[...])

    @pl.when(i == 0)
    def _():
        zeros[...] = jnp.zeros(zeros.shape, zeros.dtype)

        def pad_copy(r):
            n = pad_len_ref[r]
            return n, pltpu.make_async_copy(zeros.at[_slot_rows(0, n)], xs_hbm.at[_slot_rows(pad_start_ref[r], n)], zsem)

        def start(r, carry):
            n, copy = pad_copy(r)

            @pl.when(n > 0)
            def _():
                copy.start()
            return carry

        def wait(r, carry):
            n, copy = pad_copy(r)

            @pl.when(n > 0)
            def _():
                copy.wait()
            return carry

        lax.fori_loop(0, pad_len_ref.shape[0], start, 0)
        lax.fori_loop(0, pad_len_ref.shape[0], wait, 0)

    lane = lax.broadcasted_iota(jnp.int32, ids.shape, 1)
    e0 = jnp.where(lane == ids[:, 0:1], 1.0, 0.0)
    e1 = jnp.where(lane == ids[:, 1:2], 1.0, 0.0)
    tt = lax.broadcasted_iota(jnp.int32, (MOE_TILE, MOE_TILE), 0)
    earlier = jnp.where(lax.broadcasted_iota(jnp.int32, (MOE_TILE, MOE_TILE), 1) < tt, 1.0, 0.0).astype(_BF16)
    c0 = _dot(earlier, e0.astype(_BF16))
    c1 = _dot(earlier, e1.astype(_BF16))
    tot0 = jnp.sum(e0, axis=0, keepdims=True)
    ee = lax.broadcasted_iota(jnp.int32, (ROUTE_COLS, ROUTE_COLS), 0)
    lower = jnp.where(ee < lax.broadcasted_iota(jnp.int32, (ROUTE_COLS, ROUTE_COLS), 1), 1.0, 0.0).astype(_BF16)
    base = jnp.sum(_dot((e0 + e1).astype(_BF16), lower), axis=0, keepdims=True)
    pos0 = jnp.sum(e0 * (base + c0), axis=1, keepdims=True)
    pos1 = jnp.sum(e1 * (base + tot0 + c1), axis=1, keepdims=True)
    pos = jnp.where(lane == 0, pos0, jnp.where(lane == 1, pos1, 0.0))
    pos_ref[...] = pos
    pos_t = pos.T
    slot_id = lax.broadcasted_iota(jnp.int32, (MOE_SLOTS, MOE_TILE), 0).astype(_F32)
    pick = jnp.where((slot_id == pos_t[0:1, :]) | (slot_id == pos_t[1:2, :]), 1.0, 0.0).astype(_BF16)
    rows = _dot(pick, h)

    @pl.when(i >= 2)
    def _():
        _wait_slot(stage, sem, slot)

    _store_slot_major(stage.at[slot], rows)
    _copy_runs(cnt_ref, off_ref, i, lambda src, dst, n: pltpu.make_async_copy(
        stage.at[slot, _slot_rows(src, n)], xs_hbm.at[_slot_rows(dst, n)], sem.at[slot]))

    @pl.when(i == pl.num_programs(0) - 1)
    def _():
        _wait_slot(stage, sem, slot)

        @pl.when(i >= 1)
        def _():
            _wait_slot(stage, sem, 1 - slot)


def _dispatch(routed, routed_c, plan, n_blocks):
    d = routed[0].shape[1]
    n_main = routed[0].shape[0] // MOE_TILE
    n_ctx = 0 if routed_c is None else routed_c[0].shape[0] // MOE_TILE
    n_tiles = n_main + n_ctx
    t = n_tiles * MOE_TILE
    pair = lambda f: [pl.BlockSpec((MOE_TILE, d), lambda i, *_: (f(i), 0)),
                      pl.BlockSpec((MOE_TILE, ROUTE_COLS), lambda i, *_: (f(i), 0))]
    if routed_c is None:
        body = lambda *refs: _dispatch_kernel(n_main, *refs[:6], None, None, *refs[6:])
    else:
        body = functools.partial(_dispatch_kernel, n_main)
    grid_spec = pltpu.PrefetchScalarGridSpec(
        num_scalar_prefetch=4,
        grid=(n_tiles,),
        in_specs=pair(lambda i: jnp.minimum(i, n_main - 1)) + ([] if routed_c is None else
                                                              pair(lambda i: jnp.maximum(i - n_main, 0))),
        out_specs=[pl.BlockSpec(memory_space=pl.ANY), pl.BlockSpec((MOE_TILE, ROUTE_COLS), lambda i, *_: (i, 0))],
        scratch_shapes=[pltpu.VMEM((2, MOE_SLOTS * SLOT_SUBROWS, LANES), _F32), pltpu.SemaphoreType.DMA((2,)),
                        pltpu.VMEM((MOE_ROWS * SLOT_SUBROWS, LANES), _F32), pltpu.SemaphoreType.DMA(())],
    )
    return pl.pallas_call(
        body,
        grid_spec=grid_spec,
        out_shape=[jax.ShapeDtypeStruct((n_blocks * MOE_ROWS * SLOT_SUBROWS, LANES), _F32),
                   jax.ShapeDtypeStruct((t, ROUTE_COLS), _F32)],
        compiler_params=_cparams("arbitrary"),
        name="moe_dispatch",
    )(*plan, *routed, *(routed_c or ()))


def _expert_kernel(blk_e_ref, rows_ref, x_ref, w1_ref, w3_ref, w2_ref, o_ref, w1b, w3b, w2b):
    i = pl.program_id(0)
    rows = rows_ref[i]

    @pl.when((i == 0) | (blk_e_ref[i] != blk_e_ref[jnp.maximum(i - 1, 0)]))
    def _():
        w1b[...] = w1_ref[0, 0].astype(_BF16)
        w3b[...] = w3_ref[0, 0].astype(_BF16)
        w2b[...] = w2_ref[0, 0].astype(_BF16)

    def ffn(first, n_slots):
        span = pl.ds(first * SLOT_SUBROWS, n_slots * SLOT_SUBROWS)
        xb = _load_slot_major(x_ref.at[span], n_slots).astype(_BF16)
        h1 = _dot(xb, w1b[...])
        h3 = _dot(xb, w3b[...])
        act = (h1 * jax.nn.sigmoid(h1) * h3).astype(_BF16)
        _store_slot_major(o_ref.at[span], _dot(act, w2b[...]))

    def clear(first, n_slots):
        o_ref[pl.ds(first * SLOT_SUBROWS, n_slots * SLOT_SUBROWS), :] = jnp.zeros((n_slots * SLOT_SUBROWS, LANES), _F32)

    @pl.when(rows > MOE_HALF)
    def _():
        ffn(0, MOE_HALF)
        ffn(MOE_HALF, MOE_HALF)

    @pl.when((rows > 0) & (rows <= MOE_HALF))
    def _():
        ffn(0, MOE_HALF)
        clear(MOE_HALF, MOE_HALF)

    @pl.when(rows == 0)
    def _():
        clear(0, MOE_ROWS)


def _experts(xs, blk_e, blk_rows, layer, w1, w3, w2):
    d = w1.shape[2]
    blk = MOE_ROWS * SLOT_SUBROWS
    n_blocks = xs.shape[0] // blk
    grid_spec = pltpu.PrefetchScalarGridSpec(
        num_scalar_prefetch=2,
        grid=(n_blocks,),
        in_specs=[
            pl.BlockSpec((blk, LANES), lambda i, be, br: (i, 0)),
            pl.BlockSpec((1, 1, d, EXPERT_FF), lambda i, be, br: (layer, be[i], 0, 0)),
            pl.BlockSpec((1, 1, d, EXPERT_FF), lambda i, be, br: (layer, be[i], 0, 0)),
            pl.BlockSpec((1, 1, EXPERT_FF, d), lambda i, be, br: (layer, be[i], 0, 0)),
        ],
        out_specs=pl.BlockSpec((blk, LANES), lambda i, be, br: (i, 0)),
        scratch_shapes=[pltpu.VMEM((d, EXPERT_FF), _BF16), pltpu.VMEM((d, EXPERT_FF), _BF16),
                        pltpu.VMEM((EXPERT_FF, d), _BF16)],
    )
    return pl.pallas_call(
        _expert_kernel,
        grid_spec=grid_spec,
        out_shape=jax.ShapeDtypeStruct(xs.shape, _F32),
        compiler_params=_cparams("arbitrary"),
        name="experts",
    )(blk_e, blk_rows, xs, w1, w3, w2)


def _combine_kernel(final, tile0, off_ref, cnt_ref, x_ref, pos_ref, wt_ref, gt_ref, g_ref, ys_hbm, o_ref, stage, sem):
    i = pl.program_id(0)
    slot = i % 2

    def fetch(tile, s):
        _copy_runs(cnt_ref, off_ref, tile, lambda src, dst, n: pltpu.make_async_copy(
            ys_hbm.at[_slot_rows(dst, n)], stage.at[s, _slot_rows(src, n)], sem.at[s]))

    @pl.when(i == 0)
    def _():
        fetch(tile0, 0)

    @pl.when(i + 1 < pl.num_programs(0))
    def _():
        fetch(tile0 + i + 1, 1 - slot)

    _wait_slot(stage, sem, slot)
    y = _load_slot_major(stage.at[slot], MOE_SLOTS).astype(_BF16)
    slot_id = lax.broadcasted_iota(jnp.int32, (MOE_TILE, MOE_SLOTS), 1).astype(_F32)
    mix = (jnp.where(slot_id == pos_ref[:, 0:1], wt_ref[:, 0:1], 0.0)
           + jnp.where(slot_id == pos_ref[:, 1:2], wt_ref[:, 1:2], 0.0)).astype(_BF16)
    x = x_ref[...] + gt_ref[0] * _dot(mix, y)
    if final:
        x = (x * lax.rsqrt(jnp.mean(x * x, axis=-1, keepdims=True) + EPS)) * g_ref[...]
    o_ref[...] = x


def _combine(x, ys, pos, wts, off, cnt, gt, g_final, tok0, final):
    b, n, d = x.shape
    t = b * n
    per_b = n // MOE_TILE
    tile0 = tok0 // MOE_TILE
    row = pl.BlockSpec((MOE_TILE, d), lambda i, o, c: (i, 0))
    lanes = lambda first: pl.BlockSpec((MOE_TILE, ROUTE_COLS), lambda i, o, c: (first + i, 0))
    grid_spec = pltpu.PrefetchScalarGridSpec(
        num_scalar_prefetch=2,
        grid=(t // MOE_TILE,),
        in_specs=[row, lanes(tile0), lanes(0), pl.BlockSpec((1, 1, d), lambda i, o, c: (i // per_b, 0, 0)),
                  pl.BlockSpec((1, d), lambda i, o, c: (0, 0)), pl.BlockSpec(memory_space=pl.ANY)],
        out_specs=row,
        scratch_shapes=[pltpu.VMEM((2, MOE_SLOTS * SLOT_SUBROWS, LANES), _F32), pltpu.SemaphoreType.DMA((2,))],
    )
    out = pl.pallas_call(
        functools.partial(_combine_kernel, final, tile0),
        grid_spec=grid_spec,
        out_shape=jax.ShapeDtypeStruct((t, d), _F32),
        compiler_params=_cparams("arbitrary"),
        name="moe_combine",
    )(off, cnt, x.reshape(t, d), pos, wts, gt, g_final.reshape(1, d), ys)
    return out.reshape(b, n, d)


def _rotary_tables(n):
    n_freq = R_QK_DIM // 4
    inv = ROPE_BASE ** (-jnp.arange(n_freq, dtype=_F32) / n_freq)
    rows = n // GRID_W
    row = jnp.repeat(jnp.arange(rows, dtype=_F32), GRID_W)
    col = jnp.tile(jnp.arange(GRID_W, dtype=_F32), rows)
    ang = jnp.concatenate([row[:, None] * inv, col[:, None] * inv], axis=-1)
    cos, sin = jnp.cos(ang), jnp.sin(ang)
    return jnp.repeat(cos, 2, axis=-1), jnp.stack([-sin, sin], axis=-1).reshape(n, R_QK_DIM)


def _decay_tables(decay_logit, l_ctx):
    log_g = jax.nn.log_sigmoid(decay_logit.astype(_F32))
    lgf, lgb = log_g[0][:, None], log_g[1][:, None]
    j = jnp.arange(CHUNK, dtype=_F32)
    diff = j[:, None] - j[None, :]
    dmat = jnp.where(diff[None] >= 0, jnp.exp(lgf[:, :, None] * jnp.maximum(diff, 0.0)[None]),
                     jnp.exp(lgb[:, :, None] * jnp.maximum(-diff, 0.0)[None]))
    lanes = lambda t: jnp.broadcast_to(t[:, :, None], t.shape + (LANES,))
    m = jnp.arange(l_ctx, dtype=_F32)
    return {
        "dmat": dmat,
        "qdf": lanes(jnp.exp(lgf * (j + 1.0))), "qdb": lanes(jnp.exp(lgb * (CHUNK - j))),
        "kdf": lanes(jnp.exp(lgf * (CHUNK - 1.0 - j))), "kdb": lanes(jnp.exp(lgb * j)),
        "cdec": jnp.exp(log_g * CHUNK),
        "ctx_f": lanes(jnp.exp(lgf * ((l_ctx - 1.0) - m))), "ctx_b": lanes(jnp.exp(lgb * m)),
    }


def kernel(x, c, ctx, c_ctx, w_ada, b_ada, g_mix, g_ffn, w_in, w_s, b_s, g_sgu, decay_logit, w_pa, w_pb, w_out,
           w_group, b_group, w_erouter, b_erouter, w1, w3, w2, g_final):
    bsz, n, d = x.shape
    l_ctx = ctx.shape[1]
    t_main = bsz * n

    cond = jnp.zeros((COND_ROWS, d), _F32).at[:bsz].set(c).at[bsz].set(c_ctx)
    mods = _adaln(cond, w_ada, b_ada)
    rot = _rotary_tables(n)
    no_rot = (jnp.ones((l_ctx, LANES), _F32), jnp.zeros((l_ctx, LANES), _F32))
    zero_state = jnp.zeros((bsz, R_HEADS, R_QK_DIM, R_V_DIM), _F32)

    xc = ctx
    for l in range(DEPTH):
        update_ctx = l < DEPTH - 1
        m6 = mods[l].reshape(COND_ROWS, 6, d)
        main = [m6[:bsz, i][:, None, :] for i in range(6)]
        cmod = [jnp.broadcast_to(m6[bsz, i][None, None, :], (bsz, 1, d)) for i in range(6)]
        tabs = _decay_tables(decay_logit[l], l_ctx)
        w_in_b = w_in[l].astype(_BF16)
        gsgu = g_sgu[l].reshape(1, D_A)
        mix_w = (w_s[l].astype(_BF16), jnp.broadcast_to(b_s[l][:, :, None], (A_GROUPS, CHUNK, LANES)),
                 w_pa[l].astype(_BF16), w_pb[l].astype(_BF16), w_out[l].astype(_BF16))

        p = _proj_in(x, g_mix[l], main[0], main[1], gsgu, *rot, w_in_b, _MIX_SEGMENTS)
        if update_ctx:
            pc = _proj_in(xc, g_mix[l], cmod[0], cmod[1], gsgu, *no_rot, w_in_b, _MIX_SEGMENTS)
            s_ctx_f, s_ctx_b = _ctx_states(pc, _K_BLK, _VR_BLK, tabs["ctx_f"], tabs["ctx_b"])
        else:
            w_vk = jnp.concatenate([w_in_b[:, KV_START + D_RQK:KV_END], w_in_b[:, KV_START:KV_START + D_RQK]], axis=1)
            vk_c = _proj_in(xc, g_mix[l], cmod[0], cmod[1], gsgu, *no_rot, w_vk, _VK_SEGMENTS)
            s_ctx_f, s_ctx_b = _ctx_states(vk_c, D_RV // D_RQK, 0, tabs["ctx_f"], tabs["ctx_b"])
        sb = _bwd_states(p, tabs["kdb"], tabs["cdec"], s_ctx_b)
        wr = jnp.zeros((d, ROUTE_COLS // 2), _F32).at[:, :MOE_GROUPS].set(w_group[l])
        wr = wr.at[:, MOE_GROUPS:MOE_GROUPS + N_EXPERTS].set(w_erouter[l])
        wr_hi = wr.astype(_BF16)
        wr = jnp.concatenate([wr_hi, (wr - wr_hi.astype(_F32)).astype(_BF16)], axis=1)
        br = jnp.zeros((1, ROUTE_COLS), _F32).at[0, :MOE_GROUPS].set(b_group[l])
        br = br.at[0, MOE_GROUPS:MOE_GROUPS + N_EXPERTS].set(b_erouter[l])
        route_w = (g_ffn[l], wr, br)
        x, h2, ids, wts, cnt = _token_mix(p, x, main[2], tabs, sb, s_ctx_f, *mix_w, route_w[0], main[3], main[4],
                                          *route_w[1:])
        routed_c = None
        if update_ctx:
            sbc = _bwd_states(pc, tabs["kdb"], tabs["cdec"], zero_state)
            xc, h2c, idsc, wtsc, cntc = _token_mix(pc, xc, cmod[2], tabs, sbc, zero_state, *mix_w, route_w[0],
                                                   cmod[3], cmod[4], *route_w[1:])
            routed_c = (h2c, idsc)
            cnt = jnp.concatenate([cnt, cntc], axis=0)

        plan, blk_e, blk_rows, n_blocks = _dispatch_plan(cnt[:, 0, :N_EXPERTS])
        off, cnt = plan[:2]
        xs, pos = _dispatch((h2, ids), routed_c, plan, n_blocks)
        ys = _experts(xs, blk_e, blk_rows, l, w1, w3, w2)
        final = l == DEPTH - 1
        x = _combine(x, ys, pos, wts, off, cnt, main[5], g_final, 0, final)
        if update_ctx:
            xc = _combine(xc, ys, pos, wtsc, off, cnt, cmod[5], g_final, t_main, False)
    return x
```

```python
import functools

import numpy as np
import jax
import jax.numpy as jnp
from jax import lax
from jax.experimental import pallas as pl
from jax.experimental.pallas import tpu as pltpu

D_MODEL = 1024
DEPTH = 2
GRID_W = 64
CHUNK = 128
A_GROUP_DIM = 128
A_GROUPS = D_MODEL // A_GROUP_DIM
D_A = A_GROUPS * A_GROUP_DIM
R_HEADS = 4
R_QK_DIM = D_MODEL // 8
R_V_DIM = D_MODEL // 4
D_RQK = R_HEADS * R_QK_DIM
D_RV = R_HEADS * R_V_DIM
D_IN = 2 * D_A + 2 * D_RQK + 2 * D_RV + 2 * D_MODEL
KV_START = 2 * D_A + D_RQK
KV_END = KV_START + D_RQK + D_RV
ROPE_BASE = 10000.0
MOE_GROUPS = 4
EXPERTS_PER_GROUP = 8
N_EXPERTS = MOE_GROUPS * EXPERTS_PER_GROUP
EXPERT_FF = D_MODEL // 2
EPS = 1e-6

LANES = 128
SUBLANES = 8
MOE_ROWS = 512
MOE_HALF = MOE_ROWS // 2
VMEM_LIMIT = 56 * 1024 * 1024
COND_ROWS = 16

_U_BLK, _VA_BLK = 0, 1
_Q_BLK, _K_BLK = 2 * D_A // D_RQK, 2 * D_A // D_RQK + 1
_VR_BLK = (2 * D_A + 2 * D_RQK) // D_RV
_GR_BLK, _GA_BLK, _GB_BLK = _VR_BLK + 1, _VR_BLK + 2, _VR_BLK + 3

_F32 = jnp.float32
_BF16 = jnp.bfloat16


def _cparams(*sem):
    return pltpu.CompilerParams(dimension_semantics=sem, vmem_limit_bytes=VMEM_LIMIT)


def _dot(a, b):
    return jnp.dot(a, b, preferred_element_type=_F32)


def _adaln_kernel(cond_ref, w_ref, b_ref, o_ref):
    cnd = cond_ref[...]
    s = cnd * jax.nn.sigmoid(cnd)
    o_ref[0] = jnp.dot(s, w_ref[0], preferred_element_type=_F32, precision=lax.Precision.HIGHEST) + b_ref[0]


def _adaln(cond, w_ada, b_ada):
    depth, d, d6 = w_ada.shape
    tn = 1024
    return pl.pallas_call(
        _adaln_kernel,
        grid=(depth, d6 // tn),
        in_specs=[
            pl.BlockSpec((COND_ROWS, d), lambda l, j: (0, 0)),
            pl.BlockSpec((1, d, tn), lambda l, j: (l, 0, j)),
            pl.BlockSpec((1, 1, tn), lambda l, j: (l, 0, j)),
        ],
        out_specs=pl.BlockSpec((1, COND_ROWS, tn), lambda l, j: (l, 0, j)),
        out_shape=jax.ShapeDtypeStruct((depth, COND_ROWS, d6), _F32),
        compiler_params=_cparams("parallel", "parallel"),
        name="adaln",
    )(cond, w_ada, b_ada.reshape(depth, 1, d6))


def _modnorm(x, g, shift, scale):
    y = x * lax.rsqrt(jnp.mean(x * x, axis=-1, keepdims=True) + EPS)
    return (y * g) * (1.0 + scale) + shift


def _layernorm(t):
    mu = jnp.mean(t, axis=-1, keepdims=True)
    tc = t - mu
    return tc * lax.rsqrt(jnp.mean(tc * tc, axis=-1, keepdims=True) + EPS)


def _rotate_heads(t, cos_i, sin_i, scale):
    even = lax.broadcasted_iota(jnp.int32, cos_i.shape, 1) % 2 == 0
    out = []
    for h in range(t.shape[1] // R_QK_DIM):
        th = t[:, h * R_QK_DIM:(h + 1) * R_QK_DIM]
        partner = jnp.where(even, pltpu.roll(th, R_QK_DIM - 1, 1), pltpu.roll(th, 1, 1))
        out.append((th * cos_i + partner * sin_i) * scale)
    return jnp.concatenate(out, axis=1)


def _proj_kernel(segments, x_ref, sh_ref, sc_ref, xn_ref, shn_ref, scn_ref, g_ref, gsgu_ref, cos_ref, sin_ref, w_ref,
                 o_ref, h_ref):
    step = pl.program_id(0) * pl.num_programs(1) + pl.program_id(1)
    slot = step % 2

    @pl.when(step == 0)
    def _():
        h_ref[0] = _modnorm(x_ref[0], g_ref[...], sh_ref[0], sc_ref[0]).astype(_BF16)

    h_ref[1 - slot] = _modnorm(xn_ref[0], g_ref[...], shn_ref[0], scn_ref[0]).astype(_BF16)
    for start, width, kind in segments:
        cols = slice(start, start + width)
        acc = _dot(h_ref[slot], w_ref[0, :, cols])
        if kind == "gelu":
            acc = jax.nn.gelu(acc)
        elif kind == "gelu_ln":
            acc = _layernorm(jax.nn.gelu(acc)) * gsgu_ref[...]
        elif kind == "rot_q":
            acc = _rotate_heads(acc, cos_ref[...], sin_ref[...], 1.0)
        elif kind == "rot_k":
            acc = _rotate_heads(acc, cos_ref[...], sin_ref[...], R_QK_DIM ** -0.5)
        elif kind == "silu":
            acc = acc * jax.nn.sigmoid(acc)
        elif kind == "sigmoid":
            acc = jax.nn.sigmoid(acc)
        o_ref[0, :, cols] = acc.astype(_BF16)


_MIX_SEGMENTS = ((0, D_A, "gelu"), (D_A, D_A, "gelu_ln"), (2 * D_A, D_RQK, "rot_q"), (2 * D_A + D_RQK, D_RQK, "rot_k"),
                 (KV_END, D_RV, "silu"), (KV_END + D_RV, D_MODEL, "sigmoid"),
                 (KV_END + D_RV + D_MODEL, D_MODEL, "sigmoid"), (KV_START + D_RQK, D_RV, "plain"))
_VK_SEGMENTS = ((D_RV, D_RQK, "rot_k"), (0, D_RV, "plain"))


def _proj_in(x, g, shift, scale, gsgu, cos_i, sin_i, w, layer, segments):
    b, n, d = x.shape
    dout = w.shape[2]
    tm = min(n, 512)
    nr = n // tm
    last = b * nr - 1
    nxt = lambda i, r: jnp.minimum(i * nr + r + 1, last)
    cur_mod = pl.BlockSpec((1, 1, d), lambda i, r: (i, 0, 0))
    nxt_mod = pl.BlockSpec((1, 1, d), lambda i, r: (nxt(i, r) // nr, 0, 0))
    return pl.pallas_call(
        functools.partial(_proj_kernel, segments),
        grid=(b, nr),
        in_specs=[
            pl.BlockSpec((1, tm, d), lambda i, r: (i, r, 0)), cur_mod, cur_mod,
            pl.BlockSpec((1, tm, d), lambda i, r: (nxt(i, r) // nr, nxt(i, r) % nr, 0)), nxt_mod, nxt_mod,
            pl.BlockSpec((1, d), lambda i, r: (0, 0)),
            pl.BlockSpec((1, D_A), lambda i, r: (0, 0)),
            pl.BlockSpec((tm, LANES), lambda i, r: (r, 0)),
            pl.BlockSpec((tm, LANES), lambda i, r: (r, 0)),
            pl.BlockSpec((1, d, dout), lambda i, r: (layer, 0, 0), pipeline_mode=pl.Buffered(1)),
        ],
        out_specs=pl.BlockSpec((1, tm, dout), lambda i, r: (i, r, 0)),
        out_shape=jax.ShapeDtypeStruct((b, n, dout), _BF16),
        scratch_shapes=[pltpu.VMEM((2, tm, d), _BF16)],
        compiler_params=_cparams("arbitrary", "arbitrary"),
        name="proj_in",
    )(x, shift, scale, x, shift, scale, g.reshape(1, d), gsgu, cos_i, sin_i, w)


def _dot_t(a, b):
    return lax.dot_general(a, b, (((0,), (0,)), ((), ())), preferred_element_type=_F32)


def _ctx_state_kernel(k_ref, v_ref, wf_ref, wb_ref, sf_ref, sb_ref):
    for h in range(R_HEADS):
        kh = k_ref[0, :, h * R_QK_DIM:(h + 1) * R_QK_DIM].astype(_F32)
        vh = v_ref[0, :, h * R_V_DIM:(h + 1) * R_V_DIM]
        sf_ref[0, h] = _dot_t((kh * wf_ref[h]).astype(_BF16), vh)
        sb_ref[0, h] = _dot_t((kh * wb_ref[h]).astype(_BF16), vh)


def _ctx_states(pc, k_blk, v_blk, wf, wb):
    b, l, _ = pc.shape
    st = jax.ShapeDtypeStruct((b, R_HEADS, R_QK_DIM, R_V_DIM), _F32)
    st_spec = pl.BlockSpec((1, R_HEADS, R_QK_DIM, R_V_DIM), lambda i: (i, 0, 0, 0))
    return pl.pallas_call(
        _ctx_state_kernel,
        grid=(b,),
        in_specs=[
            pl.BlockSpec((1, l, D_RQK), lambda i: (i, 0, k_blk)),
            pl.BlockSpec((1, l, D_RV), lambda i: (i, 0, v_blk)),
            pl.BlockSpec((R_HEADS, l, LANES), lambda i: (0, 0, 0)),
            pl.BlockSpec((R_HEADS, l, LANES), lambda i: (0, 0, 0)),
        ],
        out_specs=[st_spec, st_spec],
        out_shape=[st, st],
        compiler_params=_cparams("parallel"),
        name="ctx_states",
    )(pc, pc, wf, wb)


def _bwd_state_kernel(n_chunks, cdec_ref, k_ref, v_ref, kdb_ref, s0_ref, o_ref, s_ref):
    @pl.when(pl.program_id(1) == 0)
    def _():
        s_ref[...] = s0_ref[0]

    for c in reversed(range(n_chunks)):
        rows = pl.ds(c * CHUNK, CHUNK)
        for h in range(R_HEADS):
            s = s_ref[h]
            o_ref[0, c, h] = s.astype(_BF16)
            kh = k_ref[0, rows, h * R_QK_DIM:(h + 1) * R_QK_DIM].astype(_F32)
            vh = v_ref[0, rows, h * R_V_DIM:(h + 1) * R_V_DIM]
            s_ref[h] = s * cdec_ref[1, h] + _dot_t((kh * kdb_ref[h]).astype(_BF16), vh)


def _bwd_states(p, kdb, cdec, s0_b):
    b, n, _ = p.shape
    rows = min(n, 1024)
    ncb, nblk = rows // CHUNK, n // rows
    rev = lambda r: nblk - 1 - r
    return pl.pallas_call(
        functools.partial(_bwd_state_kernel, ncb),
        grid=(b, nblk),
        in_specs=[
            pl.BlockSpec(memory_space=pltpu.SMEM),
            pl.BlockSpec((1, rows, D_RQK), lambda i, r: (i, rev(r), _K_BLK)),
            pl.BlockSpec((1, rows, D_RV), lambda i, r: (i, rev(r), _VR_BLK)),
            pl.BlockSpec((R_HEADS, CHUNK, LANES), lambda i, r: (0, 0, 0)),
            pl.BlockSpec((1, R_HEADS, R_QK_DIM, R_V_DIM), lambda i, r: (i, 0, 0, 0)),
        ],
        out_specs=pl.BlockSpec((1, ncb, R_HEADS, R_QK_DIM, R_V_DIM), lambda i, r: (i, rev(r), 0, 0, 0)),
        out_shape=jax.ShapeDtypeStruct((b, n // CHUNK, R_HEADS, R_QK_DIM, R_V_DIM), _BF16),
        scratch_shapes=[pltpu.VMEM((R_HEADS, R_QK_DIM, R_V_DIM), _F32)],
        compiler_params=_cparams("parallel", "arbitrary"),
        name="bwd_states",
    )(cdec, p, p, kdb, s0_b)


def _mix_kernel(n_chunks, cdec_ref, u_ref, va_ref, q_ref, k_ref, vr_ref, gr_ref, ga_ref, gb_ref, x_ref, gt_ref,
                dmat_ref, qdf_ref, qdb_ref, kdf_ref, sb_ref, s0_ref,
                ws_ref, bs_ref, wpa_ref, wpb_ref, wout_ref, g2_ref, sh2_ref, sc2_ref, wr_ref, br_ref,
                o_ref, h_ref, id_ref, wt_ref, cnt_ref, sf_ref, ya_ref, yb_ref):
    @pl.when(pl.program_id(1) == 0)
    def _():
        sf_ref[...] = s0_ref[0]

    for c in range(n_chunks):
        rows = pl.ds(c * CHUNK, CHUNK)
        gu = u_ref[0, rows, :].astype(_F32)
        vn = va_ref[0, rows, :]
        for g in range(A_GROUPS):
            cols = slice(g * A_GROUP_DIM, (g + 1) * A_GROUP_DIM)
            s = _dot(ws_ref[g], vn[:, cols]) + bs_ref[g]
            ya_ref[rows, cols] = (gu[:, cols] * s).astype(_BF16)
        for h in range(R_HEADS):
            qk = slice(h * R_QK_DIM, (h + 1) * R_QK_DIM)
            vv = slice(h * R_V_DIM, (h + 1) * R_V_DIM)
            qb, kb = q_ref[0, rows, qk], k_ref[0, rows, qk]
            qh, kh = qb.astype(_F32), kb.astype(_F32)
            vh = vr_ref[0, rows, vv]
            scores = lax.dot_general(qb, kb, (((1,), (1,)), ((), ())), preferred_element_type=_F32) * dmat_ref[h]
            o = _dot(scores.astype(_BF16), vh)
            s_f = sf_ref[h]
            o = o + _dot((qh * qdf_ref[h]).astype(_BF16), s_f.astype(_BF16))
            o = o + _dot((qh * qdb_ref[h]).astype(_BF16), sb_ref[0, c, h])
            yb_ref[rows, vv] = (gr_ref[0, rows, vv].astype(_F32) * _layernorm(o)).astype(_BF16)
            sf_ref[h] = s_f * cdec_ref[0, h] + _dot_t((kh * kdf_ref[h]).astype(_BF16), vh)

    a = _dot(ya_ref[...], wpa_ref[...])
    b = _dot(yb_ref[...], wpb_ref[...])
    merged = ga_ref[0].astype(_F32) * a + gb_ref[0].astype(_F32) * b
    x_new = x_ref[0] + gt_ref[0] * _dot(merged.astype(_BF16), wout_ref[...])
    o_ref[0] = x_new
    _route(x_new, g2_ref, sh2_ref, sc2_ref, wr_ref, br_ref, h_ref, id_ref, wt_ref, cnt_ref)


def _token_mix(p, x, gt, tabs, sb, s0_f, ws, bsb, wpa, wpb, wout, g2, shift2, scale2, wr, br):
    b, n, d = x.shape
    t = b * n
    rows = min(n, 512)
    ncb = rows // CHUNK
    per_b = n // rows
    tiles = rows // MOE_TILE
    flat = lambda width: pl.BlockSpec((rows, width), lambda i, r: (i * per_b + r, 0))
    pspec = lambda width, blk: pl.BlockSpec((1, rows, width), lambda i, r: (i, r, blk))
    const = lambda shape: pl.BlockSpec(shape, lambda i, r: (0,) * len(shape))
    tab = const((R_HEADS, CHUNK, LANES))
    return pl.pallas_call(
        functools.partial(_mix_kernel, ncb),
        grid=(b, n // rows),
        in_specs=[
            pl.BlockSpec(memory_space=pltpu.SMEM),
            pspec(D_A, _U_BLK), pspec(D_A, _VA_BLK), pspec(D_RQK, _Q_BLK), pspec(D_RQK, _K_BLK),
            pspec(D_RV, _VR_BLK), pspec(D_RV, _GR_BLK), pspec(d, _GA_BLK), pspec(d, _GB_BLK),
            pl.BlockSpec((1, rows, d), lambda i, r: (i, r, 0)),
            pl.BlockSpec((1, 1, d), lambda i, r: (i, 0, 0)),
            tab, tab, tab, tab,
            pl.BlockSpec((1, ncb, R_HEADS, R_QK_DIM, R_V_DIM), lambda i, r: (i, r, 0, 0, 0)),
            pl.BlockSpec((1, R_HEADS, R_QK_DIM, R_V_DIM), lambda i, r: (i, 0, 0, 0)),
            const((A_GROUPS, CHUNK, CHUNK)), const((A_GROUPS, CHUNK, LANES)),
            const((D_A, d)), const((D_RV, d)), const((d, d)),
            const((1, d)), pl.BlockSpec((1, 1, d), lambda i, r: (i, 0, 0)), pl.BlockSpec((1, 1, d), lambda i, r: (i, 0, 0)),
            const((d, ROUTE_COLS)), const((1, ROUTE_COLS)),
        ],
        out_specs=[pl.BlockSpec((1, rows, d), lambda i, r: (i, r, 0)), flat(d), flat(ROUTE_COLS), flat(ROUTE_COLS),
                   pl.BlockSpec((tiles, SUBLANES, ROUTE_COLS), lambda i, r: (i * per_b + r, 0, 0))],
        out_shape=[jax.ShapeDtypeStruct((b, n, d), _F32), jax.ShapeDtypeStruct((t, d), _BF16),
                   jax.ShapeDtypeStruct((t, ROUTE_COLS), jnp.int32), jax.ShapeDtypeStruct((t, ROUTE_COLS), _F32),
                   jax.ShapeDtypeStruct((t // MOE_TILE, SUBLANES, ROUTE_COLS), jnp.int32)],
        scratch_shapes=[
            pltpu.VMEM((R_HEADS, R_QK_DIM, R_V_DIM), _F32),
            pltpu.VMEM((rows, D_A), _BF16),
            pltpu.VMEM((rows, D_RV), _BF16),
        ],
        compiler_params=_cparams("parallel", "arbitrary"),
        name="token_mix",
    )(tabs["cdec"], p, p, p, p, p, p, p, p, x, gt, tabs["dmat"], tabs["qdf"], tabs["qdb"], tabs["kdf"],
      sb, s0_f, ws, bsb, wpa, wpb, wout, g2.reshape(1, d), shift2, scale2, wr, br)


ROUTE_COLS = LANES
MOE_TILE = 256
MOE_SLOTS = 2 * MOE_TILE
SLOT_SUBROWS = D_MODEL // LANES


def _route(x, g_ref, sh_ref, sc_ref, wr_ref, br_ref, h_ref, id_ref, wt_ref, cnt_ref):
    h = _modnorm(x, g_ref[...], sh_ref[0], sc_ref[0])
    h_hi = h.astype(_BF16)
    h_ref[...] = h_hi
    h_lo = (h - h_hi.astype(_F32)).astype(_BF16)
    r_hi = _dot(h_hi, wr_ref[...])
    logits = r_hi + pltpu.roll(r_hi, ROUTE_COLS // 2, 1) + _dot(h_lo, wr_ref[...]) + br_ref[...]
    lane = lax.broadcasted_iota(jnp.int32, logits.shape, 1)
    neg = -jnp.inf
    gl = jnp.where(lane < MOE_GROUPS, logits, neg)
    g_top = jnp.max(gl, axis=-1, keepdims=True)
    g_sel = jnp.min(jnp.where(gl == g_top, lane, ROUTE_COLS), axis=-1, keepdims=True)
    lse = g_top + jnp.log(jnp.sum(jnp.exp(gl - g_top), axis=-1, keepdims=True))
    p_group = jnp.exp(g_top - lse)
    lo = MOE_GROUPS + g_sel * EXPERTS_PER_GROUP
    el = jnp.where((lane >= lo) & (lane < lo + EXPERTS_PER_GROUP), logits, neg)
    e1 = jnp.max(el, axis=-1, keepdims=True)
    i1 = jnp.min(jnp.where(el == e1, lane, ROUTE_COLS), axis=-1, keepdims=True)
    el2 = jnp.where(lane == i1, neg, el)
    e2 = jnp.max(el2, axis=-1, keepdims=True)
    i2 = jnp.min(jnp.where(el2 == e2, lane, ROUTE_COLS), axis=-1, keepdims=True)
    x2 = jnp.exp(e2 - e1)
    den = 1.0 + x2
    id1, id2 = i1 - MOE_GROUPS, i2 - MOE_GROUPS
    id_ref[...] = jnp.where(lane == 0, id1, jnp.where(lane == 1, id2, 0))
    wt_ref[...] = jnp.where(lane == 0, p_group * (1.0 / den), jnp.where(lane == 1, p_group * (x2 / den), 0.0))
    hits = jnp.where((lane == id1) | (lane == id2), 1, 0)
    for t in range(cnt_ref.shape[0]):
        tile_hits = jnp.sum(hits[t * MOE_TILE:(t + 1) * MOE_TILE], axis=0, keepdims=True)
        cnt_ref[t] = jnp.broadcast_to(tile_hits, (SUBLANES, ROUTE_COLS))


def _dispatch_plan(cnt):
    n_tiles = cnt.shape[0]
    a = n_tiles * MOE_SLOTS
    tot = jnp.sum(cnt, axis=0)
    padded = (tot + MOE_ROWS - 1) // MOE_ROWS * MOE_ROWS
    pend = jnp.cumsum(padded)
    pstart = pend - padded
    off = pstart[None, :] + jnp.cumsum(cnt, axis=0) - cnt
    n_blocks = (a + N_EXPERTS * (MOE_ROWS - 1) + MOE_ROWS - 1) // MOE_ROWS
    blk_start = jnp.arange(n_blocks, dtype=jnp.int32) * MOE_ROWS
    blk_e = jnp.minimum(jnp.sum((pend[None, :] <= blk_start[:, None]).astype(jnp.int32), axis=1), N_EXPERTS - 1)
    owner = (blk_e[:, None] == jnp.arange(N_EXPERTS, dtype=jnp.int32)[None, :]).astype(jnp.int32)
    used_end = jnp.sum(owner * (pstart + tot)[None, :], axis=1)
    blk_rows = jnp.clip(used_end - blk_start, 0, MOE_ROWS)
    pad_start = jnp.concatenate([pstart + tot, blk_start])
    pad_len = jnp.concatenate([padded - tot, jnp.where(blk_start >= pend[-1], MOE_ROWS, 0)])
    i32 = lambda v: v.astype(jnp.int32)
    return (i32(off).reshape(-1), i32(cnt).reshape(-1), i32(pad_start), i32(pad_len)), i32(blk_e), i32(blk_rows), n_blocks


def _copy_runs(cnt_ref, off_ref, tile, make_copy):
    def body(e, src):
        n = cnt_ref[tile * N_EXPERTS + e]

        @pl.when(n > 0)
        def _():
            make_copy(src, off_ref[tile * N_EXPERTS + e], n).start()
        return src + n
    lax.fori_loop(0, N_EXPERTS, body, 0)


def _slot_rows(first, n):
    s = SLOT_SUBROWS
    return pl.ds(pl.multiple_of(first * s, s), pl.multiple_of(n * s, s))


def _load_slot_major(ref, n_slots):
    return jnp.concatenate([ref[pl.ds(s, n_slots, stride=SLOT_SUBROWS), :] for s in range(SLOT_SUBROWS)], axis=1)


def _store_slot_major(ref, val):
    for s in range(SLOT_SUBROWS):
        ref[pl.ds(s, val.shape[0], stride=SLOT_SUBROWS), :] = val[:, s * LANES:(s + 1) * LANES]


def _wait_slot(stage, sem, s):
    pltpu.make_async_copy(stage.at[s], stage.at[s], sem.at[s]).wait()


def _dispatch_kernel(n_main, off_ref, cnt_ref, pad_start_ref, pad_len_ref, h_ref, id_ref, hc_ref, idc_ref,
                     xs_hbm, pos_ref, stage, sem, zeros, zsem):
    i = pl.program_id(0)
    slot = i % 2
    from_main = i < n_main
    ids = id_ref[...] if idc_ref is None else jnp.where(from_main, id_ref[...], idc_ref[...])
    h = h_ref[...] if hc_ref is None else jnp.where(from_main, h_ref[...], hc_ref[...])

    @pl.when(i == 0)
    def _():
        zeros[...] = jnp.zeros(zeros.shape, zeros.dtype)

        def pad_copy(r):
            n = pad_len_ref[r]
            return n, pltpu.make_async_copy(zeros.at[_slot_rows(0, n)], xs_hbm.at[_slot_rows(pad_start_ref[r], n)], zsem)

        def start(r, carry):
            n, copy = pad_copy(r)

            @pl.when(n > 0)
            def _():
                copy.start()
            return carry

        def wait(r, carry):
            n, copy = pad_copy(r)

            @pl.when(n > 0)
            def _():
                copy.wait()
            return carry

        lax.fori_loop(0, pad_len_ref.shape[0], start, 0)
        lax.fori_loop(0, pad_len_ref.shape[0], wait, 0)

    lane = lax.broadcasted_iota(jnp.int32, ids.shape, 1)
    e0 = jnp.where(lane == ids[:, 0:1], 1.0, 0.0)
    e1 = jnp.where(lane == ids[:, 1:2], 1.0, 0.0)
    tt = lax.broadcasted_iota(jnp.int32, (MOE_TILE, MOE_TILE), 0)
    earlier = jnp.where(lax.broadcasted_iota(jnp.int32, (MOE_TILE, MOE_TILE), 1) < tt, 1.0, 0.0).astype(_BF16)
    c0 = _dot(earlier, e0.astype(_BF16))
    c1 = _dot(earlier, e1.astype(_BF16))
    tot0 = jnp.sum(e0, axis=0, keepdims=True)
    ee = lax.broadcasted_iota(jnp.int32, (ROUTE_COLS, ROUTE_COLS), 0)
    lower = jnp.where(ee < lax.broadcasted_iota(jnp.int32, (ROUTE_COLS, ROUTE_COLS), 1), 1.0, 0.0).astype(_BF16)
    base = jnp.sum(_dot((e0 + e1).astype(_BF16), lower), axis=0, keepdims=True)
    pos0 = jnp.sum(e0 * (base + c0), axis=1, keepdims=True)
    pos1 = jnp.sum(e1 * (base + tot0 + c1), axis=1, keepdims=True)
    pos = jnp.where(lane == 0, pos0, jnp.where(lane == 1, pos1, 0.0))
    pos_ref[...] = pos
    pos_t = pos.T
    slot_id = lax.broadcasted_iota(jnp.int32, (MOE_SLOTS, MOE_TILE), 0).astype(_F32)
    pick = jnp.where((slot_id == pos_t[0:1, :]) | (slot_id == pos_t[1:2, :]), 1.0, 0.0).astype(_BF16)
    rows = _dot(pick, h)

    @pl.when(i >= 2)
    def _():
        _wait_slot(stage, sem, slot)

    _store_slot_major(stage.at[slot], rows)
    _copy_runs(cnt_ref, off_ref, i, lambda src, dst, n: pltpu.make_async_copy(
        stage.at[slot, _slot_rows(src, n)], xs_hbm.at[_slot_rows(dst, n)], sem.at[slot]))

    @pl.when(i == pl.num_programs(0) - 1)
    def _():
        _wait_slot(stage, sem, slot)

        @pl.when(i >= 1)
        def _():
            _wait_slot(stage, sem, 1 - slot)


def _dispatch(routed, routed_c, plan, n_blocks):
    d = routed[0].shape[1]
    n_main = routed[0].shape[0] // MOE_TILE
    n_ctx = 0 if routed_c is None else routed_c[0].shape[0] // MOE_TILE
    n_tiles = n_main + n_ctx
    t = n_tiles * MOE_TILE
    pair = lambda f: [pl.BlockSpec((MOE_TILE, d), lambda i, *_: (f(i), 0)),
                      pl.BlockSpec((MOE_TILE, ROUTE_COLS), lambda i, *_: (f(i), 0))]
    if routed_c is None:
        body = lambda *refs: _dispatch_kernel(n_main, *refs[:6], None, None, *refs[6:])
    else:
        body = functools.partial(_dispatch_kernel, n_main)
    grid_spec = pltpu.PrefetchScalarGridSpec(
        num_scalar_prefetch=4,
        grid=(n_tiles,),
        in_specs=pair(lambda i: jnp.minimum(i, n_main - 1)) + ([] if routed_c is None else
                                                              pair(lambda i: jnp.maximum(i - n_main, 0))),
        out_specs=[pl.BlockSpec(memory_space=pl.ANY), pl.BlockSpec((MOE_TILE, ROUTE_COLS), lambda i, *_: (i, 0))],
        scratch_shapes=[pltpu.VMEM((2, MOE_SLOTS * SLOT_SUBROWS, LANES), _F32), pltpu.SemaphoreType.DMA((2,)),
                        pltpu.VMEM((MOE_ROWS * SLOT_SUBROWS, LANES), _F32), pltpu.SemaphoreType.DMA(())],
    )
    return pl.pallas_call(
        body,
        grid_spec=grid_spec,
        out_shape=[jax.ShapeDtypeStruct((n_blocks * MOE_ROWS * SLOT_SUBROWS, LANES), _F32),
                   jax.ShapeDtypeStruct((t, ROUTE_COLS), _F32)],
        compiler_params=_cparams("arbitrary"),
        name="moe_dispatch",
    )(*plan, *routed, *(routed_c or ()))


def _expert_kernel(blk_e_ref, rows_ref, x_ref, w1_ref, w3_ref, w2_ref, o_ref, w1b, w3b, w2b):
    i = pl.program_id(0)
    rows = rows_ref[i]

    @pl.when((i == 0) | (blk_e_ref[i] != blk_e_ref[jnp.maximum(i - 1, 0)]))
    def _():
        w1b[...] = w1_ref[0, 0].astype(_BF16)
        w3b[...] = w3_ref[0, 0].astype(_BF16)
        w2b[...] = w2_ref[0, 0].astype(_BF16)

    def ffn(first, n_slots):
        span = pl.ds(first * SLOT_SUBROWS, n_slots * SLOT_SUBROWS)
        xb = _load_slot_major(x_ref.at[span], n_slots).astype(_BF16)
        h1 = _dot(xb, w1b[...])
        h3 = _dot(xb, w3b[...])
        act = (h1 * jax.nn.sigmoid(h1) * h3).astype(_BF16)
        _store_slot_major(o_ref.at[span], _dot(act, w2b[...]))

    def clear(first, n_slots):
        o_ref[pl.ds(first * SLOT_SUBROWS, n_slots * SLOT_SUBROWS), :] = jnp.zeros((n_slots * SLOT_SUBROWS, LANES), o_ref.dtype)

    @pl.when(rows > MOE_HALF)
    def _():
        ffn(0, MOE_ROWS)

    @pl.when((rows > 0) & (rows <= MOE_HALF))
    def _():
        ffn(0, MOE_HALF)
        clear(MOE_HALF, MOE_HALF)

    @pl.when(rows == 0)
    def _():
        clear(0, MOE_ROWS)


def _experts(xs, blk_e, blk_rows, layer, w1, w3, w2):
    d = w1.shape[2]
    blk = MOE_ROWS * SLOT_SUBROWS
    n_blocks = xs.shape[0] // blk
    grid_spec = pltpu.PrefetchScalarGridSpec(
        num_scalar_prefetch=2,
        grid=(n_blocks,),
        in_specs=[
            pl.BlockSpec((blk, LANES), lambda i, be, br: (i, 0)),
            pl.BlockSpec((1, 1, d, EXPERT_FF), lambda i, be, br: (layer, be[i], 0, 0)),
            pl.BlockSpec((1, 1, d, EXPERT_FF), lambda i, be, br: (layer, be[i], 0, 0)),
            pl.BlockSpec((1, 1, EXPERT_FF, d), lambda i, be, br: (layer, be[i], 0, 0)),
        ],
        out_specs=pl.BlockSpec((blk, LANES), lambda i, be, br: (i, 0)),
        scratch_shapes=[pltpu.VMEM((d, EXPERT_FF), _BF16), pltpu.VMEM((d, EXPERT_FF), _BF16),
                        pltpu.VMEM((EXPERT_FF, d), _BF16)],
    )
    return pl.pallas_call(
        _expert_kernel,
        grid_spec=grid_spec,
        out_shape=jax.ShapeDtypeStruct(xs.shape, xs.dtype),
        compiler_params=_cparams("arbitrary"),
        name="experts",
    )(blk_e, blk_rows, xs, w1, w3, w2)


def _combine_kernel(final, tile0, off_ref, cnt_ref, x_ref, pos_ref, wt_ref, gt_ref, g_ref, ys_hbm, o_ref, stage, sem):
    i = pl.program_id(0)
    slot = i % 2

    def fetch(tile, s):
        _copy_runs(cnt_ref, off_ref, tile, lambda src, dst, n: pltpu.make_async_copy(
            ys_hbm.at[_slot_rows(dst, n)], stage.at[s, _slot_rows(src, n)], sem.at[s]))

    @pl.when(i == 0)
    def _():
        fetch(tile0, 0)

    @pl.when(i + 1 < pl.num_programs(0))
    def _():
        fetch(tile0 + i + 1, 1 - slot)

    _wait_slot(stage, sem, slot)
    y = _load_slot_major(stage.at[slot], MOE_SLOTS).astype(_BF16)
    slot_id = lax.broadcasted_iota(jnp.int32, (MOE_TILE, MOE_SLOTS), 1).astype(_F32)
    mix = (jnp.where(slot_id == pos_ref[:, 0:1], wt_ref[:, 0:1], 0.0)
           + jnp.where(slot_id == pos_ref[:, 1:2], wt_ref[:, 1:2], 0.0)).astype(_BF16)
    x = x_ref[...] + gt_ref[0] * _dot(mix, y)
    if final:
        x = (x * lax.rsqrt(jnp.mean(x * x, axis=-1, keepdims=True) + EPS)) * g_ref[...]
    o_ref[...] = x


def _combine(x, ys, pos, wts, off, cnt, gt, g_final, tok0, final):
    b, n, d = x.shape
    t = b * n
    per_b = n // MOE_TILE
    tile0 = tok0 // MOE_TILE
    row = pl.BlockSpec((MOE_TILE, d), lambda i, o, c: (i, 0))
    lanes = lambda first: pl.BlockSpec((MOE_TILE, ROUTE_COLS), lambda i, o, c: (first + i, 0))
    grid_spec = pltpu.PrefetchScalarGridSpec(
        num_scalar_prefetch=2,
        grid=(t // MOE_TILE,),
        in_specs=[row, lanes(tile0), lanes(0), pl.BlockSpec((1, 1, d), lambda i, o, c: (i // per_b, 0, 0)),
                  pl.BlockSpec((1, d), lambda i, o, c: (0, 0)), pl.BlockSpec(memory_space=pl.ANY)],
        out_specs=row,
        scratch_shapes=[pltpu.VMEM((2, MOE_SLOTS * SLOT_SUBROWS, LANES), _F32), pltpu.SemaphoreType.DMA((2,))],
    )
    out = pl.pallas_call(
        functools.partial(_combine_kernel, final, tile0),
        grid_spec=grid_spec,
        out_shape=jax.ShapeDtypeStruct((t, d), _F32),
        compiler_params=_cparams("arbitrary"),
        name="moe_combine",
    )(off, cnt, x.reshape(t, d), pos, wts, gt, g_final.reshape(1, d), ys)
    return out.reshape(b, n, d)


def _rotary_tables(n):
    n_freq = R_QK_DIM // 4
    inv = ROPE_BASE ** (-jnp.arange(n_freq, dtype=_F32) / n_freq)
    rows = n // GRID_W
    row = jnp.repeat(jnp.arange(rows, dtype=_F32), GRID_W)
    col = jnp.tile(jnp.arange(GRID_W, dtype=_F32), rows)
    ang = jnp.concatenate([row[:, None] * inv, col[:, None] * inv], axis=-1)
    cos, sin = jnp.cos(ang), jnp.sin(ang)
    return jnp.repeat(cos, 2, axis=-1), jnp.stack([-sin, sin], axis=-1).reshape(n, R_QK_DIM)


def _decay_tables(decay_logit, l_ctx):
    log_g = jax.nn.log_sigmoid(decay_logit.astype(_F32))
    lgf, lgb = log_g[0][:, None], log_g[1][:, None]
    j = jnp.arange(CHUNK, dtype=_F32)
    diff = j[:, None] - j[None, :]
    dmat = jnp.where(diff[None] >= 0, jnp.exp(lgf[:, :, None] * jnp.maximum(diff, 0.0)[None]),
                     jnp.exp(lgb[:, :, None] * jnp.maximum(-diff, 0.0)[None]))
    lanes = lambda t: jnp.broadcast_to(t[:, :, None], t.shape + (LANES,))
    m = jnp.arange(l_ctx, dtype=_F32)
    return {
        "dmat": dmat,
        "qdf": lanes(jnp.exp(lgf * (j + 1.0))), "qdb": lanes(jnp.exp(lgb * (CHUNK - j))),
        "kdf": lanes(jnp.exp(lgf * (CHUNK - 1.0 - j))), "kdb": lanes(jnp.exp(lgb * j)),
        "cdec": jnp.exp(log_g * CHUNK),
        "ctx_f": lanes(jnp.exp(lgf * ((l_ctx - 1.0) - m))), "ctx_b": lanes(jnp.exp(lgb * m)),
    }


def kernel(x, c, ctx, c_ctx, w_ada, b_ada, g_mix, g_ffn, w_in, w_s, b_s, g_sgu, decay_logit, w_pa, w_pb, w_out,
           w_group, b_group, w_erouter, b_erouter, w1, w3, w2, g_final):
    bsz, n, d = x.shape
    l_ctx = ctx.shape[1]
    t_main = bsz * n

    cond = jnp.zeros((COND_ROWS, d), _F32).at[:bsz].set(c).at[bsz].set(c_ctx)
    mods = _adaln(cond, w_ada, b_ada)
    rot = _rotary_tables(n)
    no_rot = (jnp.ones((l_ctx, LANES), _F32), jnp.zeros((l_ctx, LANES), _F32))
    zero_state = jnp.zeros((bsz, R_HEADS, R_QK_DIM, R_V_DIM), _F32)

    w_in_b = w_in.astype(_BF16)
    xc = ctx
    for l in range(DEPTH):
        update_ctx = l < DEPTH - 1
        m6 = mods[l].reshape(COND_ROWS, 6, d)
        main = [m6[:bsz, i][:, None, :] for i in range(6)]
        cmod = [jnp.broadcast_to(m6[bsz, i][None, None, :], (bsz, 1, d)) for i in range(6)]
        tabs = _decay_tables(decay_logit[l], l_ctx)
        gsgu = g_sgu[l].reshape(1, D_A)
        mix_w = (w_s[l].astype(_BF16), jnp.broadcast_to(b_s[l][:, :, None], (A_GROUPS, CHUNK, LANES)),
                 w_pa[l].astype(_BF16), w_pb[l].astype(_BF16), w_out[l].astype(_BF16))

        p = _proj_in(x, g_mix[l], main[0], main[1], gsgu, *rot, w_in_b, l, _MIX_SEGMENTS)
        if update_ctx:
            pc = _proj_in(xc, g_mix[l], cmod[0], cmod[1], gsgu, *no_rot, w_in_b, l, _MIX_SEGMENTS)
            s_ctx_f, s_ctx_b = _ctx_states(pc, _K_BLK, _VR_BLK, tabs["ctx_f"], tabs["ctx_b"])
        else:
            w_vk = jnp.concatenate([w_in_b[l:l + 1, :, KV_START + D_RQK:KV_END],
                                    w_in_b[l:l + 1, :, KV_START:KV_START + D_RQK]], axis=2)
            vk_c = _proj_in(xc, g_mix[l], cmod[0], cmod[1], gsgu, *no_rot, w_vk, 0, _VK_SEGMENTS)
            s_ctx_f, s_ctx_b = _ctx_states(vk_c, D_RV // D_RQK, 0, tabs["ctx_f"], tabs["ctx_b"])
        sb = _bwd_states(p, tabs["kdb"], tabs["cdec"], s_ctx_b)
        wr = jnp.zeros((d, ROUTE_COLS // 2), _F32).at[:, :MOE_GROUPS].set(w_group[l])
        wr = wr.at[:, MOE_GROUPS:MOE_GROUPS + N_EXPERTS].set(w_erouter[l])
        wr_hi = wr.astype(_BF16)
        wr = jnp.concatenate([wr_hi, (wr - wr_hi.astype(_F32)).astype(_BF16)], axis=1)
        br = jnp.zeros((1, ROUTE_COLS), _F32).at[0, :MOE_GROUPS].set(b_group[l])
        br = br.at[0, MOE_GROUPS:MOE_GROUPS + N_EXPERTS].set(b_erouter[l])
        route_w = (g_ffn[l], wr, br)
        x, h2, ids, wts, cnt = _token_mix(p, x, main[2], tabs, sb, s_ctx_f, *mix_w, route_w[0], main[3], main[4],
                                          *route_w[1:])
        routed_c = None
        if update_ctx:
            sbc = _bwd_states(pc, tabs["kdb"], tabs["cdec"], zero_state)
            xc, h2c, idsc, wtsc, cntc = _token_mix(pc, xc, cmod[2], tabs, sbc, zero_state, *mix_w, route_w[0],
                                                   cmod[3], cmod[4], *route_w[1:])
            routed_c = (h2c, idsc)
            cnt = jnp.concatenate([cnt, cntc], axis=0)

        plan, blk_e, blk_rows, n_blocks = _dispatch_plan(cnt[:, 0, :N_EXPERTS])
        off, cnt = plan[:2]
        xs, pos = _dispatch((h2, ids), routed_c, plan, n_blocks)
        ys = _experts(xs, blk_e, blk_rows, l, w1, w3, w2)
        final = l == DEPTH - 1
        x = _combine(x, ys, pos, wts, off, cnt, main[5], g_final, 0, final)
        if update_ctx:
            xc = _combine(xc, ys, pos, wtsc, off, cnt, cmod[5], g_final, t_main, False)
    return x
```

```python
import functools

import numpy as np
import jax
import jax.numpy as jnp
from jax import lax
from jax.experimental import pallas as pl
from jax.experimental.pallas import tpu as pltpu

D_MODEL = 1024
DEPTH = 2
GRID_W = 64
CHUNK = 128
A_GROUP_DIM = 128
A_GROUPS = D_MODEL // A_GROUP_DIM
D_A = A_GROUPS * A_GROUP_DIM
R_HEADS = 4
R_QK_DIM = D_MODEL // 8
R_V_DIM = D_MODEL // 4
D_RQK = R_HEADS * R_QK_DIM
D_RV = R_HEADS * R_V_DIM
D_IN = 2 * D_A + 2 * D_RQK + 2 * D_RV + 2 * D_MODEL
KV_START = 2 * D_A + D_RQK
KV_END = KV_START + D_RQK + D_RV
ROPE_BASE = 10000.0
MOE_GROUPS = 4
EXPERTS_PER_GROUP = 8
N_EXPERTS = MOE_GROUPS * EXPERTS_PER_GROUP
EXPERT_FF = D_MODEL // 2
EPS = 1e-6

LANES = 128
SUBLANES = 8
MOE_ROWS = 512
MOE_HALF = MOE_ROWS // 2
VMEM_LIMIT = 56 * 1024 * 1024
COND_ROWS = 16

_U_BLK, _VA_BLK = 0, 1
_Q_BLK, _K_BLK = 2 * D_A // D_RQK, 2 * D_A // D_RQK + 1
_VR_BLK = (2 * D_A + 2 * D_RQK) // D_RV
_GR_BLK, _GA_BLK, _GB_BLK = _VR_BLK + 1, _VR_BLK + 2, _VR_BLK + 3

_F32 = jnp.float32
_BF16 = jnp.bfloat16


def _cparams(*sem):
    return pltpu.CompilerParams(dimension_semantics=sem, vmem_limit_bytes=VMEM_LIMIT)


def _dot(a, b):
    return jnp.dot(a, b, preferred_element_type=_F32)


def _adaln_kernel(cond_ref, w_ref, b_ref, o_ref):
    cnd = cond_ref[...]
    s = cnd * jax.nn.sigmoid(cnd)
    o_ref[0] = jnp.dot(s, w_ref[0], preferred_element_type=_F32, precision=lax.Precision.HIGHEST) + b_ref[0]


def _adaln(cond, w_ada, b_ada):
    depth, d, d6 = w_ada.shape
    tn = 1024
    return pl.pallas_call(
        _adaln_kernel,
        grid=(depth, d6 // tn),
        in_specs=[
            pl.BlockSpec((COND_ROWS, d), lambda l, j: (0, 0)),
            pl.BlockSpec((1, d, tn), lambda l, j: (l, 0, j)),
            pl.BlockSpec((1, 1, tn), lambda l, j: (l, 0, j)),
        ],
        out_specs=pl.BlockSpec((1, COND_ROWS, tn), lambda l, j: (l, 0, j)),
        out_shape=jax.ShapeDtypeStruct((depth, COND_ROWS, d6), _F32),
        compiler_params=_cparams("parallel", "parallel"),
        name="adaln",
    )(cond, w_ada, b_ada.reshape(depth, 1, d6))


def _modnorm(x, g, shift, scale):
    y = x * lax.rsqrt(jnp.mean(x * x, axis=-1, keepdims=True) + EPS)
    return (y * g) * (1.0 + scale) + shift


def _layernorm(t):
    mu = jnp.mean(t, axis=-1, keepdims=True)
    tc = t - mu
    return tc * lax.rsqrt(jnp.mean(tc * tc, axis=-1, keepdims=True) + EPS)


def _rotate_heads(t, cos_i, sin_i, scale):
    even = lax.broadcasted_iota(jnp.int32, cos_i.shape, 1) % 2 == 0
    out = []
    for h in range(t.shape[1] // R_QK_DIM):
        th = t[:, h * R_QK_DIM:(h + 1) * R_QK_DIM]
        partner = jnp.where(even, pltpu.roll(th, R_QK_DIM - 1, 1), pltpu.roll(th, 1, 1))
        out.append((th * cos_i + partner * sin_i) * scale)
    return jnp.concatenate(out, axis=1)


def _proj_kernel(segments, x_ref, g_ref, sh_ref, sc_ref, gsgu_ref, cos_ref, sin_ref, w_ref, o_ref, h_ref):
    h_ref[...] = _modnorm(x_ref[0], g_ref[...], sh_ref[0], sc_ref[0]).astype(_BF16)
    for start, width, kind in segments:
        cols = slice(start, start + width)
        acc = _dot(h_ref[...], w_ref[0, :, cols])
        if kind == "gelu":
            acc = jax.nn.gelu(acc)
        elif kind == "gelu_ln":
            acc = _layernorm(jax.nn.gelu(acc)) * gsgu_ref[...]
        elif kind == "rot_q":
            acc = _rotate_heads(acc, cos_ref[...], sin_ref[...], 1.0)
        elif kind == "rot_k":
            acc = _rotate_heads(acc, cos_ref[...], sin_ref[...], R_QK_DIM ** -0.5)
        elif kind == "silu":
            acc = acc * jax.nn.sigmoid(acc)
        o_ref[0, :, cols] = acc.astype(_BF16)


_MIX_SEGMENTS = ((0, D_A, "gelu"), (D_A, D_A, "gelu_ln"), (2 * D_A, D_RQK, "rot_q"), (2 * D_A + D_RQK, D_RQK, "rot_k"),
                 (KV_END, D_RV, "silu"), (KV_START + D_RQK, D_RV, "plain"), (KV_END + D_RV, D_MODEL, "plain"),
                 (KV_END + D_RV + D_MODEL, D_MODEL, "plain"))
_VK_SEGMENTS = ((D_RV, D_RQK, "rot_k"), (0, D_RV, "plain"))


def _proj_in(x, g, shift, scale, gsgu, cos_i, sin_i, w, layer, segments):
    b, n, d = x.shape
    dout = w.shape[2]
    tm = min(n, 512)
    return pl.pallas_call(
        functools.partial(_proj_kernel, segments),
        grid=(b, n // tm),
        in_specs=[
            pl.BlockSpec((1, tm, d), lambda i, r: (i, r, 0)),
            pl.BlockSpec((1, d), lambda i, r: (0, 0)),
            pl.BlockSpec((1, 1, d), lambda i, r: (i, 0, 0)),
            pl.BlockSpec((1, 1, d), lambda i, r: (i, 0, 0)),
            pl.BlockSpec((1, D_A), lambda i, r: (0, 0)),
            pl.BlockSpec((tm, LANES), lambda i, r: (r, 0)),
            pl.BlockSpec((tm, LANES), lambda i, r: (r, 0)),
            pl.BlockSpec((1, d, dout), lambda i, r: (layer, 0, 0), pipeline_mode=pl.Buffered(1)),
        ],
        out_specs=pl.BlockSpec((1, tm, dout), lambda i, r: (i, r, 0)),
        out_shape=jax.ShapeDtypeStruct((b, n, dout), _BF16),
        scratch_shapes=[pltpu.VMEM((tm, d), _BF16)],
        compiler_params=_cparams("parallel", "parallel"),
        name="proj_in",
    )(x, g.reshape(1, d), shift, scale, gsgu, cos_i, sin_i, w)


def _dot_t(a, b):
    return lax.dot_general(a, b, (((0,), (0,)), ((), ())), preferred_element_type=_F32)


def _ctx_state_kernel(k_ref, v_ref, wf_ref, wb_ref, sf_ref, sb_ref):
    for h in range(R_HEADS):
        kh = k_ref[0, :, h * R_QK_DIM:(h + 1) * R_QK_DIM].astype(_F32)
        vh = v_ref[0, :, h * R_V_DIM:(h + 1) * R_V_DIM]
        sf_ref[0, h] = _dot_t((kh * wf_ref[h]).astype(_BF16), vh)
        sb_ref[0, h] = _dot_t((kh * wb_ref[h]).astype(_BF16), vh)


def _ctx_states(pc, k_blk, v_blk, wf, wb):
    b, l, _ = pc.shape
    st = jax.ShapeDtypeStruct((b, R_HEADS, R_QK_DIM, R_V_DIM), _F32)
    st_spec = pl.BlockSpec((1, R_HEADS, R_QK_DIM, R_V_DIM), lambda i: (i, 0, 0, 0))
    return pl.pallas_call(
        _ctx_state_kernel,
        grid=(b,),
        in_specs=[
            pl.BlockSpec((1, l, D_RQK), lambda i: (i, 0, k_blk)),
            pl.BlockSpec((1, l, D_RV), lambda i: (i, 0, v_blk)),
            pl.BlockSpec((R_HEADS, l, LANES), lambda i: (0, 0, 0)),
            pl.BlockSpec((R_HEADS, l, LANES), lambda i: (0, 0, 0)),
        ],
        out_specs=[st_spec, st_spec],
        out_shape=[st, st],
        compiler_params=_cparams("parallel"),
        name="ctx_states",
    )(pc, pc, wf, wb)


def _bwd_state_kernel(n_chunks, cdec_ref, k_ref, v_ref, kdb_ref, s0_ref, o_ref, s_ref):
    @pl.when(pl.program_id(1) == 0)
    def _():
        s_ref[...] = s0_ref[0]

    for c in reversed(range(n_chunks)):
        rows = pl.ds(c * CHUNK, CHUNK)
        for h in range(R_HEADS):
            s = s_ref[h]
            o_ref[0, c, h] = s.astype(_BF16)
            kh = k_ref[0, rows, h * R_QK_DIM:(h + 1) * R_QK_DIM].astype(_F32)
            vh = v_ref[0, rows, h * R_V_DIM:(h + 1) * R_V_DIM]
            s_ref[h] = s * cdec_ref[1, h] + _dot_t((kh * kdb_ref[h]).astype(_BF16), vh)


def _bwd_states(p, kdb, cdec, s0_b):
    b, n, _ = p.shape
    rows = min(n, 1024)
    ncb, nblk = rows // CHUNK, n // rows
    rev = lambda r: nblk - 1 - r
    return pl.pallas_call(
        functools.partial(_bwd_state_kernel, ncb),
        grid=(b, nblk),
        in_specs=[
            pl.BlockSpec(memory_space=pltpu.SMEM),
            pl.BlockSpec((1, rows, D_RQK), lambda i, r: (i, rev(r), _K_BLK)),
            pl.BlockSpec((1, rows, D_RV), lambda i, r: (i, rev(r), _VR_BLK)),
            pl.BlockSpec((R_HEADS, CHUNK, LANES), lambda i, r: (0, 0, 0)),
            pl.BlockSpec((1, R_HEADS, R_QK_DIM, R_V_DIM), lambda i, r: (i, 0, 0, 0)),
        ],
        out_specs=pl.BlockSpec((1, ncb, R_HEADS, R_QK_DIM, R_V_DIM), lambda i, r: (i, rev(r), 0, 0, 0)),
        out_shape=jax.ShapeDtypeStruct((b, n // CHUNK, R_HEADS, R_QK_DIM, R_V_DIM), _BF16),
        scratch_shapes=[pltpu.VMEM((R_HEADS, R_QK_DIM, R_V_DIM), _F32)],
        compiler_params=_cparams("parallel", "arbitrary"),
        name="bwd_states",
    )(cdec, p, p, kdb, s0_b)


def _mix_kernel(n_chunks, cdec_ref, u_ref, va_ref, q_ref, k_ref, vr_ref, gr_ref, ga_ref, gb_ref, x_ref, gt_ref,
                dmat_ref, qdf_ref, qdb_ref, kdf_ref, sb_ref, s0_ref,
                ws_ref, bs_ref, wpa_ref, wpb_ref, wout_ref, g2_ref, sh2_ref, sc2_ref, wr_ref, br_ref,
                o_ref, h_ref, id_ref, wt_ref, cnt_ref, sf_ref, ya_ref, yb_ref):
    @pl.when(pl.program_id(1) == 0)
    def _():
        sf_ref[...] = s0_ref[0]

    for c in range(n_chunks):
        rows = pl.ds(c * CHUNK, CHUNK)
        gu = u_ref[0, rows, :].astype(_F32)
        vn = va_ref[0, rows, :]
        for g in range(A_GROUPS):
            cols = slice(g * A_GROUP_DIM, (g + 1) * A_GROUP_DIM)
            s = _dot(ws_ref[g], vn[:, cols]) + bs_ref[g]
            ya_ref[rows, cols] = (gu[:, cols] * s).astype(_BF16)
        for h in range(R_HEADS):
            qk = slice(h * R_QK_DIM, (h + 1) * R_QK_DIM)
            vv = slice(h * R_V_DIM, (h + 1) * R_V_DIM)
            qb, kb = q_ref[0, rows, qk], k_ref[0, rows, qk]
            qh, kh = qb.astype(_F32), kb.astype(_F32)
            vh = vr_ref[0, rows, vv]
            scores = lax.dot_general(qb, kb, (((1,), (1,)), ((), ())), preferred_element_type=_F32) * dmat_ref[h]
            o = _dot(scores.astype(_BF16), vh)
            s_f = sf_ref[h]
            o = o + _dot((qh * qdf_ref[h]).astype(_BF16), s_f.astype(_BF16))
            o = o + _dot((qh * qdb_ref[h]).astype(_BF16), sb_ref[0, c, h])
            yb_ref[rows, vv] = (gr_ref[0, rows, vv].astype(_F32) * _layernorm(o)).astype(_BF16)
            sf_ref[h] = s_f * cdec_ref[0, h] + _dot_t((kh * kdf_ref[h]).astype(_BF16), vh)

    a = _dot(ya_ref[...], wpa_ref[...])
    b = _dot(yb_ref[...], wpb_ref[...])
    merged = jax.nn.sigmoid(ga_ref[0].astype(_F32)) * a + jax.nn.sigmoid(gb_ref[0].astype(_F32)) * b
    x_new = x_ref[0] + gt_ref[0] * _dot(merged.astype(_BF16), wout_ref[...])
    o_ref[0] = x_new
    _route(x_new, g2_ref, sh2_ref, sc2_ref, wr_ref, br_ref, h_ref, id_ref, wt_ref, cnt_ref)


def _token_mix(p, x, gt, tabs, sb, s0_f, ws, bsb, wpa, wpb, wout, g2, shift2, scale2, wr, br):
    b, n, d = x.shape
    t = b * n
    rows = min(n, 512)
    ncb = rows // CHUNK
    per_b = n // rows
    tiles = rows // MOE_TILE
    flat = lambda width: pl.BlockSpec((rows, width), lambda i, r: (i * per_b + r, 0))
    pspec = lambda width, blk: pl.BlockSpec((1, rows, width), lambda i, r: (i, r, blk))
    const = lambda shape: pl.BlockSpec(shape, lambda i, r: (0,) * len(shape))
    tab = const((R_HEADS, CHUNK, LANES))
    return pl.pallas_call(
        functools.partial(_mix_kernel, ncb),
        grid=(b, n // rows),
        in_specs=[
            pl.BlockSpec(memory_space=pltpu.SMEM),
            pspec(D_A, _U_BLK), pspec(D_A, _VA_BLK), pspec(D_RQK, _Q_BLK), pspec(D_RQK, _K_BLK),
            pspec(D_RV, _VR_BLK), pspec(D_RV, _GR_BLK), pspec(d, _GA_BLK), pspec(d, _GB_BLK),
            pl.BlockSpec((1, rows, d), lambda i, r: (i, r, 0)),
            pl.BlockSpec((1, 1, d), lambda i, r: (i, 0, 0)),
            tab, tab, tab, tab,
            pl.BlockSpec((1, ncb, R_HEADS, R_QK_DIM, R_V_DIM), lambda i, r: (i, r, 0, 0, 0)),
            pl.BlockSpec((1, R_HEADS, R_QK_DIM, R_V_DIM), lambda i, r: (i, 0, 0, 0)),
            const((A_GROUPS, CHUNK, CHUNK)), const((A_GROUPS, CHUNK, LANES)),
            const((D_A, d)), const((D_RV, d)), const((d, d)),
            const((1, d)), pl.BlockSpec((1, 1, d), lambda i, r: (i, 0, 0)), pl.BlockSpec((1, 1, d), lambda i, r: (i, 0, 0)),
            const((d, ROUTE_COLS)), const((1, ROUTE_COLS)),
        ],
        out_specs=[pl.BlockSpec((1, rows, d), lambda i, r: (i, r, 0)), flat(d), flat(ROUTE_COLS), flat(ROUTE_COLS),
                   pl.BlockSpec((tiles, SUBLANES, ROUTE_COLS), lambda i, r: (i * per_b + r, 0, 0))],
        out_shape=[jax.ShapeDtypeStruct((b, n, d), _F32), jax.ShapeDtypeStruct((t, d), _BF16),
                   jax.ShapeDtypeStruct((t, ROUTE_COLS), jnp.int32), jax.ShapeDtypeStruct((t, ROUTE_COLS), _F32),
                   jax.ShapeDtypeStruct((t // MOE_TILE, SUBLANES, ROUTE_COLS), jnp.int32)],
        scratch_shapes=[
            pltpu.VMEM((R_HEADS, R_QK_DIM, R_V_DIM), _F32),
            pltpu.VMEM((rows, D_A), _BF16),
            pltpu.VMEM((rows, D_RV), _BF16),
        ],
        compiler_params=_cparams("parallel", "arbitrary"),
        name="token_mix",
    )(tabs["cdec"], p, p, p, p, p, p, p, p, x, gt, tabs["dmat"], tabs["qdf"], tabs["qdb"], tabs["kdf"],
      sb, s0_f, ws, bsb, wpa, wpb, wout, g2.reshape(1, d), shift2, scale2, wr, br)


ROUTE_COLS = LANES
MOE_TILE = 256
MOE_SLOTS = 2 * MOE_TILE
SLOT_SUBROWS = D_MODEL // LANES


def _route(x, g_ref, sh_ref, sc_ref, wr_ref, br_ref, h_ref, id_ref, wt_ref, cnt_ref):
    h = _modnorm(x, g_ref[...], sh_ref[0], sc_ref[0])
    h_hi = h.astype(_BF16)
    h_ref[...] = h_hi
    h_lo = (h - h_hi.astype(_F32)).astype(_BF16)
    r_hi = _dot(h_hi, wr_ref[...])
    logits = r_hi + pltpu.roll(r_hi, ROUTE_COLS // 2, 1) + _dot(h_lo, wr_ref[...]) + br_ref[...]
    lane = lax.broadcasted_iota(jnp.int32, logits.shape, 1)
    neg = -jnp.inf
    gl = jnp.where(lane < MOE_GROUPS, logits, neg)
    g_top = jnp.max(gl, axis=-1, keepdims=True)
    g_sel = jnp.min(jnp.where(gl == g_top, lane, ROUTE_COLS), axis=-1, keepdims=True)
    lse = g_top + jnp.log(jnp.sum(jnp.exp(gl - g_top), axis=-1, keepdims=True))
    p_group = jnp.exp(g_top - lse)
    lo = MOE_GROUPS + g_sel * EXPERTS_PER_GROUP
    el = jnp.where((lane >= lo) & (lane < lo + EXPERTS_PER_GROUP), logits, neg)
    e1 = jnp.max(el, axis=-1, keepdims=True)
    i1 = jnp.min(jnp.where(el == e1, lane, ROUTE_COLS), axis=-1, keepdims=True)
    el2 = jnp.where(lane == i1, neg, el)
    e2 = jnp.max(el2, axis=-1, keepdims=True)
    i2 = jnp.min(jnp.where(el2 == e2, lane, ROUTE_COLS), axis=-1, keepdims=True)
    x2 = jnp.exp(e2 - e1)
    den = 1.0 + x2
    id1, id2 = i1 - MOE_GROUPS, i2 - MOE_GROUPS
    id_ref[...] = jnp.where(lane == 0, id1, jnp.where(lane == 1, id2, 0))
    wt_ref[...] = jnp.where(lane == 0, p_group * (1.0 / den), jnp.where(lane == 1, p_group * (x2 / den), 0.0))
    hits = jnp.where((lane == id1) | (lane == id2), 1, 0)
    for t in range(cnt_ref.shape[0]):
        tile_hits = jnp.sum(hits[t * MOE_TILE:(t + 1) * MOE_TILE], axis=0, keepdims=True)
        cnt_ref[t] = jnp.broadcast_to(tile_hits, (SUBLANES, ROUTE_COLS))


def _dispatch_plan(cnt):
    n_tiles = cnt.shape[0]
    a = n_tiles * MOE_SLOTS
    tot = jnp.sum(cnt, axis=0)
    padded = (tot + MOE_ROWS - 1) // MOE_ROWS * MOE_ROWS
    pend = jnp.cumsum(padded)
    pstart = pend - padded
    off = pstart[None, :] + jnp.cumsum(cnt, axis=0) - cnt
    n_blocks = (a + N_EXPERTS * (MOE_ROWS - 1) + MOE_ROWS - 1) // MOE_ROWS
    blk_start = jnp.arange(n_blocks, dtype=jnp.int32) * MOE_ROWS
    blk_e = jnp.minimum(jnp.sum((pend[None, :] <= blk_start[:, None]).astype(jnp.int32), axis=1), N_EXPERTS - 1)
    owner = (blk_e[:, None] == jnp.arange(N_EXPERTS, dtype=jnp.int32)[None, :]).astype(jnp.int32)
    used_end = jnp.sum(owner * (pstart + tot)[None, :], axis=1)
    blk_rows = jnp.clip(used_end - blk_start, 0, MOE_ROWS)
    pad_start = jnp.concatenate([pstart + tot, blk_start])
    pad_len = jnp.concatenate([padded - tot, jnp.where(blk_start >= pend[-1], MOE_ROWS, 0)])
    i32 = lambda v: v.astype(jnp.int32)
    return (i32(off).reshape(-1), i32(cnt).reshape(-1), i32(pad_start), i32(pad_len)), i32(blk_e), i32(blk_rows), n_blocks


def _copy_runs(cnt_ref, off_ref, tile, make_copy):
    def body(e, src):
        n = cnt_ref[tile * N_EXPERTS + e]

        @pl.when(n > 0)
        def _():
            make_copy(src, off_ref[tile * N_EXPERTS + e], n).start()
        return src + n
    lax.fori_loop(0, N_EXPERTS, body, 0)


def _slot_rows(first, n):
    s = SLOT_SUBROWS
    return pl.ds(pl.multiple_of(first * s, s), pl.multiple_of(n * s, s))


def _load_slot_major(ref, n_slots):
    return jnp.concatenate([ref[pl.ds(s, n_slots, stride=SLOT_SUBROWS), :] for s in range(SLOT_SUBROWS)], axis=1)


def _store_slot_major(ref, val):
    for s in range(SLOT_SUBROWS):
        ref[pl.ds(s, val.shape[0], stride=SLOT_SUBROWS), :] = val[:, s * LANES:(s + 1) * LANES]


def _wait_slot(stage, sem, s):
    pltpu.make_async_copy(stage.at[s], stage.at[s], sem.at[s]).wait()


def _dispatch_kernel(n_main, off_ref, cnt_ref, pad_start_ref, pad_len_ref, h_ref, id_ref, hc_ref, idc_ref,
                     xs_hbm, pos_ref, stage, sem, zeros, zsem):
    i = pl.program_id(0)
    slot = i % 2
    from_main = i < n_main
    ids = id_ref[...] if idc_ref is None else jnp.where(from_main, id_ref[...], idc_ref[...])
    h = h_ref[...] if hc_ref is None else jnp.where(from_main, h_ref[...], hc_ref[...])

    @pl.when(i == 0)
    def _():
        zeros[...] = jnp.zeros(zeros.shape, zeros.dtype)

        def pad_copy(r):
            n = pad_len_ref[r]
            return n, pltpu.make_async_copy(zeros.at[_slot_rows(0, n)], xs_hbm.at[_slot_rows(pad_start_ref[r], n)], zsem)

        def start(r, carry):
            n, copy = pad_copy(r)

            @pl.when(n > 0)
            def _():
                copy.start()
            return carry

        def wait(r, carry):
            n, copy = pad_copy(r)

            @pl.when(n > 0)
            def _():
                copy.wait()
            return carry

        lax.fori_loop(0, pad_len_ref.shape[0], start, 0)
        lax.fori_loop(0, pad_len_ref.shape[0], wait, 0)

    lane = lax.broadcasted_iota(jnp.int32, ids.shape, 1)
    e0 = jnp.where(lane == ids[:, 0:1], 1.0, 0.0)
    e1 = jnp.where(lane == ids[:, 1:2], 1.0, 0.0)
    tt = lax.broadcasted_iota(jnp.int32, (MOE_TILE, MOE_TILE), 0)
    earlier = jnp.where(lax.broadcasted_iota(jnp.int32, (MOE_TILE, MOE_TILE), 1) < tt, 1.0, 0.0).astype(_BF16)
    c0 = _dot(earlier, e0.astype(_BF16))
    c1 = _dot(earlier, e1.astype(_BF16))
    tot0 = jnp.sum(e0, axis=0, keepdims=True)
    ee = lax.broadcasted_iota(jnp.int32, (ROUTE_COLS, ROUTE_COLS), 0)
    lower = jnp.where(ee < lax.broadcasted_iota(jnp.int32, (ROUTE_COLS, ROUTE_COLS), 1), 1.0, 0.0).astype(_BF16)
    base = jnp.sum(_dot((e0 + e1).astype(_BF16), lower), axis=0, keepdims=True)
    pos0 = jnp.sum(e0 * (base + c0), axis=1, keepdims=True)
    pos1 = jnp.sum(e1 * (base + tot0 + c1), axis=1, keepdims=True)
    pos = jnp.where(lane == 0, pos0, jnp.where(lane == 1, pos1, 0.0))
    pos_ref[...] = pos
    pos_t = pos.T
    slot_id = lax.broadcasted_iota(jnp.int32, (MOE_SLOTS, MOE_TILE), 0).astype(_F32)
    pick = jnp.where((slot_id == pos_t[0:1, :]) | (slot_id == pos_t[1:2, :]), 1.0, 0.0).astype(_BF16)
    rows = _dot(pick, h)

    @pl.when(i >= 2)
    def _():
        _wait_slot(stage, sem, slot)

    _store_slot_major(stage.at[slot], rows)
    _copy_runs(cnt_ref, off_ref, i, lambda src, dst, n: pltpu.make_async_copy(
        stage.at[slot, _slot_rows(src, n)], xs_hbm.at[_slot_rows(dst, n)], sem.at[slot]))

    @pl.when(i == pl.num_programs(0) - 1)
    def _():
        _wait_slot(stage, sem, slot)

        @pl.when(i >= 1)
        def _():
            _wait_slot(stage, sem, 1 - slot)


def _dispatch(routed, routed_c, plan, n_blocks):
    d = routed[0].shape[1]
    n_main = routed[0].shape[0] // MOE_TILE
    n_ctx = 0 if routed_c is None else routed_c[0].shape[0] // MOE_TILE
    n_tiles = n_main + n_ctx
    t = n_tiles * MOE_TILE
    pair = lambda f: [pl.BlockSpec((MOE_TILE, d), lambda i, *_: (f(i), 0)),
                      pl.BlockSpec((MOE_TILE, ROUTE_COLS), lambda i, *_: (f(i), 0))]
    if routed_c is None:
        body = lambda *refs: _dispatch_kernel(n_main, *refs[:6], None, None, *refs[6:])
    else:
        body = functools.partial(_dispatch_kernel, n_main)
    grid_spec = pltpu.PrefetchScalarGridSpec(
        num_scalar_prefetch=4,
        grid=(n_tiles,),
        in_specs=pair(lambda i: jnp.minimum(i, n_main - 1)) + ([] if routed_c is None else
                                                              pair(lambda i: jnp.maximum(i - n_main, 0))),
        out_specs=[pl.BlockSpec(memory_space=pl.ANY), pl.BlockSpec((MOE_TILE, ROUTE_COLS), lambda i, *_: (i, 0))],
        scratch_shapes=[pltpu.VMEM((2, MOE_SLOTS * SLOT_SUBROWS, LANES), _F32), pltpu.SemaphoreType.DMA((2,)),
                        pltpu.VMEM((MOE_ROWS * SLOT_SUBROWS, LANES), _F32), pltpu.SemaphoreType.DMA(())],
    )
    return pl.pallas_call(
        body,
        grid_spec=grid_spec,
        out_shape=[jax.ShapeDtypeStruct((n_blocks * MOE_ROWS * SLOT_SUBROWS, LANES), _F32),
                   jax.ShapeDtypeStruct((t, ROUTE_COLS), _F32)],
        compiler_params=_cparams("arbitrary"),
        name="moe_dispatch",
    )(*plan, *routed, *(routed_c or ()))


def _expert_kernel(blk_e_ref, rows_ref, x_ref, w1_ref, w3_ref, w2_ref, o_ref, w1b, w3b, w2b):
    i = pl.program_id(0)
    rows = rows_ref[i]

    @pl.when((i == 0) | (blk_e_ref[i] != blk_e_ref[jnp.maximum(i - 1, 0)]))
    def _():
        w1b[...] = w1_ref[0, 0].astype(_BF16)
        w3b[...] = w3_ref[0, 0].astype(_BF16)
        w2b[...] = w2_ref[0, 0].astype(_BF16)

    def ffn(first, n_slots):
        span = pl.ds(first * SLOT_SUBROWS, n_slots * SLOT_SUBROWS)
        xb = _load_slot_major(x_ref.at[span], n_slots).astype(_BF16)
        h1 = _dot(xb, w1b[...])
        h3 = _dot(xb, w3b[...])
        act = (h1 * jax.nn.sigmoid(h1) * h3).astype(_BF16)
        _store_slot_major(o_ref.at[span], _dot(act, w2b[...]))

    def clear(first, n_slots):
        o_ref[pl.ds(first * SLOT_SUBROWS, n_slots * SLOT_SUBROWS), :] = jnp.zeros((n_slots * SLOT_SUBROWS, LANES), o_ref.dtype)

    @pl.when(rows > MOE_HALF)
    def _():
        ffn(0, MOE_ROWS)

    @pl.when((rows > 0) & (rows <= MOE_HALF))
    def _():
        ffn(0, MOE_HALF)
        clear(MOE_HALF, MOE_HALF)

    @pl.when(rows == 0)
    def _():
        clear(0, MOE_ROWS)


def _experts(xs, blk_e, blk_rows, layer, w1, w3, w2):
    d = w1.shape[2]
    blk = MOE_ROWS * SLOT_SUBROWS
    n_blocks = xs.shape[0] // blk
    grid_spec = pltpu.PrefetchScalarGridSpec(
        num_scalar_prefetch=2,
        grid=(n_blocks,),
        in_specs=[
            pl.BlockSpec((blk, LANES), lambda i, be, br: (i, 0)),
            pl.BlockSpec((1, 1, d, EXPERT_FF), lambda i, be, br: (layer, be[i], 0, 0)),
            pl.BlockSpec((1, 1, d, EXPERT_FF), lambda i, be, br: (layer, be[i], 0, 0)),
            pl.BlockSpec((1, 1, EXPERT_FF, d), lambda i, be, br: (layer, be[i], 0, 0)),
        ],
        out_specs=pl.BlockSpec((blk, LANES), lambda i, be, br: (i, 0)),
        scratch_shapes=[pltpu.VMEM((d, EXPERT_FF), _BF16), pltpu.VMEM((d, EXPERT_FF), _BF16),
                        pltpu.VMEM((EXPERT_FF, d), _BF16)],
    )
    return pl.pallas_call(
        _expert_kernel,
        grid_spec=grid_spec,
        out_shape=jax.ShapeDtypeStruct(xs.shape, xs.dtype),
        compiler_params=_cparams("arbitrary"),
        name="experts",
    )(blk_e, blk_rows, xs, w1, w3, w2)


def _combine_kernel(final, tile0, off_ref, cnt_ref, x_ref, pos_ref, wt_ref, gt_ref, g_ref, ys_hbm, o_ref, stage, sem):
    i = pl.program_id(0)
    slot = i % 2

    def fetch(tile, s):
        _copy_runs(cnt_ref, off_ref, tile, lambda src, dst, n: pltpu.make_async_copy(
            ys_hbm.at[_slot_rows(dst, n)], stage.at[s, _slot_rows(src, n)], sem.at[s]))

    @pl.when(i == 0)
    def _():
        fetch(tile0, 0)

    @pl.when(i + 1 < pl.num_programs(0))
    def _():
        fetch(tile0 + i + 1, 1 - slot)

    _wait_slot(stage, sem, slot)
    y = _load_slot_major(stage.at[slot], MOE_SLOTS).astype(_BF16)
    slot_id = lax.broadcasted_iota(jnp.int32, (MOE_TILE, MOE_SLOTS), 1).astype(_F32)
    mix = (jnp.where(slot_id == pos_ref[:, 0:1], wt_ref[:, 0:1], 0.0)
           + jnp.where(slot_id == pos_ref[:, 1:2], wt_ref[:, 1:2], 0.0)).astype(_BF16)
    x = x_ref[...] + gt_ref[0] * _dot(mix, y)
    if final:
        x = (x * lax.rsqrt(jnp.mean(x * x, axis=-1, keepdims=True) + EPS)) * g_ref[...]
    o_ref[...] = x


def _combine(x, ys, pos, wts, off, cnt, gt, g_final, tok0, final):
    b, n, d = x.shape
    t = b * n
    per_b = n // MOE_TILE
    tile0 = tok0 // MOE_TILE
    row = pl.BlockSpec((MOE_TILE, d), lambda i, o, c: (i, 0))
    lanes = lambda first: pl.BlockSpec((MOE_TILE, ROUTE_COLS), lambda i, o, c: (first + i, 0))
    grid_spec = pltpu.PrefetchScalarGridSpec(
        num_scalar_prefetch=2,
        grid=(t // MOE_TILE,),
        in_specs=[row, lanes(tile0), lanes(0), pl.BlockSpec((1, 1, d), lambda i, o, c: (i // per_b, 0, 0)),
                  pl.BlockSpec((1, d), lambda i, o, c: (0, 0)), pl.BlockSpec(memory_space=pl.ANY)],
        out_specs=row,
        scratch_shapes=[pltpu.VMEM((2, MOE_SLOTS * SLOT_SUBROWS, LANES), _F32), pltpu.SemaphoreType.DMA((2,))],
    )
    out = pl.pallas_call(
        functools.partial(_combine_kernel, final, tile0),
        grid_spec=grid_spec,
        out_shape=jax.ShapeDtypeStruct((t, d), _F32),
        compiler_params=_cparams("arbitrary"),
        name="moe_combine",
    )(off, cnt, x.reshape(t, d), pos, wts, gt, g_final.reshape(1, d), ys)
    return out.reshape(b, n, d)


def _rotary_tables(n):
    n_freq = R_QK_DIM // 4
    inv = ROPE_BASE ** (-jnp.arange(n_freq, dtype=_F32) / n_freq)
    rows = n // GRID_W
    row = jnp.repeat(jnp.arange(rows, dtype=_F32), GRID_W)
    col = jnp.tile(jnp.arange(GRID_W, dtype=_F32), rows)
    ang = jnp.concatenate([row[:, None] * inv, col[:, None] * inv], axis=-1)
    cos, sin = jnp.cos(ang), jnp.sin(ang)
    return jnp.repeat(cos, 2, axis=-1), jnp.stack([-sin, sin], axis=-1).reshape(n, R_QK_DIM)


def _decay_tables(decay_logit, l_ctx):
    log_g = jax.nn.log_sigmoid(decay_logit.astype(_F32))
    lgf, lgb = log_g[0][:, None], log_g[1][:, None]
    j = jnp.arange(CHUNK, dtype=_F32)
    diff = j[:, None] - j[None, :]
    dmat = jnp.where(diff[None] >= 0, jnp.exp(lgf[:, :, None] * jnp.maximum(diff, 0.0)[None]),
                     jnp.exp(lgb[:, :, None] * jnp.maximum(-diff, 0.0)[None]))
    lanes = lambda t: jnp.broadcast_to(t[:, :, None], t.shape + (LANES,))
    m = jnp.arange(l_ctx, dtype=_F32)
    return {
        "dmat": dmat,
        "qdf": lanes(jnp.exp(lgf * (j + 1.0))), "qdb": lanes(jnp.exp(lgb * (CHUNK - j))),
        "kdf": lanes(jnp.exp(lgf * (CHUNK - 1.0 - j))), "kdb": lanes(jnp.exp(lgb * j)),
        "cdec": jnp.exp(log_g * CHUNK),
        "ctx_f": lanes(jnp.exp(lgf * ((l_ctx - 1.0) - m))), "ctx_b": lanes(jnp.exp(lgb * m)),
    }


def kernel(x, c, ctx, c_ctx, w_ada, b_ada, g_mix, g_ffn, w_in, w_s, b_s, g_sgu, decay_logit, w_pa, w_pb, w_out,
           w_group, b_group, w_erouter, b_erouter, w1, w3, w2, g_final):
    bsz, n, d = x.shape
    l_ctx = ctx.shape[1]
    t_main = bsz * n

    cond = jnp.zeros((COND_ROWS, d), _F32).at[:bsz].set(c).at[bsz].set(c_ctx)
    mods = _adaln(cond, w_ada, b_ada)
    rot = _rotary_tables(n)
    no_rot = (jnp.ones((l_ctx, LANES), _F32), jnp.zeros((l_ctx, LANES), _F32))
    zero_state = jnp.zeros((bsz, R_HEADS, R_QK_DIM, R_V_DIM), _F32)

    w_in_b = w_in.astype(_BF16)
    xc = ctx
    for l in range(DEPTH):
        update_ctx = l < DEPTH - 1
        m6 = mods[l].reshape(COND_ROWS, 6, d)
        main = [m6[:bsz, i][:, None, :] for i in range(6)]
        cmod = [jnp.broadcast_to(m6[bsz, i][None, None, :], (bsz, 1, d)) for i in range(6)]
        tabs = _decay_tables(decay_logit[l], l_ctx)
        gsgu = g_sgu[l].reshape(1, D_A)
        mix_w = (w_s[l].astype(_BF16), jnp.broadcast_to(b_s[l][:, :, None], (A_GROUPS, CHUNK, LANES)),
                 w_pa[l].astype(_BF16), w_pb[l].astype(_BF16), w_out[l].astype(_BF16))

        p = _proj_in(x, g_mix[l], main[0], main[1], gsgu, *rot, w_in_b, l, _MIX_SEGMENTS)
        if update_ctx:
            pc = _proj_in(xc, g_mix[l], cmod[0], cmod[1], gsgu, *no_rot, w_in_b, l, _MIX_SEGMENTS)
            s_ctx_f, s_ctx_b = _ctx_states(pc, _K_BLK, _VR_BLK, tabs["ctx_f"], tabs["ctx_b"])
        else:
            w_vk = jnp.concatenate([w_in_b[l:l + 1, :, KV_START + D_RQK:KV_END],
                                    w_in_b[l:l + 1, :, KV_START:KV_START + D_RQK]], axis=2)
            vk_c = _proj_in(xc, g_mix[l], cmod[0], cmod[1], gsgu, *no_rot, w_vk, 0, _VK_SEGMENTS)
            s_ctx_f, s_ctx_b = _ctx_states(vk_c, D_RV // D_RQK, 0, tabs["ctx_f"], tabs["ctx_b"])
        sb = _bwd_states(p, tabs["kdb"], tabs["cdec"], s_ctx_b)
        wr = jnp.zeros((d, ROUTE_COLS // 2), _F32).at[:, :MOE_GROUPS].set(w_group[l])
        wr = wr.at[:, MOE_GROUPS:MOE_GROUPS + N_EXPERTS].set(w_erouter[l])
        wr_hi = wr.astype(_BF16)
        wr = jnp.concatenate([wr_hi, (wr - wr_hi.astype(_F32)).astype(_BF16)], axis=1)
        br = jnp.zeros((1, ROUTE_COLS), _F32).at[0, :MOE_GROUPS].set(b_group[l])
        br = br.at[0, MOE_GROUPS:MOE_GROUPS + N_EXPERTS].set(b_erouter[l])
        route_w = (g_ffn[l], wr, br)
        x, h2, ids, wts, cnt = _token_mix(p, x, main[2], tabs, sb, s_ctx_f, *mix_w, route_w[0], main[3], main[4],
                                          *route_w[1:])
        routed_c = None
        if update_ctx:
            sbc = _bwd_states(pc, tabs["kdb"], tabs["cdec"], zero_state)
            xc, h2c, idsc, wtsc, cntc = _token_mix(pc, xc, cmod[2], tabs, sbc, zero_state, *mix_w, route_w[0],
                                                   cmod[3], cmod[4], *route_w[1:])
            routed_c = (h2c, idsc)
            cnt = jnp.concatenate([cnt, cntc], axis=0)

        plan, blk_e, blk_rows, n_blocks = _dispatch_plan(cnt[:, 0, :N_EXPERTS])
        off, cnt = plan[:2]
        xs, pos = _dispatch((h2, ids), routed_c, plan, n_blocks)
        ys = _experts(xs, blk_e, blk_rows, l, w1, w3, w2)
        final = l == DEPTH - 1
        x = _combine(x, ys, pos, wts, off, cnt, main[5], g_final, 0, final)
        if update_ctx:
            xc = _combine(xc, ys, pos, wtsc, off, cnt, cmod[5], g_final, t_main, False)
    return x
```

```python
import functools

import numpy as np
import jax
import jax.numpy as jnp
from jax import lax
from jax.experimental import pallas as pl
from jax.experimental.pallas import tpu as pltpu

D_MODEL = 1024
DEPTH = 2
GRID_W = 64
CHUNK = 128
A_GROUP_DIM = 128
A_GROUPS = D_MODEL // A_GROUP_DIM
D_A = A_GROUPS * A_GROUP_DIM
R_HEADS = 4
R_QK_DIM = D_MODEL // 8
R_V_DIM = D_MODEL // 4
D_RQK = R_HEADS * R_QK_DIM
D_RV = R_HEADS * R_V_DIM
D_IN = 2 * D_A + 2 * D_RQK + 2 * D_RV + 2 * D_MODEL
KV_START = 2 * D_A + D_RQK
KV_END = KV_START + D_RQK + D_RV
ROPE_BASE = 10000.0
MOE_GROUPS = 4
EXPERTS_PER_GROUP = 8
N_EXPERTS = MOE_GROUPS * EXPERTS_PER_GROUP
EXPERT_FF = D_MODEL // 2
EPS = 1e-6

LANES = 128
SUBLANES = 8
MOE_ROWS = 512
MOE_HALF = MOE_ROWS // 2
VMEM_LIMIT = 56 * 1024 * 1024
COND_ROWS = 16

_U_BLK, _VA_BLK = 0, 1
_Q_BLK, _K_BLK = 2 * D_A // D_RQK, 2 * D_A // D_RQK + 1
_VR_BLK = (2 * D_A + 2 * D_RQK) // D_RV
_GR_BLK, _GA_BLK, _GB_BLK = _VR_BLK + 1, _VR_BLK + 2, _VR_BLK + 3

_F32 = jnp.float32
_BF16 = jnp.bfloat16


def _cparams(*sem):
    return pltpu.CompilerParams(dimension_semantics=sem, vmem_limit_bytes=VMEM_LIMIT)


def _dot(a, b):
    return jnp.dot(a, b, preferred_element_type=_F32)


def _adaln_kernel(cond_ref, w_ref, b_ref, o_ref):
    cnd = cond_ref[...]
    s = cnd * jax.nn.sigmoid(cnd)
    o_ref[0] = jnp.dot(s, w_ref[0], preferred_element_type=_F32, precision=lax.Precision.HIGHEST) + b_ref[0]


def _adaln(cond, w_ada, b_ada):
    depth, d, d6 = w_ada.shape
    tn = 1024
    return pl.pallas_call(
        _adaln_kernel,
        grid=(depth, d6 // tn),
        in_specs=[
            pl.BlockSpec((COND_ROWS, d), lambda l, j: (0, 0)),
            pl.BlockSpec((1, d, tn), lambda l, j: (l, 0, j)),
            pl.BlockSpec((1, 1, tn), lambda l, j: (l, 0, j)),
        ],
        out_specs=pl.BlockSpec((1, COND_ROWS, tn), lambda l, j: (l, 0, j)),
        out_shape=jax.ShapeDtypeStruct((depth, COND_ROWS, d6), _F32),
        compiler_params=_cparams("parallel", "parallel"),
        name="adaln",
    )(cond, w_ada, b_ada.reshape(depth, 1, d6))


def _modnorm(x, g, shift, scale):
    y = x * lax.rsqrt(jnp.mean(x * x, axis=-1, keepdims=True) + EPS)
    return (y * g) * (1.0 + scale) + shift


def _layernorm(t):
    mu = jnp.mean(t, axis=-1, keepdims=True)
    tc = t - mu
    return tc * lax.rsqrt(jnp.mean(tc * tc, axis=-1, keepdims=True) + EPS)


def _rotate_heads(t, cos_i, sin_i, scale):
    even = lax.broadcasted_iota(jnp.int32, cos_i.shape, 1) % 2 == 0
    out = []
    for h in range(t.shape[1] // R_QK_DIM):
        th = t[:, h * R_QK_DIM:(h + 1) * R_QK_DIM]
        partner = jnp.where(even, pltpu.roll(th, R_QK_DIM - 1, 1), pltpu.roll(th, 1, 1))
        out.append((th * cos_i + partner * sin_i) * scale)
    return jnp.concatenate(out, axis=1)


def _proj_kernel(segments, x_ref, g_ref, sh_ref, sc_ref, gsgu_ref, cos_ref, sin_ref, w_ref, o_ref, h_ref):
    h_ref[...] = _modnorm(x_ref[0], g_ref[...], sh_ref[0], sc_ref[0]).astype(_BF16)
    for start, width, kind in segments:
        cols = slice(start, start + width)
        acc = _dot(h_ref[...], w_ref[0, :, cols])
        if kind == "gelu_ln":
            acc = _layernorm(jax.nn.gelu(acc)) * gsgu_ref[...]
        elif kind == "rot_q":
            acc = _rotate_heads(acc, cos_ref[...], sin_ref[...], 1.0)
        elif kind == "rot_k":
            acc = _rotate_heads(acc, cos_ref[...], sin_ref[...], R_QK_DIM ** -0.5)
        o_ref[0, :, cols] = acc.astype(_BF16)


_MIX_SEGMENTS = ((D_A, D_A, "gelu_ln"), (2 * D_A, D_RQK, "rot_q"), (2 * D_A + D_RQK, D_RQK, "rot_k"),
                 (0, D_A, "plain"), (KV_START + D_RQK, D_RV, "plain"), (KV_END, D_RV, "plain"),
                 (KV_END + D_RV, D_MODEL, "plain"), (KV_END + D_RV + D_MODEL, D_MODEL, "plain"))
_VK_SEGMENTS = ((D_RV, D_RQK, "rot_k"), (0, D_RV, "plain"))


def _proj_in(x, g, shift, scale, gsgu, cos_i, sin_i, w, layer, segments):
    b, n, d = x.shape
    dout = w.shape[2]
    tm = min(n, 512)
    return pl.pallas_call(
        functools.partial(_proj_kernel, segments),
        grid=(b, n // tm),
        in_specs=[
            pl.BlockSpec((1, tm, d), lambda i, r: (i, r, 0)),
            pl.BlockSpec((1, d), lambda i, r: (0, 0)),
            pl.BlockSpec((1, 1, d), lambda i, r: (i, 0, 0)),
            pl.BlockSpec((1, 1, d), lambda i, r: (i, 0, 0)),
            pl.BlockSpec((1, D_A), lambda i, r: (0, 0)),
            pl.BlockSpec((tm, LANES), lambda i, r: (r, 0)),
            pl.BlockSpec((tm, LANES), lambda i, r: (r, 0)),
            pl.BlockSpec((1, d, dout), lambda i, r: (layer, 0, 0), pipeline_mode=pl.Buffered(1)),
        ],
        out_specs=pl.BlockSpec((1, tm, dout), lambda i, r: (i, r, 0)),
        out_shape=jax.ShapeDtypeStruct((b, n, dout), _BF16),
        scratch_shapes=[pltpu.VMEM((tm, d), _BF16)],
        compiler_params=_cparams("parallel", "parallel"),
        name="proj_in",
    )(x, g.reshape(1, d), shift, scale, gsgu, cos_i, sin_i, w)


def _dot_t(a, b):
    return lax.dot_general(a, b, (((0,), (0,)), ((), ())), preferred_element_type=_F32)


def _ctx_state_kernel(k_ref, v_ref, wf_ref, wb_ref, sf_ref, sb_ref):
    for h in range(R_HEADS):
        kh = k_ref[0, :, h * R_QK_DIM:(h + 1) * R_QK_DIM].astype(_F32)
        vh = v_ref[0, :, h * R_V_DIM:(h + 1) * R_V_DIM]
        sf_ref[0, h] = _dot_t((kh * wf_ref[h]).astype(_BF16), vh)
        sb_ref[0, h] = _dot_t((kh * wb_ref[h]).astype(_BF16), vh)


def _ctx_states(pc, k_blk, v_blk, wf, wb):
    b, l, _ = pc.shape
    st = jax.ShapeDtypeStruct((b, R_HEADS, R_QK_DIM, R_V_DIM), _F32)
    st_spec = pl.BlockSpec((1, R_HEADS, R_QK_DIM, R_V_DIM), lambda i: (i, 0, 0, 0))
    return pl.pallas_call(
        _ctx_state_kernel,
        grid=(b,),
        in_specs=[
            pl.BlockSpec((1, l, D_RQK), lambda i: (i, 0, k_blk)),
            pl.BlockSpec((1, l, D_RV), lambda i: (i, 0, v_blk)),
            pl.BlockSpec((R_HEADS, l, LANES), lambda i: (0, 0, 0)),
            pl.BlockSpec((R_HEADS, l, LANES), lambda i: (0, 0, 0)),
        ],
        out_specs=[st_spec, st_spec],
        out_shape=[st, st],
        compiler_params=_cparams("parallel"),
        name="ctx_states",
    )(pc, pc, wf, wb)


def _bwd_state_kernel(n_chunks, cdec_ref, k_ref, v_ref, kdb_ref, s0_ref, o_ref, s_ref):
    @pl.when(pl.program_id(1) == 0)
    def _():
        s_ref[...] = s0_ref[0]

    for c in reversed(range(n_chunks)):
        rows = pl.ds(c * CHUNK, CHUNK)
        for h in range(R_HEADS):
            s = s_ref[h]
            o_ref[0, c, h] = s.astype(_BF16)
            kh = k_ref[0, rows, h * R_QK_DIM:(h + 1) * R_QK_DIM].astype(_F32)
            vh = v_ref[0, rows, h * R_V_DIM:(h + 1) * R_V_DIM]
            s_ref[h] = s * cdec_ref[1, h] + _dot_t((kh * kdb_ref[h]).astype(_BF16), vh)


def _bwd_states(p, kdb, cdec, s0_b):
    b, n, _ = p.shape
    rows = min(n, 1024)
    ncb, nblk = rows // CHUNK, n // rows
    rev = lambda r: nblk - 1 - r
    return pl.pallas_call(
        functools.partial(_bwd_state_kernel, ncb),
        grid=(b, nblk),
        in_specs=[
            pl.BlockSpec(memory_space=pltpu.SMEM),
            pl.BlockSpec((1, rows, D_RQK), lambda i, r: (i, rev(r), _K_BLK)),
            pl.BlockSpec((1, rows, D_RV), lambda i, r: (i, rev(r), _VR_BLK)),
            pl.BlockSpec((R_HEADS, CHUNK, LANES), lambda i, r: (0, 0, 0)),
            pl.BlockSpec((1, R_HEADS, R_QK_DIM, R_V_DIM), lambda i, r: (i, 0, 0, 0)),
        ],
        out_specs=pl.BlockSpec((1, ncb, R_HEADS, R_QK_DIM, R_V_DIM), lambda i, r: (i, rev(r), 0, 0, 0)),
        out_shape=jax.ShapeDtypeStruct((b, n // CHUNK, R_HEADS, R_QK_DIM, R_V_DIM), _BF16),
        scratch_shapes=[pltpu.VMEM((R_HEADS, R_QK_DIM, R_V_DIM), _F32)],
        compiler_params=_cparams("parallel", "arbitrary"),
        name="bwd_states",
    )(cdec, p, p, kdb, s0_b)


def _mix_kernel(n_chunks, cdec_ref, u_ref, va_ref, q_ref, k_ref, vr_ref, gr_ref, ga_ref, gb_ref, x_ref, gt_ref,
                dmat_ref, qdf_ref, qdb_ref, kdf_ref, sb_ref, s0_ref,
                ws_ref, bs_ref, wpa_ref, wpb_ref, wout_ref, g2_ref, sh2_ref, sc2_ref, wr_ref, br_ref,
                o_ref, h_ref, id_ref, wt_ref, cnt_ref, sf_ref, sa_ref, lo_ref):
    @pl.when(pl.program_id(1) == 0)
    def _():
        sf_ref[...] = s0_ref[0]

    for c in range(n_chunks):
        rows = pl.ds(c * CHUNK, CHUNK)
        vn = va_ref[0, rows, :]
        for g in range(A_GROUPS):
            cols = slice(g * A_GROUP_DIM, (g + 1) * A_GROUP_DIM)
            sa_ref[rows, cols] = _dot(ws_ref[g], vn[:, cols]) + bs_ref[g]
        for h in range(R_HEADS):
            qk = slice(h * R_QK_DIM, (h + 1) * R_QK_DIM)
            vv = slice(h * R_V_DIM, (h + 1) * R_V_DIM)
            qb, kb = q_ref[0, rows, qk], k_ref[0, rows, qk]
            qh, kh = qb.astype(_F32), kb.astype(_F32)
            vh = vr_ref[0, rows, vv]
            scores = lax.dot_general(qb, kb, (((1,), (1,)), ((), ())), preferred_element_type=_F32) * dmat_ref[h]
            o = _dot(scores.astype(_BF16), vh)
            s_f = sf_ref[h]
            o = o + _dot((qh * qdf_ref[h]).astype(_BF16), s_f.astype(_BF16))
            o = o + _dot((qh * qdb_ref[h]).astype(_BF16), sb_ref[0, c, h])
            lo_ref[rows, vv] = _layernorm(o)
            sf_ref[h] = s_f * cdec_ref[0, h] + _dot_t((kh * kdf_ref[h]).astype(_BF16), vh)

    gr = gr_ref[0].astype(_F32)
    a = _dot((jax.nn.gelu(u_ref[0].astype(_F32)) * sa_ref[...]).astype(_BF16), wpa_ref[...])
    b = _dot((gr * jax.nn.sigmoid(gr) * lo_ref[...]).astype(_BF16), wpb_ref[...])
    merged = jax.nn.sigmoid(ga_ref[0].astype(_F32)) * a + jax.nn.sigmoid(gb_ref[0].astype(_F32)) * b
    x_new = x_ref[0] + gt_ref[0] * _dot(merged.astype(_BF16), wout_ref[...])
    o_ref[0] = x_new
    _route(x_new, g2_ref, sh2_ref, sc2_ref, wr_ref, br_ref, h_ref, id_ref, wt_ref, cnt_ref)


def _token_mix(p, x, gt, tabs, sb, s0_f, ws, bsb, wpa, wpb, wout, g2, shift2, scale2, wr, br):
    b, n, d = x.shape
    t = b * n
    rows = min(n, 512)
    ncb = rows // CHUNK
    per_b = n // rows
    tiles = rows // MOE_TILE
    flat = lambda width: pl.BlockSpec((rows, width), lambda i, r: (i * per_b + r, 0))
    pspec = lambda width, blk: pl.BlockSpec((1, rows, width), lambda i, r: (i, r, blk))
    const = lambda shape: pl.BlockSpec(shape, lambda i, r: (0,) * len(shape))
    tab = const((R_HEADS, CHUNK, LANES))
    return pl.pallas_call(
        functools.partial(_mix_kernel, ncb),
        grid=(b, n // rows),
        in_specs=[
            pl.BlockSpec(memory_space=pltpu.SMEM),
            pspec(D_A, _U_BLK), pspec(D_A, _VA_BLK), pspec(D_RQK, _Q_BLK), pspec(D_RQK, _K_BLK),
            pspec(D_RV, _VR_BLK), pspec(D_RV, _GR_BLK), pspec(d, _GA_BLK), pspec(d, _GB_BLK),
            pl.BlockSpec((1, rows, d), lambda i, r: (i, r, 0)),
            pl.BlockSpec((1, 1, d), lambda i, r: (i, 0, 0)),
            tab, tab, tab, tab,
            pl.BlockSpec((1, ncb, R_HEADS, R_QK_DIM, R_V_DIM), lambda i, r: (i, r, 0, 0, 0)),
            pl.BlockSpec((1, R_HEADS, R_QK_DIM, R_V_DIM), lambda i, r: (i, 0, 0, 0)),
            const((A_GROUPS, CHUNK, CHUNK)), const((A_GROUPS, CHUNK, LANES)),
            const((D_A, d)), const((D_RV, d)), const((d, d)),
            const((1, d)), pl.BlockSpec((1, 1, d), lambda i, r: (i, 0, 0)), pl.BlockSpec((1, 1, d), lambda i, r: (i, 0, 0)),
            const((d, ROUTE_COLS)), const((1, ROUTE_COLS)),
        ],
        out_specs=[pl.BlockSpec((1, rows, d), lambda i, r: (i, r, 0)), flat(d), flat(ROUTE_COLS), flat(ROUTE_COLS),
                   pl.BlockSpec((tiles, SUBLANES, ROUTE_COLS), lambda i, r: (i * per_b + r, 0, 0))],
        out_shape=[jax.ShapeDtypeStruct((b, n, d), _F32), jax.ShapeDtypeStruct((t, d), _BF16),
                   jax.ShapeDtypeStruct((t, ROUTE_COLS), jnp.int32), jax.ShapeDtypeStruct((t, ROUTE_COLS), _F32),
                   jax.ShapeDtypeStruct((t // MOE_TILE, SUBLANES, ROUTE_COLS), jnp.int32)],
        scratch_shapes=[
            pltpu.VMEM((R_HEADS, R_QK_DIM, R_V_DIM), _F32),
            pltpu.VMEM((rows, D_A), _F32),
            pltpu.VMEM((rows, D_RV), _F32),
        ],
        compiler_params=_cparams("parallel", "arbitrary"),
        name="token_mix",
    )(tabs["cdec"], p, p, p, p, p, p, p, p, x, gt, tabs["dmat"], tabs["qdf"], tabs["qdb"], tabs["kdf"],
      sb, s0_f, ws, bsb, wpa, wpb, wout, g2.reshape(1, d), shift2, scale2, wr, br)


ROUTE_COLS = LANES
MOE_TILE = 256
MOE_SLOTS = 2 * MOE_TILE
SLOT_SUBROWS = D_MODEL // LANES


def _route(x, g_ref, sh_ref, sc_ref, wr_ref, br_ref, h_ref, id_ref, wt_ref, cnt_ref):
    h = _modnorm(x, g_ref[...], sh_ref[0], sc_ref[0])
    h_hi = h.astype(_BF16)
    h_ref[...] = h_hi
    h_lo = (h - h_hi.astype(_F32)).astype(_BF16)
    r_hi = _dot(h_hi, wr_ref[...])
    logits = r_hi + pltpu.roll(r_hi, ROUTE_COLS // 2, 1) + _dot(h_lo, wr_ref[...]) + br_ref[...]
    lane = lax.broadcasted_iota(jnp.int32, logits.shape, 1)
    neg = -jnp.inf
    gl = jnp.where(lane < MOE_GROUPS, logits, neg)
    g_top = jnp.max(gl, axis=-1, keepdims=True)
    g_sel = jnp.min(jnp.where(gl == g_top, lane, ROUTE_COLS), axis=-1, keepdims=True)
    lse = g_top + jnp.log(jnp.sum(jnp.exp(gl - g_top), axis=-1, keepdims=True))
    p_group = jnp.exp(g_top - lse)
    lo = MOE_GROUPS + g_sel * EXPERTS_PER_GROUP
    el = jnp.where((lane >= lo) & (lane < lo + EXPERTS_PER_GROUP), logits, neg)
    e1 = jnp.max(el, axis=-1, keepdims=True)
    i1 = jnp.min(jnp.where(el == e1, lane, ROUTE_COLS), axis=-1, keepdims=True)
    el2 = jnp.where(lane == i1, neg, el)
    e2 = jnp.max(el2, axis=-1, keepdims=True)
    i2 = jnp.min(jnp.where(el2 == e2, lane, ROUTE_COLS), axis=-1, keepdims=True)
    x2 = jnp.exp(e2 - e1)
    den = 1.0 + x2
    id1, id2 = i1 - MOE_GROUPS, i2 - MOE_GROUPS
    id_ref[...] = jnp.where(lane == 0, id1, jnp.where(lane == 1, id2, 0))
    wt_ref[...] = jnp.where(lane == 0, p_group * (1.0 / den), jnp.where(lane == 1, p_group * (x2 / den), 0.0))
    hits = jnp.where((lane == id1) | (lane == id2), 1, 0)
    for t in range(cnt_ref.shape[0]):
        tile_hits = jnp.sum(hits[t * MOE_TILE:(t + 1) * MOE_TILE], axis=0, keepdims=True)
        cnt_ref[t] = jnp.broadcast_to(tile_hits, (SUBLANES, ROUTE_COLS))


def _dispatch_plan(cnt):
    n_tiles = cnt.shape[0]
    a = n_tiles * MOE_SLOTS
    tot = jnp.sum(cnt, axis=0)
    padded = (tot + MOE_ROWS - 1) // MOE_ROWS * MOE_ROWS
    pend = jnp.cumsum(padded)
    pstart = pend - padded
    off = pstart[None, :] + jnp.cumsum(cnt, axis=0) - cnt
    n_blocks = (a + N_EXPERTS * (MOE_ROWS - 1) + MOE_ROWS - 1) // MOE_ROWS
    blk_start = jnp.arange(n_blocks, dtype=jnp.int32) * MOE_ROWS
    blk_e = jnp.minimum(jnp.sum((pend[None, :] <= blk_start[:, None]).astype(jnp.int32), axis=1), N_EXPERTS - 1)
    owner = (blk_e[:, None] == jnp.arange(N_EXPERTS, dtype=jnp.int32)[None, :]).astype(jnp.int32)
    used_end = jnp.sum(owner * (pstart + tot)[None, :], axis=1)
    blk_rows = jnp.clip(used_end - blk_start, 0, MOE_ROWS)
    pad_start = jnp.concatenate([pstart + tot, blk_start])
    pad_len = jnp.concatenate([padded - tot, jnp.where(blk_start >= pend[-1], MOE_ROWS, 0)])
    i32 = lambda v: v.astype(jnp.int32)
    return (i32(off).reshape(-1), i32(cnt).reshape(-1), i32(pad_start), i32(pad_len)), i32(blk_e), i32(blk_rows), n_blocks


def _copy_runs(cnt_ref, off_ref, tile, make_copy):
    def body(e, src):
        n = cnt_ref[tile * N_EXPERTS + e]

        @pl.when(n > 0)
        def _():
            make_copy(src, off_ref[tile * N_EXPERTS + e], n).start()
        return src + n
    lax.fori_loop(0, N_EXPERTS, body, 0)


def _slot_rows(first, n):
    s = SLOT_SUBROWS
    return pl.ds(pl.multiple_of(first * s, s), pl.multiple_of(n * s, s))


def _load_slot_major(ref, n_slots):
    return jnp.concatenate([ref[pl.ds(s, n_slots, stride=SLOT_SUBROWS), :] for s in range(SLOT_SUBROWS)], axis=1)


def _store_slot_major(ref, val):
    for s in range(SLOT_SUBROWS):
        ref[pl.ds(s, val.shape[0], stride=SLOT_SUBROWS), :] = val[:, s * LANES:(s + 1) * LANES]


def _wait_slot(stage, sem, s):
    pltpu.make_async_copy(stage.at[s], stage.at[s], sem.at[s]).wait()


def _dispatch_kernel(n_main, off_ref, cnt_ref, pad_start_ref, pad_len_ref, h_ref, id_ref, hc_ref, idc_ref,
                     xs_hbm, pos_ref, stage, sem, zeros, zsem):
    i = pl.program_id(0)
    slot = i % 2
    from_main = i < n_main
    ids = id_ref[...] if idc_ref is None else jnp.where(from_main, id_ref[...], idc_ref[...])
    h = h_ref[...] if hc_ref is None else jnp.where(from_main, h_ref[...], hc_ref[...])

    @pl.when(i == 0)
    def _():
        zeros[...] = jnp.zeros(zeros.shape, zeros.dtype)

        def pad_copy(r):
            n = pad_len_ref[r]
            return n, pltpu.make_async_copy(zeros.at[_slot_rows(0, n)], xs_hbm.at[_slot_rows(pad_start_ref[r], n)], zsem)

        def start(r, carry):
            n, copy = pad_copy(r)

            @pl.when(n > 0)
            def _():
                copy.start()
            return carry

        def wait(r, carry):
            n, copy = pad_copy(r)

            @pl.when(n > 0)
            def _():
                copy.wait()
            return carry

        lax.fori_loop(0, pad_len_ref.shape[0], start, 0)
        lax.fori_loop(0, pad_len_ref.shape[0], wait, 0)

    lane = lax.broadcasted_iota(jnp.int32, ids.shape, 1)
    e0 = jnp.where(lane == ids[:, 0:1], 1.0, 0.0)
    e1 = jnp.where(lane == ids[:, 1:2], 1.0, 0.0)
    tt = lax.broadcasted_iota(jnp.int32, (MOE_TILE, MOE_TILE), 0)
    earlier = jnp.where(lax.broadcasted_iota(jnp.int32, (MOE_TILE, MOE_TILE), 1) < tt, 1.0, 0.0).astype(_BF16)
    c0 = _dot(earlier, e0.astype(_BF16))
    c1 = _dot(earlier, e1.astype(_BF16))
    tot0 = jnp.sum(e0, axis=0, keepdims=True)
    ee = lax.broadcasted_iota(jnp.int32, (ROUTE_COLS, ROUTE_COLS), 0)
    lower = jnp.where(ee < lax.broadcasted_iota(jnp.int32, (ROUTE_COLS, ROUTE_COLS), 1), 1.0, 0.0).astype(_BF16)
    base = jnp.sum(_dot((e0 + e1).astype(_BF16), lower), axis=0, keepdims=True)
    pos0 = jnp.sum(e0 * (base + c0), axis=1, keepdims=True)
    pos1 = jnp.sum(e1 * (base + tot0 + c1), axis=1, keepdims=True)
    pos = jnp.where(lane == 0, pos0, jnp.where(lane == 1, pos1, 0.0))
    pos_ref[...] = pos
    pos_t = pos.T
    slot_id = lax.broadcasted_iota(jnp.int32, (MOE_SLOTS, MOE_TILE), 0).astype(_F32)
    pick = jnp.where((slot_id == pos_t[0:1, :]) | (slot_id == pos_t[1:2, :]), 1.0, 0.0).astype(_BF16)
    rows = _dot(pick, h)

    @pl.when(i >= 2)
    def _():
        _wait_slot(stage, sem, slot)

    _store_slot_major(stage.at[slot], rows)
    _copy_runs(cnt_ref, off_ref, i, lambda src, dst, n: pltpu.make_async_copy(
        stage.at[slot, _slot_rows(src, n)], xs_hbm.at[_slot_rows(dst, n)], sem.at[slot]))

    @pl.when(i == pl.num_programs(0) - 1)
    def _():
        _wait_slot(stage, sem, slot)

        @pl.when(i >= 1)
        def _():
            _wait_slot(stage, sem, 1 - slot)


def _dispatch(routed, routed_c, plan, n_blocks):
    d = routed[0].shape[1]
    n_main = routed[0].shape[0] // MOE_TILE
    n_ctx = 0 if routed_c is None else routed_c[0].shape[0] // MOE_TILE
    n_tiles = n_main + n_ctx
    t = n_tiles * MOE_TILE
    pair = lambda f: [pl.BlockSpec((MOE_TILE, d), lambda i, *_: (f(i), 0)),
                      pl.BlockSpec((MOE_TILE, ROUTE_COLS), lambda i, *_: (f(i), 0))]
    if routed_c is None:
        body = lambda *refs: _dispatch_kernel(n_main, *refs[:6], None, None, *refs[6:])
    else:
        body = functools.partial(_dispatch_kernel, n_main)
    grid_spec = pltpu.PrefetchScalarGridSpec(
        num_scalar_prefetch=4,
        grid=(n_tiles,),
        in_specs=pair(lambda i: jnp.minimum(i, n_main - 1)) + ([] if routed_c is None else
                                                              pair(lambda i: jnp.maximum(i - n_main, 0))),
        out_specs=[pl.BlockSpec(memory_space=pl.ANY), pl.BlockSpec((MOE_TILE, ROUTE_COLS), lambda i, *_: (i, 0))],
        scratch_shapes=[pltpu.VMEM((2, MOE_SLOTS * SLOT_SUBROWS, LANES), _F32), pltpu.SemaphoreType.DMA((2,)),
                        pltpu.VMEM((MOE_ROWS * SLOT_SUBROWS, LANES), _F32), pltpu.SemaphoreType.DMA(())],
    )
    return pl.pallas_call(
        body,
        grid_spec=grid_spec,
        out_shape=[jax.ShapeDtypeStruct((n_blocks * MOE_ROWS * SLOT_SUBROWS, LANES), _F32),
                   jax.ShapeDtypeStruct((t, ROUTE_COLS), _F32)],
        compiler_params=_cparams("arbitrary"),
        name="moe_dispatch",
    )(*plan, *routed, *(routed_c or ()))


def _expert_kernel(blk_e_ref, rows_ref, x_ref, w1_ref, w3_ref, w2_ref, o_ref, w1b, w3b, w2b):
    i = pl.program_id(0)
    rows = rows_ref[i]

    @pl.when((i == 0) | (blk_e_ref[i] != blk_e_ref[jnp.maximum(i - 1, 0)]))
    def _():
        w1b[...] = w1_ref[0, 0].astype(_BF16)
        w3b[...] = w3_ref[0, 0].astype(_BF16)
        w2b[...] = w2_ref[0, 0].astype(_BF16)

    def ffn(first, n_slots):
        span = pl.ds(first * SLOT_SUBROWS, n_slots * SLOT_SUBROWS)
        xb = _load_slot_major(x_ref.at[span], n_slots).astype(_BF16)
        h1 = _dot(xb, w1b[...])
        h3 = _dot(xb, w3b[...])
        act = (h1 * jax.nn.sigmoid(h1) * h3).astype(_BF16)
        _store_slot_major(o_ref.at[span], _dot(act, w2b[...]))

    def clear(first, n_slots):
        o_ref[pl.ds(first * SLOT_SUBROWS, n_slots * SLOT_SUBROWS), :] = jnp.zeros((n_slots * SLOT_SUBROWS, LANES), o_ref.dtype)

    @pl.when(rows > MOE_HALF)
    def _():
        ffn(0, MOE_ROWS)

    @pl.when((rows > 0) & (rows <= MOE_HALF))
    def _():
        ffn(0, MOE_HALF)
        clear(MOE_HALF, MOE_HALF)

    @pl.when(rows == 0)
    def _():
        clear(0, MOE_ROWS)


def _experts(xs, blk_e, blk_rows, layer, w1, w3, w2):
    d = w1.shape[2]
    blk = MOE_ROWS * SLOT_SUBROWS
    n_blocks = xs.shape[0] // blk
    grid_spec = pltpu.PrefetchScalarGridSpec(
        num_scalar_prefetch=2,
        grid=(n_blocks,),
        in_specs=[
            pl.BlockSpec((blk, LANES), lambda i, be, br: (i, 0)),
            pl.BlockSpec((1, 1, d, EXPERT_FF), lambda i, be, br: (layer, be[i], 0, 0)),
            pl.BlockSpec((1, 1, d, EXPERT_FF), lambda i, be, br: (layer, be[i], 0, 0)),
            pl.BlockSpec((1, 1, EXPERT_FF, d), lambda i, be, br: (layer, be[i], 0, 0)),
        ],
        out_specs=pl.BlockSpec((blk, LANES), lambda i, be, br: (i, 0)),
        scratch_shapes=[pltpu.VMEM((d, EXPERT_FF), _BF16), pltpu.VMEM((d, EXPERT_FF), _BF16),
                        pltpu.VMEM((EXPERT_FF, d), _BF16)],
    )
    return pl.pallas_call(
        _expert_kernel,
        grid_spec=grid_spec,
        out_shape=jax.ShapeDtypeStruct(xs.shape, xs.dtype),
        compiler_params=_cparams("arbitrary"),
        name="experts",
    )(blk_e, blk_rows, xs, w1, w3, w2)


def _combine_kernel(final, tile0, off_ref, cnt_ref, x_ref, pos_ref, wt_ref, gt_ref, g_ref, ys_hbm, o_ref, stage, sem):
    i = pl.program_id(0)
    slot = i % 2

    def fetch(tile, s):
        _copy_runs(cnt_ref, off_ref, tile, lambda src, dst, n: pltpu.make_async_copy(
            ys_hbm.at[_slot_rows(dst, n)], stage.at[s, _slot_rows(src, n)], sem.at[s]))

    @pl.when(i == 0)
    def _():
        fetch(tile0, 0)

    @pl.when(i + 1 < pl.num_programs(0))
    def _():
        fetch(tile0 + i + 1, 1 - slot)

    _wait_slot(stage, sem, slot)
    y = _load_slot_major(stage.at[slot], MOE_SLOTS).astype(_BF16)
    slot_id = lax.broadcasted_iota(jnp.int32, (MOE_TILE, MOE_SLOTS), 1).astype(_F32)
    mix = (jnp.where(slot_id == pos_ref[:, 0:1], wt_ref[:, 0:1], 0.0)
           + jnp.where(slot_id == pos_ref[:, 1:2], wt_ref[:, 1:2], 0.0)).astype(_BF16)
    x = x_ref[...] + gt_ref[0] * _dot(mix, y)
    if final:
        x = (x * lax.rsqrt(jnp.mean(x * x, axis=-1, keepdims=True) + EPS)) * g_ref[...]
    o_ref[...] = x


def _combine(x, ys, pos, wts, off, cnt, gt, g_final, tok0, final):
    b, n, d = x.shape
    t = b * n
    per_b = n // MOE_TILE
    tile0 = tok0 // MOE_TILE
    row = pl.BlockSpec((MOE_TILE, d), lambda i, o, c: (i, 0))
    lanes = lambda first: pl.BlockSpec((MOE_TILE, ROUTE_COLS), lambda i, o, c: (first + i, 0))
    grid_spec = pltpu.PrefetchScalarGridSpec(
        num_scalar_prefetch=2,
        grid=(t // MOE_TILE,),
        in_specs=[row, lanes(tile0), lanes(0), pl.BlockSpec((1, 1, d), lambda i, o, c: (i // per_b, 0, 0)),
                  pl.BlockSpec((1, d), lambda i, o, c: (0, 0)), pl.BlockSpec(memory_space=pl.ANY)],
        out_specs=row,
        scratch_shapes=[pltpu.VMEM((2, MOE_SLOTS * SLOT_SUBROWS, LANES), _F32), pltpu.SemaphoreType.DMA((2,))],
    )
    out = pl.pallas_call(
        functools.partial(_combine_kernel, final, tile0),
        grid_spec=grid_spec,
        out_shape=jax.ShapeDtypeStruct((t, d), _F32),
        compiler_params=_cparams("arbitrary"),
        name="moe_combine",
    )(off, cnt, x.reshape(t, d), pos, wts, gt, g_final.reshape(1, d), ys)
    return out.reshape(b, n, d)


def _rotary_tables(n):
    n_freq = R_QK_DIM // 4
    inv = ROPE_BASE ** (-jnp.arange(n_freq, dtype=_F32) / n_freq)
    rows = n // GRID_W
    row = jnp.repeat(jnp.arange(rows, dtype=_F32), GRID_W)
    col = jnp.tile(jnp.arange(GRID_W, dtype=_F32), rows)
    ang = jnp.concatenate([row[:, None] * inv, col[:, None] * inv], axis=-1)
    cos, sin = jnp.cos(ang), jnp.sin(ang)
    return jnp.repeat(cos, 2, axis=-1), jnp.stack([-sin, sin], axis=-1).reshape(n, R_QK_DIM)


def _decay_tables(decay_logit, l_ctx):
    log_g = jax.nn.log_sigmoid(decay_logit.astype(_F32))
    lgf, lgb = log_g[0][:, None], log_g[1][:, None]
    j = jnp.arange(CHUNK, dtype=_F32)
    diff = j[:, None] - j[None, :]
    dmat = jnp.where(diff[None] >= 0, jnp.exp(lgf[:, :, None] * jnp.maximum(diff, 0.0)[None]),
                     jnp.exp(lgb[:, :, None] * jnp.maximum(-diff, 0.0)[None]))
    lanes = lambda t: jnp.broadcast_to(t[:, :, None], t.shape + (LANES,))
    m = jnp.arange(l_ctx, dtype=_F32)
    return {
        "dmat": dmat,
        "qdf": lanes(jnp.exp(lgf * (j + 1.0))), "qdb": lanes(jnp.exp(lgb * (CHUNK - j))),
        "kdf": lanes(jnp.exp(lgf * (CHUNK - 1.0 - j))), "kdb": lanes(jnp.exp(lgb * j)),
        "cdec": jnp.exp(log_g * CHUNK),
        "ctx_f": lanes(jnp.exp(lgf * ((l_ctx - 1.0) - m))), "ctx_b": lanes(jnp.exp(lgb * m)),
    }


def kernel(x, c, ctx, c_ctx, w_ada, b_ada, g_mix, g_ffn, w_in, w_s, b_s, g_sgu, decay_logit, w_pa, w_pb, w_out,
           w_group, b_group, w_erouter, b_erouter, w1, w3, w2, g_final):
    bsz, n, d = x.shape
    l_ctx = ctx.shape[1]
    t_main = bsz * n

    cond = jnp.zeros((COND_ROWS, d), _F32).at[:bsz].set(c).at[bsz].set(c_ctx)
    mods = _adaln(cond, w_ada, b_ada)
    rot = _rotary_tables(n)
    no_rot = (jnp.ones((l_ctx, LANES), _F32), jnp.zeros((l_ctx, LANES), _F32))
    zero_state = jnp.zeros((bsz, R_HEADS, R_QK_DIM, R_V_DIM), _F32)

    w_in_b = w_in.astype(_BF16)
    xc = ctx
    for l in range(DEPTH):
        update_ctx = l < DEPTH - 1
        m6 = mods[l].reshape(COND_ROWS, 6, d)
        main = [m6[:bsz, i][:, None, :] for i in range(6)]
        cmod = [jnp.broadcast_to(m6[bsz, i][None, None, :], (bsz, 1, d)) for i in range(6)]
        tabs = _decay_tables(decay_logit[l], l_ctx)
        gsgu = g_sgu[l].reshape(1, D_A)
        mix_w = (w_s[l].astype(_BF16), jnp.broadcast_to(b_s[l][:, :, None], (A_GROUPS, CHUNK, LANES)),
                 w_pa[l].astype(_BF16), w_pb[l].astype(_BF16), w_out[l].astype(_BF16))

        p = _proj_in(x, g_mix[l], main[0], main[1], gsgu, *rot, w_in_b, l, _MIX_SEGMENTS)
        if update_ctx:
            pc = _proj_in(xc, g_mix[l], cmod[0], cmod[1], gsgu, *no_rot, w_in_b, l, _MIX_SEGMENTS)
            s_ctx_f, s_ctx_b = _ctx_states(pc, _K_BLK, _VR_BLK, tabs["ctx_f"], tabs["ctx_b"])
        else:
            w_vk = jnp.concatenate([w_in_b[l:l + 1, :, KV_START + D_RQK:KV_END],
                                    w_in_b[l:l + 1, :, KV_START:KV_START + D_RQK]], axis=2)
            vk_c = _proj_in(xc, g_mix[l], cmod[0], cmod[1], gsgu, *no_rot, w_vk, 0, _VK_SEGMENTS)
            s_ctx_f, s_ctx_b = _ctx_states(vk_c, D_RV // D_RQK, 0, tabs["ctx_f"], tabs["ctx_b"])
        sb = _bwd_states(p, tabs["kdb"], tabs["cdec"], s_ctx_b)
        wr = jnp.zeros((d, ROUTE_COLS // 2), _F32).at[:, :MOE_GROUPS].set(w_group[l])
        wr = wr.at[:, MOE_GROUPS:MOE_GROUPS + N_EXPERTS].set(w_erouter[l])
        wr_hi = wr.astype(_BF16)
        wr = jnp.concatenate([wr_hi, (wr - wr_hi.astype(_F32)).astype(_BF16)], axis=1)
        br = jnp.zeros((1, ROUTE_COLS), _F32).at[0, :MOE_GROUPS].set(b_group[l])
        br = br.at[0, MOE_GROUPS:MOE_GROUPS + N_EXPERTS].set(b_erouter[l])
        route_w = (g_ffn[l], wr, br)
        x, h2, ids, wts, cnt = _token_mix(p, x, main[2], tabs, sb, s_ctx_f, *mix_w, route_w[0], main[3], main[4],
                                          *route_w[1:])
        routed_c = None
        if update_ctx:
            sbc = _bwd_states(pc, tabs["kdb"], tabs["cdec"], zero_state)
            xc, h2c, idsc, wtsc, cntc = _token_mix(pc, xc, cmod[2], tabs, sbc, zero_state, *mix_w, route_w[0],
                                                   cmod[3], cmod[4], *route_w[1:])
            routed_c = (h2c, idsc)
            cnt = jnp.concatenate([cnt, cntc], axis=0)

        plan, blk_e, blk_rows, n_blocks = _dispatch_plan(cnt[:, 0, :N_EXPERTS])
        off, cnt = plan[:2]
        xs, pos = _dispatch((h2, ids), routed_c, plan, n_blocks)
        ys = _experts(xs, blk_e, blk_rows, l, w1, w3, w2)
        final = l == DEPTH - 1
        x = _combine(x, ys, pos, wts, off, cnt, main[5], g_final, 0, final)
        if update_ctx:
            xc = _combine(xc, ys, pos, wtsc, off, cnt, cmod[5], g_final, t_main, False)
    return x
```

```python
import functools

import numpy as np
import jax
import jax.numpy as jnp
from jax import lax
from jax.experimental import pallas as pl
from jax.experimental.pallas import tpu as pltpu

D_MODEL = 1024
DEPTH = 2
GRID_W = 64
CHUNK = 128
A_GROUP_DIM = 128
A_GROUPS = D_MODEL // A_GROUP_DIM
D_A = A_GROUPS * A_GROUP_DIM
R_HEADS = 4
R_QK_DIM = D_MODEL // 8
R_V_DIM = D_MODEL // 4
D_RQK = R_HEADS * R_QK_DIM
D_RV = R_HEADS * R_V_DIM
D_IN = 2 * D_A + 2 * D_RQK + 2 * D_RV + 2 * D_MODEL
KV_START = 2 * D_A + D_RQK
KV_END = KV_START + D_RQK + D_RV
ROPE_BASE = 10000.0
MOE_GROUPS = 4
EXPERTS_PER_GROUP = 8
N_EXPERTS = MOE_GROUPS * EXPERTS_PER_GROUP
EXPERT_FF = D_MODEL // 2
EPS = 1e-6

LANES = 128
SUBLANES = 8
MOE_ROWS = 512
MOE_HALF = MOE_ROWS // 2
VMEM_LIMIT = 56 * 1024 * 1024
COND_ROWS = 16

_U_BLK, _VA_BLK = 0, 1
_Q_BLK, _K_BLK = 2 * D_A // D_RQK, 2 * D_A // D_RQK + 1
_VR_BLK = (2 * D_A + 2 * D_RQK) // D_RV
_GR_BLK, _GA_BLK, _GB_BLK = _VR_BLK + 1, _VR_BLK + 2, _VR_BLK + 3

_F32 = jnp.float32
_BF16 = jnp.bfloat16


def _cparams(*sem):
    return pltpu.CompilerParams(dimension_semantics=sem, vmem_limit_bytes=VMEM_LIMIT)


def _dot(a, b):
    return jnp.dot(a, b, preferred_element_type=_F32)


def _adaln_kernel(cond_ref, w_ref, b_ref, o_ref):
    cnd = cond_ref[...]
    s = cnd * jax.nn.sigmoid(cnd)
    o_ref[0] = jnp.dot(s, w_ref[0], preferred_element_type=_F32, precision=lax.Precision.HIGHEST) + b_ref[0]


def _adaln(cond, w_ada, b_ada):
    depth, d, d6 = w_ada.shape
    tn = 1024
    return pl.pallas_call(
        _adaln_kernel,
        grid=(depth, d6 // tn),
        in_specs=[
            pl.BlockSpec((COND_ROWS, d), lambda l, j: (0, 0)),
            pl.BlockSpec((1, d, tn), lambda l, j: (l, 0, j)),
            pl.BlockSpec((1, 1, tn), lambda l, j: (l, 0, j)),
        ],
        out_specs=pl.BlockSpec((1, COND_ROWS, tn), lambda l, j: (l, 0, j)),
        out_shape=jax.ShapeDtypeStruct((depth, COND_ROWS, d6), _F32),
        compiler_params=_cparams("parallel", "parallel"),
        name="adaln",
    )(cond, w_ada, b_ada.reshape(depth, 1, d6))


def _modnorm(x, g, shift, scale):
    y = x * lax.rsqrt(jnp.mean(x * x, axis=-1, keepdims=True) + EPS)
    return (y * g) * (1.0 + scale) + shift


def _layernorm(t):
    mu = jnp.mean(t, axis=-1, keepdims=True)
    tc = t - mu
    return tc * lax.rsqrt(jnp.mean(tc * tc, axis=-1, keepdims=True) + EPS)


def _rotate_heads(t, cos_i, sin_i, scale):
    even = lax.broadcasted_iota(jnp.int32, cos_i.shape, 1) % 2 == 0
    out = []
    for h in range(t.shape[1] // R_QK_DIM):
        th = t[:, h * R_QK_DIM:(h + 1) * R_QK_DIM]
        partner = jnp.where(even, pltpu.roll(th, R_QK_DIM - 1, 1), pltpu.roll(th, 1, 1))
        out.append((th * cos_i + partner * sin_i) * scale)
    return jnp.concatenate(out, axis=1)


def _proj_kernel(segments, x_ref, g_ref, sh_ref, sc_ref, gsgu_ref, cos_ref, sin_ref, w_ref, o_ref, h_ref):
    h_ref[...] = _modnorm(x_ref[0], g_ref[...], sh_ref[0], sc_ref[0]).astype(_BF16)
    for start, width, kind in segments:
        cols = slice(start, start + width)
        acc = _dot(h_ref[...], w_ref[0, :, cols])
        if kind == "gelu_ln":
            acc = _layernorm(jax.nn.gelu(acc)) * gsgu_ref[...]
        elif kind == "rot_q":
            acc = _rotate_heads(acc, cos_ref[...], sin_ref[...], 1.0)
        elif kind == "rot_k":
            acc = _rotate_heads(acc, cos_ref[...], sin_ref[...], R_QK_DIM ** -0.5)
        o_ref[0, :, cols] = acc.astype(_BF16)


_MIX_SEGMENTS = ((D_A, D_A, "gelu_ln"), (2 * D_A, D_RQK, "rot_q"), (2 * D_A + D_RQK, D_RQK, "rot_k"),
                 (0, D_A, "plain"), (KV_START + D_RQK, D_RV, "plain"), (KV_END, D_RV, "plain"),
                 (KV_END + D_RV, D_MODEL, "plain"), (KV_END + D_RV + D_MODEL, D_MODEL, "plain"))
_VK_SEGMENTS = ((D_RV, D_RQK, "rot_k"), (0, D_RV, "plain"))


def _proj_in(x, g, shift, scale, gsgu, cos_i, sin_i, w, layer, segments):
    b, n, d = x.shape
    dout = w.shape[2]
    tm = min(n, 512)
    return pl.pallas_call(
        functools.partial(_proj_kernel, segments),
        grid=(b, n // tm),
        in_specs=[
            pl.BlockSpec((1, tm, d), lambda i, r: (i, r, 0)),
            pl.BlockSpec((1, d), lambda i, r: (0, 0)),
            pl.BlockSpec((1, 1, d), lambda i, r: (i, 0, 0)),
            pl.BlockSpec((1, 1, d), lambda i, r: (i, 0, 0)),
            pl.BlockSpec((1, D_A), lambda i, r: (0, 0)),
            pl.BlockSpec((tm, LANES), lambda i, r: (r, 0)),
            pl.BlockSpec((tm, LANES), lambda i, r: (r, 0)),
            pl.BlockSpec((1, d, dout), lambda i, r: (layer, 0, 0), pipeline_mode=pl.Buffered(1)),
        ],
        out_specs=pl.BlockSpec((1, tm, dout), lambda i, r: (i, r, 0)),
        out_shape=jax.ShapeDtypeStruct((b, n, dout), _BF16),
        scratch_shapes=[pltpu.VMEM((tm, d), _BF16)],
        compiler_params=_cparams("parallel", "parallel"),
        name="proj_in",
    )(x, g.reshape(1, d), shift, scale, gsgu, cos_i, sin_i, w)


def _dot_t(a, b):
    return lax.dot_general(a, b, (((0,), (0,)), ((), ())), preferred_element_type=_F32)


def _ctx_state_kernel(k_ref, v_ref, wf_ref, wb_ref, sf_ref, sb_ref):
    for h in range(R_HEADS):
        kh = k_ref[0, :, h * R_QK_DIM:(h + 1) * R_QK_DIM].astype(_F32)
        vh = v_ref[0, :, h * R_V_DIM:(h + 1) * R_V_DIM]
        sf_ref[0, h] = _dot_t((kh * wf_ref[h]).astype(_BF16), vh)
        sb_ref[0, h] = _dot_t((kh * wb_ref[h]).astype(_BF16), vh)


def _ctx_states(pc, k_blk, v_blk, wf, wb):
    b, l, _ = pc.shape
    st = jax.ShapeDtypeStruct((b, R_HEADS, R_QK_DIM, R_V_DIM), _F32)
    st_spec = pl.BlockSpec((1, R_HEADS, R_QK_DIM, R_V_DIM), lambda i: (i, 0, 0, 0))
    return pl.pallas_call(
        _ctx_state_kernel,
        grid=(b,),
        in_specs=[
            pl.BlockSpec((1, l, D_RQK), lambda i: (i, 0, k_blk)),
            pl.BlockSpec((1, l, D_RV), lambda i: (i, 0, v_blk)),
            pl.BlockSpec((R_HEADS, l, LANES), lambda i: (0, 0, 0)),
            pl.BlockSpec((R_HEADS, l, LANES), lambda i: (0, 0, 0)),
        ],
        out_specs=[st_spec, st_spec],
        out_shape=[st, st],
        compiler_params=_cparams("parallel"),
        name="ctx_states",
    )(pc, pc, wf, wb)


def _bwd_state_kernel(n_chunks, cdec_ref, k_ref, v_ref, kdb_ref, s0_ref, o_ref, s_ref):
    @pl.when(pl.program_id(1) == 0)
    def _():
        s_ref[...] = s0_ref[0]

    for c in reversed(range(n_chunks)):
        rows = pl.ds(c * CHUNK, CHUNK)
        for h in range(R_HEADS):
            s = s_ref[h]
            o_ref[0, c, h] = s.astype(_BF16)
            kh = k_ref[0, rows, h * R_QK_DIM:(h + 1) * R_QK_DIM].astype(_F32)
            vh = v_ref[0, rows, h * R_V_DIM:(h + 1) * R_V_DIM]
            s_ref[h] = s * cdec_ref[1, h] + _dot_t((kh * kdb_ref[h]).astype(_BF16), vh)


def _bwd_states(p, kdb, cdec, s0_b):
    b, n, _ = p.shape
    rows = min(n, 1024)
    ncb, nblk = rows // CHUNK, n // rows
    rev = lambda r: nblk - 1 - r
    return pl.pallas_call(
        functools.partial(_bwd_state_kernel, ncb),
        grid=(b, nblk),
        in_specs=[
            pl.BlockSpec(memory_space=pltpu.SMEM),
            pl.BlockSpec((1, rows, D_RQK), lambda i, r: (i, rev(r), _K_BLK)),
            pl.BlockSpec((1, rows, D_RV), lambda i, r: (i, rev(r), _VR_BLK)),
            pl.BlockSpec((R_HEADS, CHUNK, LANES), lambda i, r: (0, 0, 0)),
            pl.BlockSpec((1, R_HEADS, R_QK_DIM, R_V_DIM), lambda i, r: (i, 0, 0, 0)),
        ],
        out_specs=pl.BlockSpec((1, ncb, R_HEADS, R_QK_DIM, R_V_DIM), lambda i, r: (i, rev(r), 0, 0, 0)),
        out_shape=jax.ShapeDtypeStruct((b, n // CHUNK, R_HEADS, R_QK_DIM, R_V_DIM), _BF16),
        scratch_shapes=[pltpu.VMEM((R_HEADS, R_QK_DIM, R_V_DIM), _F32)],
        compiler_params=_cparams("parallel", "arbitrary"),
        name="bwd_states",
    )(cdec, p, p, kdb, s0_b)


def _mix_kernel(n_chunks, cdec_ref, u_ref, va_ref, q_ref, k_ref, vr_ref, gr_ref, ga_ref, gb_ref, x_ref, gt_ref,
                dmat_ref, qdf_ref, qdb_ref, kdf_ref, sb_ref, s0_ref,
                ws_ref, bs_ref, wpa_ref, wpb_ref, wout_ref, g2_ref, sh2_ref, sc2_ref, wr_ref, br_ref,
                o_ref, h_ref, id_ref, wt_ref, cnt_ref, sf_ref, sa_ref, lo_ref):
    @pl.when(pl.program_id(1) == 0)
    def _():
        sf_ref[...] = s0_ref[0]

    for c in range(n_chunks):
        rows = pl.ds(c * CHUNK, CHUNK)
        vn = va_ref[0, rows, :]
        for g in range(A_GROUPS):
            cols = slice(g * A_GROUP_DIM, (g + 1) * A_GROUP_DIM)
            sa_ref[rows, cols] = _dot(ws_ref[g], vn[:, cols]) + bs_ref[g]
        for h in range(R_HEADS):
            qk = slice(h * R_QK_DIM, (h + 1) * R_QK_DIM)
            vv = slice(h * R_V_DIM, (h + 1) * R_V_DIM)
            qb, kb = q_ref[0, rows, qk], k_ref[0, rows, qk]
            qh, kh = qb.astype(_F32), kb.astype(_F32)
            vh = vr_ref[0, rows, vv]
            scores = lax.dot_general(qb, kb, (((1,), (1,)), ((), ())), preferred_element_type=_F32) * dmat_ref[h]
            o = _dot(scores.astype(_BF16), vh)
            s_f = sf_ref[h]
            o = o + _dot((qh * qdf_ref[h]).astype(_BF16), s_f.astype(_BF16))
            o = o + _dot((qh * qdb_ref[h]).astype(_BF16), sb_ref[0, c, h])
            lo_ref[rows, vv] = _layernorm(o)
            sf_ref[h] = s_f * cdec_ref[0, h] + _dot_t((kh * kdf_ref[h]).astype(_BF16), vh)

    gr = gr_ref[0].astype(_F32)
    a = _dot((jax.nn.gelu(u_ref[0].astype(_F32)) * sa_ref[...]).astype(_BF16), wpa_ref[...])
    b = _dot((gr * jax.nn.sigmoid(gr) * lo_ref[...]).astype(_BF16), wpb_ref[...])
    merged = jax.nn.sigmoid(ga_ref[0].astype(_F32)) * a + jax.nn.sigmoid(gb_ref[0].astype(_F32)) * b
    x_new = x_ref[0] + gt_ref[0] * _dot(merged.astype(_BF16), wout_ref[...])
    o_ref[0] = x_new
    _route(x_new, g2_ref, sh2_ref, sc2_ref, wr_ref, br_ref, h_ref, id_ref, wt_ref, cnt_ref)


def _token_mix(p, x, gt, tabs, sb, s0_f, ws, bsb, wpa, wpb, wout, g2, shift2, scale2, wr, br):
    b, n, d = x.shape
    t = b * n
    rows = min(n, 512)
    ncb = rows // CHUNK
    per_b = n // rows
    tiles = rows // MOE_TILE
    flat = lambda width: pl.BlockSpec((rows, width), lambda i, r: (i * per_b + r, 0))
    pspec = lambda width, blk: pl.BlockSpec((1, rows, width), lambda i, r: (i, r, blk))
    const = lambda shape: pl.BlockSpec(shape, lambda i, r: (0,) * len(shape))
    tab = const((R_HEADS, CHUNK, LANES))
    return pl.pallas_call(
        functools.partial(_mix_kernel, ncb),
        grid=(b, n // rows),
        in_specs=[
            pl.BlockSpec(memory_space=pltpu.SMEM),
            pspec(D_A, _U_BLK), pspec(D_A, _VA_BLK), pspec(D_RQK, _Q_BLK), pspec(D_RQK, _K_BLK),
            pspec(D_RV, _VR_BLK), pspec(D_RV, _GR_BLK), pspec(d, _GA_BLK), pspec(d, _GB_BLK),
            pl.BlockSpec((1, rows, d), lambda i, r: (i, r, 0)),
            pl.BlockSpec((1, 1, d), lambda i, r: (i, 0, 0)),
            tab, tab, tab, tab,
            pl.BlockSpec((1, ncb, R_HEADS, R_QK_DIM, R_V_DIM), lambda i, r: (i, r, 0, 0, 0)),
            pl.BlockSpec((1, R_HEADS, R_QK_DIM, R_V_DIM), lambda i, r: (i, 0, 0, 0)),
            const((A_GROUPS, CHUNK, CHUNK)), const((A_GROUPS, CHUNK, LANES)),
            const((D_A, d)), const((D_RV, d)), const((d, d)),
            const((1, d)), pl.BlockSpec((1, 1, d), lambda i, r: (i, 0, 0)), pl.BlockSpec((1, 1, d), lambda i, r: (i, 0, 0)),
            const((d, ROUTE_COLS)), const((1, ROUTE_COLS)),
        ],
        out_specs=[pl.BlockSpec((1, rows, d), lambda i, r: (i, r, 0)), flat(d), flat(ROUTE_COLS), flat(ROUTE_COLS),
                   pl.BlockSpec((tiles, SUBLANES, ROUTE_COLS), lambda i, r: (i * per_b + r, 0, 0))],
        out_shape=[jax.ShapeDtypeStruct((b, n, d), _F32), jax.ShapeDtypeStruct((t, d), _BF16),
                   jax.ShapeDtypeStruct((t, ROUTE_COLS), jnp.int32), jax.ShapeDtypeStruct((t, ROUTE_COLS), _F32),
                   jax.ShapeDtypeStruct((t // MOE_TILE, SUBLANES, ROUTE_COLS), jnp.int32)],
        scratch_shapes=[
            pltpu.VMEM((R_HEADS, R_QK_DIM, R_V_DIM), _F32),
            pltpu.VMEM((rows, D_A), _F32),
            pltpu.VMEM((rows, D_RV), _F32),
        ],
        compiler_params=_cparams("parallel", "arbitrary"),
        name="token_mix",
    )(tabs["cdec"], p, p, p, p, p, p, p, p, x, gt, tabs["dmat"], tabs["qdf"], tabs["qdb"], tabs["kdf"],
      sb, s0_f, ws, bsb, wpa, wpb, wout, g2.reshape(1, d), shift2, scale2, wr, br)


ROUTE_COLS = LANES
MOE_TILE = 256
MOE_SLOTS = 2 * MOE_TILE
SLOT_SUBROWS = D_MODEL // LANES


def _route(x, g_ref, sh_ref, sc_ref, wr_ref, br_ref, h_ref, id_ref, wt_ref, cnt_ref):
    h = _modnorm(x, g_ref[...], sh_ref[0], sc_ref[0])
    h_hi = h.astype(_BF16)
    h_ref[...] = h_hi
    h_lo = (h - h_hi.astype(_F32)).astype(_BF16)
    r_hi = _dot(h_hi, wr_ref[...])
    logits = r_hi + pltpu.roll(r_hi, ROUTE_COLS // 2, 1) + _dot(h_lo, wr_ref[...]) + br_ref[...]
    lane = lax.broadcasted_iota(jnp.int32, logits.shape, 1)
    neg = -jnp.inf
    gl = jnp.where(lane < MOE_GROUPS, logits, neg)
    g_top = jnp.max(gl, axis=-1, keepdims=True)
    g_sel = jnp.min(jnp.where(gl == g_top, lane, ROUTE_COLS), axis=-1, keepdims=True)
    lse = g_top + jnp.log(jnp.sum(jnp.exp(gl - g_top), axis=-1, keepdims=True))
    p_group = jnp.exp(g_top - lse)
    lo = MOE_GROUPS + g_sel * EXPERTS_PER_GROUP
    el = jnp.where((lane >= lo) & (lane < lo + EXPERTS_PER_GROUP), logits, neg)
    e1 = jnp.max(el, axis=-1, keepdims=True)
    i1 = jnp.min(jnp.where(el == e1, lane, ROUTE_COLS), axis=-1, keepdims=True)
    el2 = jnp.where(lane == i1, neg, el)
    e2 = jnp.max(el2, axis=-1, keepdims=True)
    i2 = jnp.min(jnp.where(el2 == e2, lane, ROUTE_COLS), axis=-1, keepdims=True)
    x2 = jnp.exp(e2 - e1)
    den = 1.0 + x2
    id1, id2 = i1 - MOE_GROUPS, i2 - MOE_GROUPS
    id_ref[...] = jnp.where(lane == 0, id1, jnp.where(lane == 1, id2, 0))
    wt_ref[...] = jnp.where(lane == 0, p_group * (1.0 / den), jnp.where(lane == 1, p_group * (x2 / den), 0.0))
    hits = jnp.where((lane == id1) | (lane == id2), 1, 0)
    for t in range(cnt_ref.shape[0]):
        tile_hits = jnp.sum(hits[t * MOE_TILE:(t + 1) * MOE_TILE], axis=0, keepdims=True)
        cnt_ref[t] = jnp.broadcast_to(tile_hits, (SUBLANES, ROUTE_COLS))


def _dispatch_plan(cnt):
    n_tiles = cnt.shape[0]
    a = n_tiles * MOE_SLOTS
    tot = jnp.sum(cnt, axis=0)
    padded = (tot + MOE_ROWS - 1) // MOE_ROWS * MOE_ROWS
    pend = jnp.cumsum(padded)
    pstart = pend - padded
    off = pstart[None, :] + jnp.cumsum(cnt, axis=0) - cnt
    n_blocks = (a + N_EXPERTS * (MOE_ROWS - 1) + MOE_ROWS - 1) // MOE_ROWS
    blk_start = jnp.arange(n_blocks, dtype=jnp.int32) * MOE_ROWS
    blk_e = jnp.minimum(jnp.sum((pend[None, :] <= blk_start[:, None]).astype(jnp.int32), axis=1), N_EXPERTS - 1)
    owner = (blk_e[:, None] == jnp.arange(N_EXPERTS, dtype=jnp.int32)[None, :]).astype(jnp.int32)
    used_end = jnp.sum(owner * (pstart + tot)[None, :], axis=1)
    blk_rows = jnp.clip(used_end - blk_start, 0, MOE_ROWS)
    pad_start = jnp.concatenate([pstart + tot, blk_start])
    pad_len = jnp.concatenate([padded - tot, jnp.where(blk_start >= pend[-1], MOE_ROWS, 0)])
    i32 = lambda v: v.astype(jnp.int32)
    return (i32(off).reshape(-1), i32(cnt).reshape(-1), i32(pad_start), i32(pad_len)), i32(blk_e), i32(blk_rows), n_blocks


def _copy_runs(cnt_ref, off_ref, tile, make_copy):
    def body(e, src):
        n = cnt_ref[tile * N_EXPERTS + e]

        @pl.when(n > 0)
        def _():
            make_copy(src, off_ref[tile * N_EXPERTS + e], n).start()
        return src + n
    lax.fori_loop(0, N_EXPERTS, body, 0)


def _slot_rows(first, n):
    s = SLOT_SUBROWS
    return pl.ds(pl.multiple_of(first * s, s), pl.multiple_of(n * s, s))


def _load_slot_major(ref, n_slots):
    return jnp.concatenate([ref[pl.ds(s, n_slots, stride=SLOT_SUBROWS), :] for s in range(SLOT_SUBROWS)], axis=1)


def _store_slot_major(ref, val):
    for s in range(SLOT_SUBROWS):
        ref[pl.ds(s, val.shape[0], stride=SLOT_SUBROWS), :] = val[:, s * LANES:(s + 1) * LANES]


def _wait_slot(stage, sem, s):
    pltpu.make_async_copy(stage.at[s], stage.at[s], sem.at[s]).wait()


def _dispatch_kernel(n_main, off_ref, cnt_ref, pad_start_ref, pad_len_ref, h_ref, id_ref, hc_ref, idc_ref,
                     xs_hbm, pos_ref, stage, sem, zeros, zsem):
    i = pl.program_id(0)
    slot = i % 2
    from_main = i < n_main
    ids = id_ref[...] if idc_ref is None else jnp.where(from_main, id_ref[...], idc_ref[...])
    h = h_ref[...] if hc_ref is None else jnp.where(from_main, h_ref[...], hc_ref[...])

    def pad_copies(act):
        def body(r, carry):
            n = pad_len_ref[r]

            @pl.when(n > 0)
            def _():
                act(pltpu.make_async_copy(zeros.at[_slot_rows(0, n)], xs_hbm.at[_slot_rows(pad_start_ref[r], n)], zsem))
            return carry
        lax.fori_loop(0, pad_len_ref.shape[0], body, 0)

    @pl.when(i == 0)
    def _():
        zeros[...] = jnp.zeros(zeros.shape, zeros.dtype)
        pad_copies(lambda copy: copy.start())

    lane = lax.broadcasted_iota(jnp.int32, ids.shape, 1)
    e0 = jnp.where(lane == ids[:, 0:1], 1.0, 0.0)
    e1 = jnp.where(lane == ids[:, 1:2], 1.0, 0.0)
    tt = lax.broadcasted_iota(jnp.int32, (MOE_TILE, MOE_TILE), 0)
    earlier = jnp.where(lax.broadcasted_iota(jnp.int32, (MOE_TILE, MOE_TILE), 1) < tt, 1.0, 0.0).astype(_BF16)
    c0 = _dot(earlier, e0.astype(_BF16))
    c1 = _dot(earlier, e1.astype(_BF16))
    tot0 = jnp.sum(e0, axis=0, keepdims=True)
    ee = lax.broadcasted_iota(jnp.int32, (ROUTE_COLS, ROUTE_COLS), 0)
    lower = jnp.where(ee < lax.broadcasted_iota(jnp.int32, (ROUTE_COLS, ROUTE_COLS), 1), 1.0, 0.0).astype(_BF16)
    base = jnp.sum(_dot((e0 + e1).astype(_BF16), lower), axis=0, keepdims=True)
    pos0 = jnp.sum(e0 * (base + c0), axis=1, keepdims=True)
    pos1 = jnp.sum(e1 * (base + tot0 + c1), axis=1, keepdims=True)
    pos = jnp.where(lane == 0, pos0, jnp.where(lane == 1, pos1, 0.0))
    pos_ref[...] = pos
    pos_t = pos.T
    slot_id = lax.broadcasted_iota(jnp.int32, (MOE_SLOTS, MOE_TILE), 0).astype(_F32)
    pick = jnp.where((slot_id == pos_t[0:1, :]) | (slot_id == pos_t[1:2, :]), 1.0, 0.0).astype(_BF16)
    rows = _dot(pick, h)

    @pl.when(i >= 2)
    def _():
        _wait_slot(stage, sem, slot)

    _store_slot_major(stage.at[slot], rows)
    _copy_runs(cnt_ref, off_ref, i, lambda src, dst, n: pltpu.make_async_copy(
        stage.at[slot, _slot_rows(src, n)], xs_hbm.at[_slot_rows(dst, n)], sem.at[slot]))

    @pl.when(i == pl.num_programs(0) - 1)
    def _():
        _wait_slot(stage, sem, slot)

        @pl.when(i >= 1)
        def _():
            _wait_slot(stage, sem, 1 - slot)

        pad_copies(lambda copy: copy.wait())


def _dispatch(routed, routed_c, plan, n_blocks):
    d = routed[0].shape[1]
    n_main = routed[0].shape[0] // MOE_TILE
    n_ctx = 0 if routed_c is None else routed_c[0].shape[0] // MOE_TILE
    n_tiles = n_main + n_ctx
    t = n_tiles * MOE_TILE
    pair = lambda f: [pl.BlockSpec((MOE_TILE, d), lambda i, *_: (f(i), 0)),
                      pl.BlockSpec((MOE_TILE, ROUTE_COLS), lambda i, *_: (f(i), 0))]
    if routed_c is None:
        body = lambda *refs: _dispatch_kernel(n_main, *refs[:6], None, None, *refs[6:])
    else:
        body = functools.partial(_dispatch_kernel, n_main)
    grid_spec = pltpu.PrefetchScalarGridSpec(
        num_scalar_prefetch=4,
        grid=(n_tiles,),
        in_specs=pair(lambda i: jnp.minimum(i, n_main - 1)) + ([] if routed_c is None else
                                                              pair(lambda i: jnp.maximum(i - n_main, 0))),
        out_specs=[pl.BlockSpec(memory_space=pl.ANY), pl.BlockSpec((MOE_TILE, ROUTE_COLS), lambda i, *_: (i, 0))],
        scratch_shapes=[pltpu.VMEM((2, MOE_SLOTS * SLOT_SUBROWS, LANES), _F32), pltpu.SemaphoreType.DMA((2,)),
                        pltpu.VMEM((MOE_ROWS * SLOT_SUBROWS, LANES), _F32), pltpu.SemaphoreType.DMA(())],
    )
    return pl.pallas_call(
        body,
        grid_spec=grid_spec,
        out_shape=[jax.ShapeDtypeStruct((n_blocks * MOE_ROWS * SLOT_SUBROWS, LANES), _F32),
                   jax.ShapeDtypeStruct((t, ROUTE_COLS), _F32)],
        compiler_params=_cparams("arbitrary"),
        name="moe_dispatch",
    )(*plan, *routed, *(routed_c or ()))


def _expert_kernel(blk_e_ref, rows_ref, x_ref, w1_ref, w3_ref, w2_ref, o_ref, w1b, w3b, w2b):
    i = pl.program_id(0)
    rows = rows_ref[i]

    @pl.when((i == 0) | (blk_e_ref[i] != blk_e_ref[jnp.maximum(i - 1, 0)]))
    def _():
        w1b[...] = w1_ref[0, 0].astype(_BF16)
        w3b[...] = w3_ref[0, 0].astype(_BF16)
        w2b[...] = w2_ref[0, 0].astype(_BF16)

    def ffn(first, n_slots):
        span = pl.ds(first * SLOT_SUBROWS, n_slots * SLOT_SUBROWS)
        xb = _load_slot_major(x_ref.at[span], n_slots).astype(_BF16)
        h1 = _dot(xb, w1b[...])
        h3 = _dot(xb, w3b[...])
        act = (h1 * jax.nn.sigmoid(h1) * h3).astype(_BF16)
        _store_slot_major(o_ref.at[span], _dot(act, w2b[...]))

    def clear(first, n_slots):
        o_ref[pl.ds(first * SLOT_SUBROWS, n_slots * SLOT_SUBROWS), :] = jnp.zeros((n_slots * SLOT_SUBROWS, LANES), o_ref.dtype)

    @pl.when(rows > MOE_HALF)
    def _():
        ffn(0, MOE_ROWS)

    @pl.when((rows > 0) & (rows <= MOE_HALF))
    def _():
        ffn(0, MOE_HALF)
        clear(MOE_HALF, MOE_HALF)

    @pl.when(rows == 0)
    def _():
        clear(0, MOE_ROWS)


def _experts(xs, blk_e, blk_rows, layer, w1, w3, w2):
    d = w1.shape[2]
    blk = MOE_ROWS * SLOT_SUBROWS
    n_blocks = xs.shape[0] // blk
    grid_spec = pltpu.PrefetchScalarGridSpec(
        num_scalar_prefetch=2,
        grid=(n_blocks,),
        in_specs=[
            pl.BlockSpec((blk, LANES), lambda i, be, br: (i, 0)),
            pl.BlockSpec((1, 1, d, EXPERT_FF), lambda i, be, br: (layer, be[i], 0, 0)),
            pl.BlockSpec((1, 1, d, EXPERT_FF), lambda i, be, br: (layer, be[i], 0, 0)),
            pl.BlockSpec((1, 1, EXPERT_FF, d), lambda i, be, br: (layer, be[i], 0, 0)),
        ],
        out_specs=pl.BlockSpec((blk, LANES), lambda i, be, br: (i, 0)),
        scratch_shapes=[pltpu.VMEM((d, EXPERT_FF), _BF16), pltpu.VMEM((d, EXPERT_FF), _BF16),
                        pltpu.VMEM((EXPERT_FF, d), _BF16)],
    )
    return pl.pallas_call(
        _expert_kernel,
        grid_spec=grid_spec,
        out_shape=jax.ShapeDtypeStruct(xs.shape, xs.dtype),
        compiler_params=_cparams("arbitrary"),
        name="experts",
    )(blk_e, blk_rows, xs, w1, w3, w2)


def _combine_kernel(final, tile0, off_ref, cnt_ref, x_ref, pos_ref, wt_ref, gt_ref, g_ref, ys_hbm, o_ref, stage, sem):
    i = pl.program_id(0)
    slot = i % 2

    def fetch(tile, s):
        _copy_runs(cnt_ref, off_ref, tile, lambda src, dst, n: pltpu.make_async_copy(
            ys_hbm.at[_slot_rows(dst, n)], stage.at[s, _slot_rows(src, n)], sem.at[s]))

    @pl.when(i == 0)
    def _():
        fetch(tile0, 0)

    @pl.when(i + 1 < pl.num_programs(0))
    def _():
        fetch(tile0 + i + 1, 1 - slot)

    _wait_slot(stage, sem, slot)
    y = _load_slot_major(stage.at[slot], MOE_SLOTS).astype(_BF16)
    slot_id = lax.broadcasted_iota(jnp.int32, (MOE_TILE, MOE_SLOTS), 1).astype(_F32)
    mix = (jnp.where(slot_id == pos_ref[:, 0:1], wt_ref[:, 0:1], 0.0)
           + jnp.where(slot_id == pos_ref[:, 1:2], wt_ref[:, 1:2], 0.0)).astype(_BF16)
    x = x_ref[...] + gt_ref[0] * _dot(mix, y)
    if final:
        x = (x * lax.rsqrt(jnp.mean(x * x, axis=-1, keepdims=True) + EPS)) * g_ref[...]
    o_ref[...] = x


def _combine(x, ys, pos, wts, off, cnt, gt, g_final, tok0, final):
    b, n, d = x.shape
    t = b * n
    per_b = n // MOE_TILE
    tile0 = tok0 // MOE_TILE
    row = pl.BlockSpec((MOE_TILE, d), lambda i, o, c: (i, 0))
    lanes = lambda first: pl.BlockSpec((MOE_TILE, ROUTE_COLS), lambda i, o, c: (first + i, 0))
    grid_spec = pltpu.PrefetchScalarGridSpec(
        num_scalar_prefetch=2,
        grid=(t // MOE_TILE,),
        in_specs=[row, lanes(tile0), lanes(0), pl.BlockSpec((1, 1, d), lambda i, o, c: (i // per_b, 0, 0)),
                  pl.BlockSpec((1, d), lambda i, o, c: (0, 0)), pl.BlockSpec(memory_space=pl.ANY)],
        out_specs=row,
        scratch_shapes=[pltpu.VMEM((2, MOE_SLOTS * SLOT_SUBROWS, LANES), _F32), pltpu.SemaphoreType.DMA((2,))],
    )
    out = pl.pallas_call(
        functools.partial(_combine_kernel, final, tile0),
        grid_spec=grid_spec,
        out_shape=jax.ShapeDtypeStruct((t, d), _F32),
        compiler_params=_cparams("arbitrary"),
        name="moe_combine",
    )(off, cnt, x.reshape(t, d), pos, wts, gt, g_final.reshape(1, d), ys)
    return out.reshape(b, n, d)


def _rotary_tables(n):
    n_freq = R_QK_DIM // 4
    inv = ROPE_BASE ** (-jnp.arange(n_freq, dtype=_F32) / n_freq)
    rows = n // GRID_W
    row = jnp.repeat(jnp.arange(rows, dtype=_F32), GRID_W)
    col = jnp.tile(jnp.arange(GRID_W, dtype=_F32), rows)
    ang = jnp.concatenate([row[:, None] * inv, col[:, None] * inv], axis=-1)
    cos, sin = jnp.cos(ang), jnp.sin(ang)
    return jnp.repeat(cos, 2, axis=-1), jnp.stack([-sin, sin], axis=-1).reshape(n, R_QK_DIM)


def _decay_tables(decay_logit, l_ctx):
    log_g = jax.nn.log_sigmoid(decay_logit.astype(_F32))
    lgf, lgb = log_g[0][:, None], log_g[1][:, None]
    j = jnp.arange(CHUNK, dtype=_F32)
    diff = j[:, None] - j[None, :]
    dmat = jnp.where(diff[None] >= 0, jnp.exp(lgf[:, :, None] * jnp.maximum(diff, 0.0)[None]),
                     jnp.exp(lgb[:, :, None] * jnp.maximum(-diff, 0.0)[None]))
    lanes = lambda t: jnp.broadcast_to(t[:, :, None], t.shape + (LANES,))
    m = jnp.arange(l_ctx, dtype=_F32)
    return {
        "dmat": dmat,
        "qdf": lanes(jnp.exp(lgf * (j + 1.0))), "qdb": lanes(jnp.exp(lgb * (CHUNK - j))),
        "kdf": lanes(jnp.exp(lgf * (CHUNK - 1.0 - j))), "kdb": lanes(jnp.exp(lgb * j)),
        "cdec": jnp.exp(log_g * CHUNK),
        "ctx_f": lanes(jnp.exp(lgf * ((l_ctx - 1.0) - m))), "ctx_b": lanes(jnp.exp(lgb * m)),
    }


def kernel(x, c, ctx, c_ctx, w_ada, b_ada, g_mix, g_ffn, w_in, w_s, b_s, g_sgu, decay_logit, w_pa, w_pb, w_out,
           w_group, b_group, w_erouter, b_erouter, w1, w3, w2, g_final):
    bsz, n, d = x.shape
    l_ctx = ctx.shape[1]
    t_main = bsz * n

    cond = jnp.zeros((COND_ROWS, d), _F32).at[:bsz].set(c).at[bsz].set(c_ctx)
    mods = _adaln(cond, w_ada, b_ada)
    rot = _rotary_tables(n)
    no_rot = (jnp.ones((l_ctx, LANES), _F32), jnp.zeros((l_ctx, LANES), _F32))
    zero_state = jnp.zeros((bsz, R_HEADS, R_QK_DIM, R_V_DIM), _F32)

    w_in_b = w_in.astype(_BF16)
    xc = ctx
    for l in range(DEPTH):
        update_ctx = l < DEPTH - 1
        m6 = mods[l].reshape(COND_ROWS, 6, d)
        main = [m6[:bsz, i][:, None, :] for i in range(6)]
        cmod = [jnp.broadcast_to(m6[bsz, i][None, None, :], (bsz, 1, d)) for i in range(6)]
        tabs = _decay_tables(decay_logit[l], l_ctx)
        gsgu = g_sgu[l].reshape(1, D_A)
        mix_w = (w_s[l].astype(_BF16), jnp.broadcast_to(b_s[l][:, :, None], (A_GROUPS, CHUNK, LANES)),
                 w_pa[l].astype(_BF16), w_pb[l].astype(_BF16), w_out[l].astype(_BF16))

        p = _proj_in(x, g_mix[l], main[0], main[1], gsgu, *rot, w_in_b, l, _MIX_SEGMENTS)
        if update_ctx:
            pc = _proj_in(xc, g_mix[l], cmod[0], cmod[1], gsgu, *no_rot, w_in_b, l, _MIX_SEGMENTS)
            s_ctx_f, s_ctx_b = _ctx_states(pc, _K_BLK, _VR_BLK, tabs["ctx_f"], tabs["ctx_b"])
        else:
            w_vk = jnp.concatenate([w_in_b[l:l + 1, :, KV_START + D_RQK:KV_END],
                                    w_in_b[l:l + 1, :, KV_START:KV_START + D_RQK]], axis=2)
            vk_c = _proj_in(xc, g_mix[l], cmod[0], cmod[1], gsgu, *no_rot, w_vk, 0, _VK_SEGMENTS)
            s_ctx_f, s_ctx_b = _ctx_states(vk_c, D_RV // D_RQK, 0, tabs["ctx_f"], tabs["ctx_b"])
        sb = _bwd_states(p, tabs["kdb"], tabs["cdec"], s_ctx_b)
        wr = jnp.zeros((d, ROUTE_COLS // 2), _F32).at[:, :MOE_GROUPS].set(w_group[l])
        wr = wr.at[:, MOE_GROUPS:MOE_GROUPS + N_EXPERTS].set(w_erouter[l])
        wr_hi = wr.astype(_BF16)
        wr = jnp.concatenate([wr_hi, (wr - wr_hi.astype(_F32)).astype(_BF16)], axis=1)
        br = jnp.zeros((1, ROUTE_COLS), _F32).at[0, :MOE_GROUPS].set(b_group[l])
        br = br.at[0, MOE_GROUPS:MOE_GROUPS + N_EXPERTS].set(b_erouter[l])
        route_w = (g_ffn[l], wr, br)
        x, h2, ids, wts, cnt = _token_mix(p, x, main[2], tabs, sb, s_ctx_f, *mix_w, route_w[0], main[3], main[4],
                                          *route_w[1:])
        routed_c = None
        if update_ctx:
            sbc = _bwd_states(pc, tabs["kdb"], tabs["cdec"], zero_state)
            xc, h2c, idsc, wtsc, cntc = _token_mix(pc, xc, cmod[2], tabs, sbc, zero_state, *mix_w, route_w[0],
                                                   cmod[3], cmod[4], *route_w[1:])
            routed_c = (h2c, idsc)
            cnt = jnp.concatenate([cnt, cntc], axis=0)

        plan, blk_e, blk_rows, n_blocks = _dispatch_plan(cnt[:, 0, :N_EXPERTS])
        off, cnt = plan[:2]
        xs, pos = _dispatch((h2, ids), routed_c, plan, n_blocks)
        ys = _experts(xs, blk_e, blk_rows, l, w1, w3, w2)
        final = l == DEPTH - 1
        x = _combine(x, ys, pos, wts, off, cnt, main[5], g_final, 0, final)
        if update_ctx:
            xc = _combine(xc, ys, pos, wtsc, off, cnt, cmod[5], g_final, t_main, False)
    return x
```

```python
import functools

import numpy as np
import jax
import jax.numpy as jnp
from jax import lax
from jax.experimental import pallas as pl
from jax.experimental.pallas import tpu as pltpu

D_MODEL = 1024
DEPTH = 2
GRID_W = 64
CHUNK = 128
A_GROUP_DIM = 128
A_GROUPS = D_MODEL // A_GROUP_DIM
D_A = A_GROUPS * A_GROUP_DIM
R_HEADS = 4
R_QK_DIM = D_MODEL // 8
R_V_DIM = D_MODEL // 4
D_RQK = R_HEADS * R_QK_DIM
D_RV = R_HEADS * R_V_DIM
D_IN = 2 * D_A + 2 * D_RQK + 2 * D_RV + 2 * D_MODEL
KV_START = 2 * D_A + D_RQK
KV_END = KV_START + D_RQK + D_RV
ROPE_BASE = 10000.0
MOE_GROUPS = 4
EXPERTS_PER_GROUP = 8
N_EXPERTS = MOE_GROUPS * EXPERTS_PER_GROUP
EXPERT_FF = D_MODEL // 2
EPS = 1e-6

LANES = 128
SUBLANES = 8
MOE_ROWS = 512
MOE_HALF = MOE_ROWS // 2
VMEM_LIMIT = 60 * 1024 * 1024
COND_ROWS = 16

_U_BLK, _VA_BLK = 0, 1
_Q_BLK, _K_BLK = 2 * D_A // D_RQK, 2 * D_A // D_RQK + 1
_VR_BLK = (2 * D_A + 2 * D_RQK) // D_RV
_GR_BLK, _GA_BLK, _GB_BLK = _VR_BLK + 1, _VR_BLK + 2, _VR_BLK + 3

_F32 = jnp.float32
_BF16 = jnp.bfloat16


def _cparams(*sem):
    return pltpu.CompilerParams(dimension_semantics=sem, vmem_limit_bytes=VMEM_LIMIT)


def _dot(a, b):
    return jnp.dot(a, b, preferred_element_type=_F32)


def _adaln_kernel(cond_ref, w_ref, b_ref, o_ref):
    cnd = cond_ref[...]
    s = cnd * jax.nn.sigmoid(cnd)
    o_ref[0] = jnp.dot(s, w_ref[0], preferred_element_type=_F32, precision=lax.Precision.HIGHEST) + b_ref[0]


def _adaln(cond, w_ada, b_ada):
    depth, d, d6 = w_ada.shape
    tn = 1024
    return pl.pallas_call(
        _adaln_kernel,
        grid=(depth, d6 // tn),
        in_specs=[
            pl.BlockSpec((COND_ROWS, d), lambda l, j: (0, 0)),
            pl.BlockSpec((1, d, tn), lambda l, j: (l, 0, j)),
            pl.BlockSpec((1, 1, tn), lambda l, j: (l, 0, j)),
        ],
        out_specs=pl.BlockSpec((1, COND_ROWS, tn), lambda l, j: (l, 0, j)),
        out_shape=jax.ShapeDtypeStruct((depth, COND_ROWS, d6), _F32),
        compiler_params=_cparams("parallel", "parallel"),
        name="adaln",
    )(cond, w_ada, b_ada.reshape(depth, 1, d6))


def _modnorm(x, g, shift, scale):
    y = x * lax.rsqrt(jnp.mean(x * x, axis=-1, keepdims=True) + EPS)
    return (y * g) * (1.0 + scale) + shift


def _layernorm(t):
    mu = jnp.mean(t, axis=-1, keepdims=True)
    tc = t - mu
    return tc * lax.rsqrt(jnp.mean(tc * tc, axis=-1, keepdims=True) + EPS)


def _rotate_heads(t, cos_i, sin_i, scale):
    even = lax.broadcasted_iota(jnp.int32, cos_i.shape, 1) % 2 == 0
    out = []
    for h in range(t.shape[1] // R_QK_DIM):
        th = t[:, h * R_QK_DIM:(h + 1) * R_QK_DIM]
        partner = jnp.where(even, pltpu.roll(th, R_QK_DIM - 1, 1), pltpu.roll(th, 1, 1))
        out.append((th * cos_i + partner * sin_i) * scale)
    return jnp.concatenate(out, axis=1)


def _proj_kernel(segments, x_ref, g_ref, sh_ref, sc_ref, gsgu_ref, cos_ref, sin_ref, w_ref, o_ref, h_ref):
    h_ref[...] = _modnorm(x_ref[0], g_ref[...], sh_ref[0], sc_ref[0]).astype(_BF16)
    for start, width, kind in segments:
        cols = slice(start, start + width)
        acc = _dot(h_ref[...], w_ref[0, :, cols])
        if kind == "gelu_ln":
            acc = _layernorm(jax.nn.gelu(acc)) * gsgu_ref[...]
        elif kind == "rot_q":
            acc = _rotate_heads(acc, cos_ref[...], sin_ref[...], 1.0)
        elif kind == "rot_k":
            acc = _rotate_heads(acc, cos_ref[...], sin_ref[...], R_QK_DIM ** -0.5)
        o_ref[0, :, cols] = acc.astype(_BF16)


_MIX_SEGMENTS = ((D_A, D_A, "gelu_ln"), (2 * D_A, D_RQK, "rot_q"), (2 * D_A + D_RQK, D_RQK, "rot_k"),
                 (0, D_A, "plain"), (KV_START + D_RQK, D_RV, "plain"), (KV_END, D_RV, "plain"),
                 (KV_END + D_RV, D_MODEL, "plain"), (KV_END + D_RV + D_MODEL, D_MODEL, "plain"))
_VK_SEGMENTS = ((D_RV, D_RQK, "rot_k"), (0, D_RV, "plain"))


def _proj_in(x, g, shift, scale, gsgu, cos_i, sin_i, w, layer, segments):
    b, n, d = x.shape
    dout = w.shape[2]
    tm = min(n, 1024)
    return pl.pallas_call(
        functools.partial(_proj_kernel, segments),
        grid=(b, n // tm),
        in_specs=[
            pl.BlockSpec((1, tm, d), lambda i, r: (i, r, 0)),
            pl.BlockSpec((1, d), lambda i, r: (0, 0)),
            pl.BlockSpec((1, 1, d), lambda i, r: (i, 0, 0)),
            pl.BlockSpec((1, 1, d), lambda i, r: (i, 0, 0)),
            pl.BlockSpec((1, D_A), lambda i, r: (0, 0)),
            pl.BlockSpec((tm, LANES), lambda i, r: (r, 0)),
            pl.BlockSpec((tm, LANES), lambda i, r: (r, 0)),
            pl.BlockSpec((1, d, dout), lambda i, r: (layer, 0, 0), pipeline_mode=pl.Buffered(1)),
        ],
        out_specs=pl.BlockSpec((1, tm, dout), lambda i, r: (i, r, 0)),
        out_shape=jax.ShapeDtypeStruct((b, n, dout), _BF16),
        scratch_shapes=[pltpu.VMEM((tm, d), _BF16)],
        compiler_params=_cparams("parallel", "parallel"),
        name="proj_in",
    )(x, g.reshape(1, d), shift, scale, gsgu, cos_i, sin_i, w)


def _dot_t(a, b):
    return lax.dot_general(a, b, (((0,), (0,)), ((), ())), preferred_element_type=_F32)


def _ctx_state_kernel(k_ref, v_ref, wf_ref, wb_ref, sf_ref, sb_ref):
    for h in range(R_HEADS):
        kh = k_ref[0, :, h * R_QK_DIM:(h + 1) * R_QK_DIM].astype(_F32)
        vh = v_ref[0, :, h * R_V_DIM:(h + 1) * R_V_DIM]
        sf_ref[0, h] = _dot_t((kh * wf_ref[h]).astype(_BF16), vh)
        sb_ref[0, h] = _dot_t((kh * wb_ref[h]).astype(_BF16), vh)


def _ctx_states(pc, k_blk, v_blk, wf, wb):
    b, l, _ = pc.shape
    st = jax.ShapeDtypeStruct((b, R_HEADS, R_QK_DIM, R_V_DIM), _F32)
    st_spec = pl.BlockSpec((1, R_HEADS, R_QK_DIM, R_V_DIM), lambda i: (i, 0, 0, 0))
    return pl.pallas_call(
        _ctx_state_kernel,
        grid=(b,),
        in_specs=[
            pl.BlockSpec((1, l, D_RQK), lambda i: (i, 0, k_blk)),
            pl.BlockSpec((1, l, D_RV), lambda i: (i, 0, v_blk)),
            pl.BlockSpec((R_HEADS, l, LANES), lambda i: (0, 0, 0)),
            pl.BlockSpec((R_HEADS, l, LANES), lambda i: (0, 0, 0)),
        ],
        out_specs=[st_spec, st_spec],
        out_shape=[st, st],
        compiler_params=_cparams("parallel"),
        name="ctx_states",
    )(pc, pc, wf, wb)


def _bwd_state_kernel(n_chunks, cdec_ref, k_ref, v_ref, kdb_ref, s0_ref, o_ref, s_ref):
    @pl.when(pl.program_id(1) == 0)
    def _():
        s_ref[...] = s0_ref[0]

    for c in reversed(range(n_chunks)):
        rows = pl.ds(c * CHUNK, CHUNK)
        for h in range(R_HEADS):
            s = s_ref[h]
            o_ref[0, c, h] = s.astype(_BF16)
            kh = k_ref[0, rows, h * R_QK_DIM:(h + 1) * R_QK_DIM].astype(_F32)
            vh = v_ref[0, rows, h * R_V_DIM:(h + 1) * R_V_DIM]
            s_ref[h] = s * cdec_ref[1, h] + _dot_t((kh * kdb_ref[h]).astype(_BF16), vh)


def _bwd_states(p, kdb, cdec, s0_b):
    b, n, _ = p.shape
    rows = min(n, 1024)
    ncb, nblk = rows // CHUNK, n // rows
    rev = lambda r: nblk - 1 - r
    return pl.pallas_call(
        functools.partial(_bwd_state_kernel, ncb),
        grid=(b, nblk),
        in_specs=[
            pl.BlockSpec(memory_space=pltpu.SMEM),
            pl.BlockSpec((1, rows, D_RQK), lambda i, r: (i, rev(r), _K_BLK)),
            pl.BlockSpec((1, rows, D_RV), lambda i, r: (i, rev(r), _VR_BLK)),
            pl.BlockSpec((R_HEADS, CHUNK, LANES), lambda i, r: (0, 0, 0)),
            pl.BlockSpec((1, R_HEADS, R_QK_DIM, R_V_DIM), lambda i, r: (i, 0, 0, 0)),
        ],
        out_specs=pl.BlockSpec((1, ncb, R_HEADS, R_QK_DIM, R_V_DIM), lambda i, r: (i, rev(r), 0, 0, 0)),
        out_shape=jax.ShapeDtypeStruct((b, n // CHUNK, R_HEADS, R_QK_DIM, R_V_DIM), _BF16),
        scratch_shapes=[pltpu.VMEM((R_HEADS, R_QK_DIM, R_V_DIM), _F32)],
        compiler_params=_cparams("parallel", "arbitrary"),
        name="bwd_states",
    )(cdec, p, p, kdb, s0_b)


def _mix_kernel(n_chunks, cdec_ref, u_ref, va_ref, q_ref, k_ref, vr_ref, gr_ref, ga_ref, gb_ref, x_ref, gt_ref,
                dmat_ref, qdf_ref, qdb_ref, kdf_ref, sb_ref, s0_ref,
                ws_ref, bs_ref, wpa_ref, wpb_ref, wout_ref, g2_ref, sh2_ref, sc2_ref, wr_ref, br_ref,
                o_ref, h_ref, id_ref, wt_ref, cnt_ref, sf_ref, sa_ref, lo_ref):
    @pl.when(pl.program_id(1) == 0)
    def _():
        sf_ref[...] = s0_ref[0]

    for c in range(n_chunks):
        rows = pl.ds(c * CHUNK, CHUNK)
        vn = va_ref[0, rows, :]
        for g in range(A_GROUPS):
            cols = slice(g * A_GROUP_DIM, (g + 1) * A_GROUP_DIM)
            sa_ref[rows, cols] = _dot(ws_ref[g], vn[:, cols]) + bs_ref[g]
        for h in range(R_HEADS):
            qk = slice(h * R_QK_DIM, (h + 1) * R_QK_DIM)
            vv = slice(h * R_V_DIM, (h + 1) * R_V_DIM)
            qb, kb = q_ref[0, rows, qk], k_ref[0, rows, qk]
            qh, kh = qb.astype(_F32), kb.astype(_F32)
            vh = vr_ref[0, rows, vv]
            scores = lax.dot_general(qb, kb, (((1,), (1,)), ((), ())), preferred_element_type=_F32) * dmat_ref[h]
            o = _dot(scores.astype(_BF16), vh)
            s_f = sf_ref[h]
            o = o + _dot((qh * qdf_ref[h]).astype(_BF16), s_f.astype(_BF16))
            o = o + _dot((qh * qdb_ref[h]).astype(_BF16), sb_ref[0, c, h])
            lo_ref[rows, vv] = _layernorm(o)
            sf_ref[h] = s_f * cdec_ref[0, h] + _dot_t((kh * kdf_ref[h]).astype(_BF16), vh)

    gr = gr_ref[0].astype(_F32)
    a = _dot((jax.nn.gelu(u_ref[0].astype(_F32)) * sa_ref[...]).astype(_BF16), wpa_ref[...])
    b = _dot((gr * jax.nn.sigmoid(gr) * lo_ref[...]).astype(_BF16), wpb_ref[...])
    merged = jax.nn.sigmoid(ga_ref[0].astype(_F32)) * a + jax.nn.sigmoid(gb_ref[0].astype(_F32)) * b
    x_new = x_ref[0] + gt_ref[0] * _dot(merged.astype(_BF16), wout_ref[...])
    o_ref[0] = x_new
    _route(x_new, g2_ref, sh2_ref, sc2_ref, wr_ref, br_ref, h_ref, id_ref, wt_ref, cnt_ref)


def _token_mix(p, x, gt, tabs, sb, s0_f, ws, bsb, wpa, wpb, wout, g2, shift2, scale2, wr, br):
    b, n, d = x.shape
    t = b * n
    rows = min(n, 512)
    ncb = rows // CHUNK
    per_b = n // rows
    tiles = rows // MOE_TILE
    flat = lambda width: pl.BlockSpec((rows, width), lambda i, r: (i * per_b + r, 0))
    pspec = lambda width, blk: pl.BlockSpec((1, rows, width), lambda i, r: (i, r, blk))
    const = lambda shape: pl.BlockSpec(shape, lambda i, r: (0,) * len(shape))
    tab = const((R_HEADS, CHUNK, LANES))
    return pl.pallas_call(
        functools.partial(_mix_kernel, ncb),
        grid=(b, n // rows),
        in_specs=[
            pl.BlockSpec(memory_space=pltpu.SMEM),
            pspec(D_A, _U_BLK), pspec(D_A, _VA_BLK), pspec(D_RQK, _Q_BLK), pspec(D_RQK, _K_BLK),
            pspec(D_RV, _VR_BLK), pspec(D_RV, _GR_BLK), pspec(d, _GA_BLK), pspec(d, _GB_BLK),
            pl.BlockSpec((1, rows, d), lambda i, r: (i, r, 0)),
            pl.BlockSpec((1, 1, d), lambda i, r: (i, 0, 0)),
            tab, tab, tab, tab,
            pl.BlockSpec((1, ncb, R_HEADS, R_QK_DIM, R_V_DIM), lambda i, r: (i, r, 0, 0, 0)),
            pl.BlockSpec((1, R_HEADS, R_QK_DIM, R_V_DIM), lambda i, r: (i, 0, 0, 0)),
            const((A_GROUPS, CHUNK, CHUNK)), const((A_GROUPS, CHUNK, LANES)),
            const((D_A, d)), const((D_RV, d)), const((d, d)),
            const((1, d)), pl.BlockSpec((1, 1, d), lambda i, r: (i, 0, 0)), pl.BlockSpec((1, 1, d), lambda i, r: (i, 0, 0)),
            const((d, ROUTE_COLS)), const((1, ROUTE_COLS)),
        ],
        out_specs=[pl.BlockSpec((1, rows, d), lambda i, r: (i, r, 0)), flat(d), flat(ROUTE_COLS), flat(ROUTE_COLS),
                   pl.BlockSpec((tiles, SUBLANES, ROUTE_COLS), lambda i, r: (i * per_b + r, 0, 0))],
        out_shape=[jax.ShapeDtypeStruct((b, n, d), _F32), jax.ShapeDtypeStruct((t, d), _BF16),
                   jax.ShapeDtypeStruct((t, ROUTE_COLS), jnp.int32), jax.ShapeDtypeStruct((t, ROUTE_COLS), _F32),
                   jax.ShapeDtypeStruct((t // MOE_TILE, SUBLANES, ROUTE_COLS), jnp.int32)],
        scratch_shapes=[
            pltpu.VMEM((R_HEADS, R_QK_DIM, R_V_DIM), _F32),
            pltpu.VMEM((rows, D_A), _F32),
            pltpu.VMEM((rows, D_RV), _F32),
        ],
        compiler_params=_cparams("parallel", "arbitrary"),
        name="token_mix",
    )(tabs["cdec"], p, p, p, p, p, p, p, p, x, gt, tabs["dmat"], tabs["qdf"], tabs["qdb"], tabs["kdf"],
      sb, s0_f, ws, bsb, wpa, wpb, wout, g2.reshape(1, d), shift2, scale2, wr, br)


ROUTE_COLS = LANES
MOE_TILE = 256
MOE_SLOTS = 2 * MOE_TILE
SLOT_SUBROWS = D_MODEL // LANES


def _route(x, g_ref, sh_ref, sc_ref, wr_ref, br_ref, h_ref, id_ref, wt_ref, cnt_ref):
    h = _modnorm(x, g_ref[...], sh_ref[0], sc_ref[0])
    h_hi = h.astype(_BF16)
    h_ref[...] = h_hi
    h_lo = (h - h_hi.astype(_F32)).astype(_BF16)
    r_hi = _dot(h_hi, wr_ref[...])
    logits = r_hi + pltpu.roll(r_hi, ROUTE_COLS // 2, 1) + _dot(h_lo, wr_ref[...]) + br_ref[...]
    lane = lax.broadcasted_iota(jnp.int32, logits.shape, 1)
    neg = -jnp.inf
    gl = jnp.where(lane < MOE_GROUPS, logits, neg)
    g_top = jnp.max(gl, axis=-1, keepdims=True)
    g_sel = jnp.min(jnp.where(gl == g_top, lane, ROUTE_COLS), axis=-1, keepdims=True)
    lse = g_top + jnp.log(jnp.sum(jnp.exp(gl - g_top), axis=-1, keepdims=True))
    p_group = jnp.exp(g_top - lse)
    lo = MOE_GROUPS + g_sel * EXPERTS_PER_GROUP
    el = jnp.where((lane >= lo) & (lane < lo + EXPERTS_PER_GROUP), logits, neg)
    e1 = jnp.max(el, axis=-1, keepdims=True)
    i1 = jnp.min(jnp.where(el == e1, lane, ROUTE_COLS), axis=-1, keepdims=True)
    el2 = jnp.where(lane == i1, neg, el)
    e2 = jnp.max(el2, axis=-1, keepdims=True)
    i2 = jnp.min(jnp.where(el2 == e2, lane, ROUTE_COLS), axis=-1, keepdims=True)
    x2 = jnp.exp(e2 - e1)
    den = 1.0 + x2
    id1, id2 = i1 - MOE_GROUPS, i2 - MOE_GROUPS
    id_ref[...] = jnp.where(lane == 0, id1, jnp.where(lane == 1, id2, 0))
    wt_ref[...] = jnp.where(lane == 0, p_group * (1.0 / den), jnp.where(lane == 1, p_group * (x2 / den), 0.0))
    hits = jnp.where((lane == id1) | (lane == id2), 1, 0)
    for t in range(cnt_ref.shape[0]):
        tile_hits = jnp.sum(hits[t * MOE_TILE:(t + 1) * MOE_TILE], axis=0, keepdims=True)
        cnt_ref[t] = jnp.broadcast_to(tile_hits, (SUBLANES, ROUTE_COLS))


def _dispatch_plan(cnt):
    n_tiles = cnt.shape[0]
    a = n_tiles * MOE_SLOTS
    tot = jnp.sum(cnt, axis=0)
    padded = (tot + MOE_ROWS - 1) // MOE_ROWS * MOE_ROWS
    pend = jnp.cumsum(padded)
    pstart = pend - padded
    off = pstart[None, :] + jnp.cumsum(cnt, axis=0) - cnt
    n_blocks = (a + N_EXPERTS * (MOE_ROWS - 1) + MOE_ROWS - 1) // MOE_ROWS
    blk_start = jnp.arange(n_blocks, dtype=jnp.int32) * MOE_ROWS
    blk_e = jnp.minimum(jnp.sum((pend[None, :] <= blk_start[:, None]).astype(jnp.int32), axis=1), N_EXPERTS - 1)
    owner = (blk_e[:, None] == jnp.arange(N_EXPERTS, dtype=jnp.int32)[None, :]).astype(jnp.int32)
    used_end = jnp.sum(owner * (pstart + tot)[None, :], axis=1)
    blk_rows = jnp.clip(used_end - blk_start, 0, MOE_ROWS)
    pad_start = jnp.concatenate([pstart + tot, blk_start])
    pad_len = jnp.concatenate([padded - tot, jnp.where(blk_start >= pend[-1], MOE_ROWS, 0)])
    i32 = lambda v: v.astype(jnp.int32)
    return (i32(off).reshape(-1), i32(cnt).reshape(-1), i32(pad_start), i32(pad_len)), i32(blk_e), i32(blk_rows), n_blocks


def _copy_runs(cnt_ref, off_ref, tile, make_copy):
    def body(e, src):
        n = cnt_ref[tile * N_EXPERTS + e]

        @pl.when(n > 0)
        def _():
            make_copy(src, off_ref[tile * N_EXPERTS + e], n).start()
        return src + n
    lax.fori_loop(0, N_EXPERTS, body, 0)


def _slot_rows(first, n):
    s = SLOT_SUBROWS
    return pl.ds(pl.multiple_of(first * s, s), pl.multiple_of(n * s, s))


def _load_slot_major(ref, n_slots):
    return jnp.concatenate([ref[pl.ds(s, n_slots, stride=SLOT_SUBROWS), :] for s in range(SLOT_SUBROWS)], axis=1)


def _store_slot_major(ref, val):
    for s in range(SLOT_SUBROWS):
        ref[pl.ds(s, val.shape[0], stride=SLOT_SUBROWS), :] = val[:, s * LANES:(s + 1) * LANES]


def _wait_slot(stage, sem, s):
    pltpu.make_async_copy(stage.at[s], stage.at[s], sem.at[s]).wait()


def _dispatch_kernel(n_main, off_ref, cnt_ref, pad_start_ref, pad_len_ref, h_ref, id_ref, hc_ref, idc_ref,
                     xs_hbm, pos_ref, stage, sem, zeros, zsem):
    i = pl.program_id(0)
    slot = i % 2
    from_main = i < n_main
    ids = id_ref[...] if idc_ref is None else jnp.where(from_main, id_ref[...], idc_ref[...])
    h = h_ref[...] if hc_ref is None else jnp.where(from_main, h_ref[...], hc_ref[...])

    @pl.when(i == 0)
    def _():
        zeros[...] = jnp.zeros(zeros.shape, zeros.dtype)

        def pad_copy(r):
            n = pad_len_ref[r]
            return n, pltpu.make_async_copy(zeros.at[_slot_rows(0, n)], xs_hbm.at[_slot_rows(pad_start_ref[r], n)], zsem)

        def start(r, carry):
            n, copy = pad_copy(r)

            @pl.when(n > 0)
            def _():
                copy.start()
            return carry

        def wait(r, carry):
            n, copy = pad_copy(r)

            @pl.when(n > 0)
            def _():
                copy.wait()
            return carry

        lax.fori_loop(0, pad_len_ref.shape[0], start, 0)
        lax.fori_loop(0, pad_len_ref.shape[0], wait, 0)

    lane = lax.broadcasted_iota(jnp.int32, ids.shape, 1)
    e0 = jnp.where(lane == ids[:, 0:1], 1.0, 0.0)
    e1 = jnp.where(lane == ids[:, 1:2], 1.0, 0.0)
    tt = lax.broadcasted_iota(jnp.int32, (MOE_TILE, MOE_TILE), 0)
    earlier = jnp.where(lax.broadcasted_iota(jnp.int32, (MOE_TILE, MOE_TILE), 1) < tt, 1.0, 0.0).astype(_BF16)
    c0 = _dot(earlier, e0.astype(_BF16))
    c1 = _dot(earlier, e1.astype(_BF16))
    tot0 = jnp.sum(e0, axis=0, keepdims=True)
    ee = lax.broadcasted_iota(jnp.int32, (ROUTE_COLS, ROUTE_COLS), 0)
    lower = jnp.where(ee < lax.broadcasted_iota(jnp.int32, (ROUTE_COLS, ROUTE_COLS), 1), 1.0, 0.0).astype(_BF16)
    base = jnp.sum(_dot((e0 + e1).astype(_BF16), lower), axis=0, keepdims=True)
    pos0 = jnp.sum(e0 * (base + c0), axis=1, keepdims=True)
    pos1 = jnp.sum(e1 * (base + tot0 + c1), axis=1, keepdims=True)
    pos = jnp.where(lane == 0, pos0, jnp.where(lane == 1, pos1, 0.0))
    pos_ref[...] = pos
    pos_t = pos.T
    slot_id = lax.broadcasted_iota(jnp.int32, (MOE_SLOTS, MOE_TILE), 0).astype(_F32)
    pick = jnp.where((slot_id == pos_t[0:1, :]) | (slot_id == pos_t[1:2, :]), 1.0, 0.0).astype(_BF16)
    rows = _dot(pick, h)

    @pl.when(i >= 2)
    def _():
        _wait_slot(stage, sem, slot)

    _store_slot_major(stage.at[slot], rows)
    _copy_runs(cnt_ref, off_ref, i, lambda src, dst, n: pltpu.make_async_copy(
        stage.at[slot, _slot_rows(src, n)], xs_hbm.at[_slot_rows(dst, n)], sem.at[slot]))

    @pl.when(i == pl.num_programs(0) - 1)
    def _():
        _wait_slot(stage, sem, slot)

        @pl.when(i >= 1)
        def _():
            _wait_slot(stage, sem, 1 - slot)


def _dispatch(routed, routed_c, plan, n_blocks):
    d = routed[0].shape[1]
    n_main = routed[0].shape[0] // MOE_TILE
    n_ctx = 0 if routed_c is None else routed_c[0].shape[0] // MOE_TILE
    n_tiles = n_main + n_ctx
    t = n_tiles * MOE_TILE
    pair = lambda f: [pl.BlockSpec((MOE_TILE, d), lambda i, *_: (f(i), 0)),
                      pl.BlockSpec((MOE_TILE, ROUTE_COLS), lambda i, *_: (f(i), 0))]
    if routed_c is None:
        body = lambda *refs: _dispatch_kernel(n_main, *refs[:6], None, None, *refs[6:])
    else:
        body = functools.partial(_dispatch_kernel, n_main)
    grid_spec = pltpu.PrefetchScalarGridSpec(
        num_scalar_prefetch=4,
        grid=(n_tiles,),
        in_specs=pair(lambda i: jnp.minimum(i, n_main - 1)) + ([] if routed_c is None else
                                                              pair(lambda i: jnp.maximum(i - n_main, 0))),
        out_specs=[pl.BlockSpec(memory_space=pl.ANY), pl.BlockSpec((MOE_TILE, ROUTE_COLS), lambda i, *_: (i, 0))],
        scratch_shapes=[pltpu.VMEM((2, MOE_SLOTS * SLOT_SUBROWS, LANES), _F32), pltpu.SemaphoreType.DMA((2,)),
                        pltpu.VMEM((MOE_ROWS * SLOT_SUBROWS, LANES), _F32), pltpu.SemaphoreType.DMA(())],
    )
    return pl.pallas_call(
        body,
        grid_spec=grid_spec,
        out_shape=[jax.ShapeDtypeStruct((n_blocks * MOE_ROWS * SLOT_SUBROWS, LANES), _F32),
                   jax.ShapeDtypeStruct((t, ROUTE_COLS), _F32)],
        compiler_params=_cparams("arbitrary"),
        name="moe_dispatch",
    )(*plan, *routed, *(routed_c or ()))


def _expert_kernel(blk_e_ref, rows_ref, x_ref, w1_ref, w3_ref, w2_ref, o_ref, w1b, w3b, w2b):
    i = pl.program_id(0)
    rows = rows_ref[i]

    @pl.when((i == 0) | (blk_e_ref[i] != blk_e_ref[jnp.maximum(i - 1, 0)]))
    def _():
        w1b[...] = w1_ref[0, 0].astype(_BF16)
        w3b[...] = w3_ref[0, 0].astype(_BF16)
        w2b[...] = w2_ref[0, 0].astype(_BF16)

    def ffn(first, n_slots):
        span = pl.ds(first * SLOT_SUBROWS, n_slots * SLOT_SUBROWS)
        xb = _load_slot_major(x_ref.at[span], n_slots).astype(_BF16)
        h1 = _dot(xb, w1b[...])
        h3 = _dot(xb, w3b[...])
        act = (h1 * jax.nn.sigmoid(h1) * h3).astype(_BF16)
        _store_slot_major(o_ref.at[span], _dot(act, w2b[...]))

    def clear(first, n_slots):
        o_ref[pl.ds(first * SLOT_SUBROWS, n_slots * SLOT_SUBROWS), :] = jnp.zeros((n_slots * SLOT_SUBROWS, LANES), o_ref.dtype)

    @pl.when(rows > MOE_HALF)
    def _():
        ffn(0, MOE_ROWS)

    @pl.when((rows > 0) & (rows <= MOE_HALF))
    def _():
        ffn(0, MOE_HALF)
        clear(MOE_HALF, MOE_HALF)

    @pl.when(rows == 0)
    def _():
        clear(0, MOE_ROWS)


def _experts(xs, blk_e, blk_rows, layer, w1, w3, w2):
    d = w1.shape[2]
    blk = MOE_ROWS * SLOT_SUBROWS
    n_blocks = xs.shape[0] // blk
    grid_spec = pltpu.PrefetchScalarGridSpec(
        num_scalar_prefetch=2,
        grid=(n_blocks,),
        in_specs=[
            pl.BlockSpec((blk, LANES), lambda i, be, br: (i, 0)),
            pl.BlockSpec((1, 1, d, EXPERT_FF), lambda i, be, br: (layer, be[i], 0, 0)),
            pl.BlockSpec((1, 1, d, EXPERT_FF), lambda i, be, br: (layer, be[i], 0, 0)),
            pl.BlockSpec((1, 1, EXPERT_FF, d), lambda i, be, br: (layer, be[i], 0, 0)),
        ],
        out_specs=pl.BlockSpec((blk, LANES), lambda i, be, br: (i, 0)),
        scratch_shapes=[pltpu.VMEM((d, EXPERT_FF), _BF16), pltpu.VMEM((d, EXPERT_FF), _BF16),
                        pltpu.VMEM((EXPERT_FF, d), _BF16)],
    )
    return pl.pallas_call(
        _expert_kernel,
        grid_spec=grid_spec,
        out_shape=jax.ShapeDtypeStruct(xs.shape, xs.dtype),
        compiler_params=_cparams("arbitrary"),
        name="experts",
    )(blk_e, blk_rows, xs, w1, w3, w2)


def _combine_kernel(final, tile0, off_ref, cnt_ref, x_ref, pos_ref, wt_ref, gt_ref, g_ref, ys_hbm, o_ref, stage, sem):
    i = pl.program_id(0)
    slot = i % 2

    def fetch(tile, s):
        _copy_runs(cnt_ref, off_ref, tile, lambda src, dst, n: pltpu.make_async_copy(
            ys_hbm.at[_slot_rows(dst, n)], stage.at[s, _slot_rows(src, n)], sem.at[s]))

    @pl.when(i == 0)
    def _():
        fetch(tile0, 0)

    @pl.when(i + 1 < pl.num_programs(0))
    def _():
        fetch(tile0 + i + 1, 1 - slot)

    _wait_slot(stage, sem, slot)
    y = _load_slot_major(stage.at[slot], MOE_SLOTS).astype(_BF16)
    slot_id = lax.broadcasted_iota(jnp.int32, (MOE_TILE, MOE_SLOTS), 1).astype(_F32)
    mix = (jnp.where(slot_id == pos_ref[:, 0:1], wt_ref[:, 0:1], 0.0)
           + jnp.where(slot_id == pos_ref[:, 1:2], wt_ref[:, 1:2], 0.0)).astype(_BF16)
    x = x_ref[...] + gt_ref[0] * _dot(mix, y)
    if final:
        x = (x * lax.rsqrt(jnp.mean(x * x, axis=-1, keepdims=True) + EPS)) * g_ref[...]
    o_ref[...] = x


def _combine(x, ys, pos, wts, off, cnt, gt, g_final, tok0, final):
    b, n, d = x.shape
    t = b * n
    per_b = n // MOE_TILE
    tile0 = tok0 // MOE_TILE
    row = pl.BlockSpec((MOE_TILE, d), lambda i, o, c: (i, 0))
    lanes = lambda first: pl.BlockSpec((MOE_TILE, ROUTE_COLS), lambda i, o, c: (first + i, 0))
    grid_spec = pltpu.PrefetchScalarGridSpec(
        num_scalar_prefetch=2,
        grid=(t // MOE_TILE,),
        in_specs=[row, lanes(tile0), lanes(0), pl.BlockSpec((1, 1, d), lambda i, o, c: (i // per_b, 0, 0)),
                  pl.BlockSpec((1, d), lambda i, o, c: (0, 0)), pl.BlockSpec(memory_space=pl.ANY)],
        out_specs=row,
        scratch_shapes=[pltpu.VMEM((2, MOE_SLOTS * SLOT_SUBROWS, LANES), _F32), pltpu.SemaphoreType.DMA((2,))],
    )
    out = pl.pallas_call(
        functools.partial(_combine_kernel, final, tile0),
        grid_spec=grid_spec,
        out_shape=jax.ShapeDtypeStruct((t, d), _F32),
        compiler_params=_cparams("arbitrary"),
        name="moe_combine",
    )(off, cnt, x.reshape(t, d), pos, wts, gt, g_final.reshape(1, d), ys)
    return out.reshape(b, n, d)


def _rotary_tables(n):
    n_freq = R_QK_DIM // 4
    inv = ROPE_BASE ** (-jnp.arange(n_freq, dtype=_F32) / n_freq)
    rows = n // GRID_W
    row = jnp.repeat(jnp.arange(rows, dtype=_F32), GRID_W)
    col = jnp.tile(jnp.arange(GRID_W, dtype=_F32), rows)
    ang = jnp.concatenate([row[:, None] * inv, col[:, None] * inv], axis=-1)
    cos, sin = jnp.cos(ang), jnp.sin(ang)
    return jnp.repeat(cos, 2, axis=-1), jnp.stack([-sin, sin], axis=-1).reshape(n, R_QK_DIM)


def _decay_tables(decay_logit, l_ctx):
    log_g = jax.nn.log_sigmoid(decay_logit.astype(_F32))
    lgf, lgb = log_g[0][:, None], log_g[1][:, None]
    j = jnp.arange(CHUNK, dtype=_F32)
    diff = j[:, None] - j[None, :]
    dmat = jnp.where(diff[None] >= 0, jnp.exp(lgf[:, :, None] * jnp.maximum(diff, 0.0)[None]),
                     jnp.exp(lgb[:, :, None] * jnp.maximum(-diff, 0.0)[None]))
    lanes = lambda t: jnp.broadcast_to(t[:, :, None], t.shape + (LANES,))
    m = jnp.arange(l_ctx, dtype=_F32)
    return {
        "dmat": dmat,
        "qdf": lanes(jnp.exp(lgf * (j + 1.0))), "qdb": lanes(jnp.exp(lgb * (CHUNK - j))),
        "kdf": lanes(jnp.exp(lgf * (CHUNK - 1.0 - j))), "kdb": lanes(jnp.exp(lgb * j)),
        "cdec": jnp.exp(log_g * CHUNK),
        "ctx_f": lanes(jnp.exp(lgf * ((l_ctx - 1.0) - m))), "ctx_b": lanes(jnp.exp(lgb * m)),
    }


def kernel(x, c, ctx, c_ctx, w_ada, b_ada, g_mix, g_ffn, w_in, w_s, b_s, g_sgu, decay_logit, w_pa, w_pb, w_out,
           w_group, b_group, w_erouter, b_erouter, w1, w3, w2, g_final):
    bsz, n, d = x.shape
    l_ctx = ctx.shape[1]
    t_main = bsz * n

    cond = jnp.zeros((COND_ROWS, d), _F32).at[:bsz].set(c).at[bsz].set(c_ctx)
    mods = _adaln(cond, w_ada, b_ada)
    rot = _rotary_tables(n)
    no_rot = (jnp.ones((l_ctx, LANES), _F32), jnp.zeros((l_ctx, LANES), _F32))
    zero_state = jnp.zeros((bsz, R_HEADS, R_QK_DIM, R_V_DIM), _F32)

    w_in_b = w_in.astype(_BF16)
    xc = ctx
    for l in range(DEPTH):
        update_ctx = l < DEPTH - 1
        m6 = mods[l].reshape(COND_ROWS, 6, d)
        main = [m6[:bsz, i][:, None, :] for i in range(6)]
        cmod = [jnp.broadcast_to(m6[bsz, i][None, None, :], (bsz, 1, d)) for i in range(6)]
        tabs = _decay_tables(decay_logit[l], l_ctx)
        gsgu = g_sgu[l].reshape(1, D_A)
        mix_w = (w_s[l].astype(_BF16), jnp.broadcast_to(b_s[l][:, :, None], (A_GROUPS, CHUNK, LANES)),
                 w_pa[l].astype(_BF16), w_pb[l].astype(_BF16), w_out[l].astype(_BF16))

        p = _proj_in(x, g_mix[l], main[0], main[1], gsgu, *rot, w_in_b, l, _MIX_SEGMENTS)
        if update_ctx:
            pc = _proj_in(xc, g_mix[l], cmod[0], cmod[1], gsgu, *no_rot, w_in_b, l, _MIX_SEGMENTS)
            s_ctx_f, s_ctx_b = _ctx_states(pc, _K_BLK, _VR_BLK, tabs["ctx_f"], tabs["ctx_b"])
        else:
            w_vk = jnp.concatenate([w_in_b[l:l + 1, :, KV_START + D_RQK:KV_END],
                                    w_in_b[l:l + 1, :, KV_START:KV_START + D_RQK]], axis=2)
            vk_c = _proj_in(xc, g_mix[l], cmod[0], cmod[1], gsgu, *no_rot, w_vk, 0, _VK_SEGMENTS)
            s_ctx_f, s_ctx_b = _ctx_states(vk_c, D_RV // D_RQK, 0, tabs["ctx_f"], tabs["ctx_b"])
        sb = _bwd_states(p, tabs["kdb"], tabs["cdec"], s_ctx_b)
        wr = jnp.zeros((d, ROUTE_COLS // 2), _F32).at[:, :MOE_GROUPS].set(w_group[l])
        wr = wr.at[:, MOE_GROUPS:MOE_GROUPS + N_EXPERTS].set(w_erouter[l])
        wr_hi = wr.astype(_BF16)
        wr = jnp.concatenate([wr_hi, (wr - wr_hi.astype(_F32)).astype(_BF16)], axis=1)
        br = jnp.zeros((1, ROUTE_COLS), _F32).at[0, :MOE_GROUPS].set(b_group[l])
        br = br.at[0, MOE_GROUPS:MOE_GROUPS + N_EXPERTS].set(b_erouter[l])
        route_w = (g_ffn[l], wr, br)
        x, h2, ids, wts, cnt = _token_mix(p, x, main[2], tabs, sb, s_ctx_f, *mix_w, route_w[0], main[3], main[4],
                                          *route_w[1:])
        routed_c = None
        if update_ctx:
            sbc = _bwd_states(pc, tabs["kdb"], tabs["cdec"], zero_state)
            xc, h2c, idsc, wtsc, cntc = _token_mix(pc, xc, cmod[2], tabs, sbc, zero_state, *mix_w, route_w[0],
                                                   cmod[3], cmod[4], *route_w[1:])
            routed_c = (h2c, idsc)
            cnt = jnp.concatenate([cnt, cntc], axis=0)

        plan, blk_e, blk_rows, n_blocks = _dispatch_plan(cnt[:, 0, :N_EXPERTS])
        off, cnt = plan[:2]
        xs, pos = _dispatch((h2, ids), routed_c, plan, n_blocks)
        ys = _experts(xs, blk_e, blk_rows, l, w1, w3, w2)
        final = l == DEPTH - 1
        x = _combine(x, ys, pos, wts, off, cnt, main[5], g_final, 0, final)
        if update_ctx:
            xc = _combine(xc, ys, pos, wtsc, off, cnt, cmod[5], g_final, t_main, False)
    return x
```

```python
import functools

import numpy as np
import jax
import jax.numpy as jnp
from jax import lax
from jax.experimental import pallas as pl
from jax.experimental.pallas import tpu as pltpu

D_MODEL = 1024
DEPTH = 2
GRID_W = 64
CHUNK = 128
A_GROUP_DIM = 128
A_GROUPS = D_MODEL // A_GROUP_DIM
D_A = A_GROUPS * A_GROUP_DIM
R_HEADS = 4
R_QK_DIM = D_MODEL // 8
R_V_DIM = D_MODEL // 4
D_RQK = R_HEADS * R_QK_DIM
D_RV = R_HEADS * R_V_DIM
D_IN = 2 * D_A + 2 * D_RQK + 2 * D_RV + 2 * D_MODEL
KV_START = 2 * D_A + D_RQK
KV_END = KV_START + D_RQK + D_RV
ROPE_BASE = 10000.0
MOE_GROUPS = 4
EXPERTS_PER_GROUP = 8
N_EXPERTS = MOE_GROUPS * EXPERTS_PER_GROUP
EXPERT_FF = D_MODEL // 2
EPS = 1e-6

LANES = 128
SUBLANES = 8
MOE_ROWS = 512
MOE_HALF = MOE_ROWS // 2
VMEM_LIMIT = 56 * 1024 * 1024
COND_ROWS = 16

_U_BLK, _VA_BLK = 0, 1
_Q_BLK, _K_BLK = 2 * D_A // D_RQK, 2 * D_A // D_RQK + 1
_VR_BLK = (2 * D_A + 2 * D_RQK) // D_RV
_GR_BLK, _GA_BLK, _GB_BLK = _VR_BLK + 1, _VR_BLK + 2, _VR_BLK + 3

_F32 = jnp.float32
_BF16 = jnp.bfloat16


def _cparams(*sem):
    return pltpu.CompilerParams(dimension_semantics=sem, vmem_limit_bytes=VMEM_LIMIT)


def _dot(a, b):
    return jnp.dot(a, b, preferred_element_type=_F32)


def _adaln_kernel(cond_ref, w_ref, b_ref, o_ref):
    cnd = cond_ref[...]
    s = cnd * jax.nn.sigmoid(cnd)
    o_ref[0] = jnp.dot(s, w_ref[0], preferred_element_type=_F32, precision=lax.Precision.HIGHEST) + b_ref[0]


def _adaln(cond, w_ada, b_ada):
    depth, d, d6 = w_ada.shape
    tn = 1024
    return pl.pallas_call(
        _adaln_kernel,
        grid=(depth, d6 // tn),
        in_specs=[
            pl.BlockSpec((COND_ROWS, d), lambda l, j: (0, 0)),
            pl.BlockSpec((1, d, tn), lambda l, j: (l, 0, j)),
            pl.BlockSpec((1, 1, tn), lambda l, j: (l, 0, j)),
        ],
        out_specs=pl.BlockSpec((1, COND_ROWS, tn), lambda l, j: (l, 0, j)),
        out_shape=jax.ShapeDtypeStruct((depth, COND_ROWS, d6), _F32),
        compiler_params=_cparams("parallel", "parallel"),
        name="adaln",
    )(cond, w_ada, b_ada.reshape(depth, 1, d6))


def _modnorm(x, g, shift, scale):
    y = x * lax.rsqrt(jnp.mean(x * x, axis=-1, keepdims=True) + EPS)
    return (y * g) * (1.0 + scale) + shift


def _layernorm(t):
    mu = jnp.mean(t, axis=-1, keepdims=True)
    tc = t - mu
    return tc * lax.rsqrt(jnp.mean(tc * tc, axis=-1, keepdims=True) + EPS)


def _rotate_heads(t, cos_i, sin_i, scale):
    even = lax.broadcasted_iota(jnp.int32, cos_i.shape, 1) % 2 == 0
    out = []
    for h in range(t.shape[1] // R_QK_DIM):
        th = t[:, h * R_QK_DIM:(h + 1) * R_QK_DIM]
        partner = jnp.where(even, pltpu.roll(th, R_QK_DIM - 1, 1), pltpu.roll(th, 1, 1))
        out.append((th * cos_i + partner * sin_i) * scale)
    return jnp.concatenate(out, axis=1)


def _proj_kernel(segments, x_ref, g_ref, sh_ref, sc_ref, gsgu_ref, cos_ref, sin_ref, w_ref, o_ref, h_ref):
    h_ref[...] = _modnorm(x_ref[0], g_ref[...], sh_ref[0], sc_ref[0]).astype(_BF16)
    for start, width, kind in segments:
        cols = slice(start, start + width)
        acc = _dot(h_ref[...], w_ref[0, :, cols])
        if kind == "gelu_ln":
            acc = _layernorm(jax.nn.gelu(acc)) * gsgu_ref[...]
        elif kind == "rot_q":
            acc = _rotate_heads(acc, cos_ref[...], sin_ref[...], 1.0)
        elif kind == "rot_k":
            acc = _rotate_heads(acc, cos_ref[...], sin_ref[...], R_QK_DIM ** -0.5)
        o_ref[0, :, cols] = acc.astype(_BF16)


_MIX_SEGMENTS = ((D_A, D_A, "gelu_ln"), (2 * D_A, D_RQK, "rot_q"), (2 * D_A + D_RQK, D_RQK, "rot_k"),
                 (0, D_A, "plain"), (KV_START + D_RQK, D_RV, "plain"), (KV_END, D_RV, "plain"),
                 (KV_END + D_RV, D_MODEL, "plain"), (KV_END + D_RV + D_MODEL, D_MODEL, "plain"))
_VK_SEGMENTS = ((D_RV, D_RQK, "rot_k"), (0, D_RV, "plain"))


def _proj_in(x, g, shift, scale, gsgu, cos_i, sin_i, w, layer, segments):
    b, n, d = x.shape
    dout = w.shape[2]
    tm = min(n, 512)
    return pl.pallas_call(
        functools.partial(_proj_kernel, segments),
        grid=(b, n // tm),
        in_specs=[
            pl.BlockSpec((1, tm, d), lambda i, r: (i, r, 0)),
            pl.BlockSpec((1, d), lambda i, r: (0, 0)),
            pl.BlockSpec((1, 1, d), lambda i, r: (i, 0, 0)),
            pl.BlockSpec((1, 1, d), lambda i, r: (i, 0, 0)),
            pl.BlockSpec((1, D_A), lambda i, r: (0, 0)),
            pl.BlockSpec((tm, LANES), lambda i, r: (r, 0)),
            pl.BlockSpec((tm, LANES), lambda i, r: (r, 0)),
            pl.BlockSpec((1, d, dout), lambda i, r: (layer, 0, 0), pipeline_mode=pl.Buffered(1)),
        ],
        out_specs=pl.BlockSpec((1, tm, dout), lambda i, r: (i, r, 0)),
        out_shape=jax.ShapeDtypeStruct((b, n, dout), _BF16),
        scratch_shapes=[pltpu.VMEM((tm, d), _BF16)],
        compiler_params=_cparams("parallel", "parallel"),
        name="proj_in",
    )(x, g.reshape(1, d), shift, scale, gsgu, cos_i, sin_i, w)


def _dot_t(a, b):
    return lax.dot_general(a, b, (((0,), (0,)), ((), ())), preferred_element_type=_F32)


def _ctx_state_kernel(k_ref, v_ref, wf_ref, wb_ref, sf_ref, sb_ref):
    for h in range(R_HEADS):
        kh = k_ref[0, :, h * R_QK_DIM:(h + 1) * R_QK_DIM].astype(_F32)
        vh = v_ref[0, :, h * R_V_DIM:(h + 1) * R_V_DIM]
        sf_ref[0, h] = _dot_t((kh * wf_ref[h]).astype(_BF16), vh)
        sb_ref[0, h] = _dot_t((kh * wb_ref[h]).astype(_BF16), vh)


def _ctx_states(pc, k_blk, v_blk, wf, wb):
    b, l, _ = pc.shape
    st = jax.ShapeDtypeStruct((b, R_HEADS, R_QK_DIM, R_V_DIM), _F32)
    st_spec = pl.BlockSpec((1, R_HEADS, R_QK_DIM, R_V_DIM), lambda i: (i, 0, 0, 0))
    return pl.pallas_call(
        _ctx_state_kernel,
        grid=(b,),
        in_specs=[
            pl.BlockSpec((1, l, D_RQK), lambda i: (i, 0, k_blk)),
            pl.BlockSpec((1, l, D_RV), lambda i: (i, 0, v_blk)),
            pl.BlockSpec((R_HEADS, l, LANES), lambda i: (0, 0, 0)),
            pl.BlockSpec((R_HEADS, l, LANES), lambda i: (0, 0, 0)),
        ],
        out_specs=[st_spec, st_spec],
        out_shape=[st, st],
        compiler_params=_cparams("parallel"),
        name="ctx_states",
    )(pc, pc, wf, wb)


def _bwd_state_kernel(n_chunks, cdec_ref, k_ref, v_ref, kdb_ref, s0_ref, o_ref, s_ref):
    @pl.when(pl.program_id(1) == 0)
    def _():
        s_ref[...] = s0_ref[0]

    for c in reversed(range(n_chunks)):
        rows = pl.ds(c * CHUNK, CHUNK)
        for h in range(R_HEADS):
            s = s_ref[h]
            o_ref[0, c, h] = s.astype(_BF16)
            kh = k_ref[0, rows, h * R_QK_DIM:(h + 1) * R_QK_DIM].astype(_F32)
            vh = v_ref[0, rows, h * R_V_DIM:(h + 1) * R_V_DIM]
            s_ref[h] = s * cdec_ref[1, h] + _dot_t((kh * kdb_ref[h]).astype(_BF16), vh)


def _bwd_states(p, kdb, cdec, s0_b):
    b, n, _ = p.shape
    rows = min(n, 1024)
    ncb, nblk = rows // CHUNK, n // rows
    rev = lambda r: nblk - 1 - r
    return pl.pallas_call(
        functools.partial(_bwd_state_kernel, ncb),
        grid=(b, nblk),
        in_specs=[
            pl.BlockSpec(memory_space=pltpu.SMEM),
            pl.BlockSpec((1, rows, D_RQK), lambda i, r: (i, rev(r), _K_BLK)),
            pl.BlockSpec((1, rows, D_RV), lambda i, r: (i, rev(r), _VR_BLK)),
            pl.BlockSpec((R_HEADS, CHUNK, LANES), lambda i, r: (0, 0, 0)),
            pl.BlockSpec((1, R_HEADS, R_QK_DIM, R_V_DIM), lambda i, r: (i, 0, 0, 0)),
        ],
        out_specs=pl.BlockSpec((1, ncb, R_HEADS, R_QK_DIM, R_V_DIM), lambda i, r: (i, rev(r), 0, 0, 0)),
        out_shape=jax.ShapeDtypeStruct((b, n // CHUNK, R_HEADS, R_QK_DIM, R_V_DIM), _BF16),
        scratch_shapes=[pltpu.VMEM((R_HEADS, R_QK_DIM, R_V_DIM), _F32)],
        compiler_params=_cparams("parallel", "arbitrary"),
        name="bwd_states",
    )(cdec, p, p, kdb, s0_b)


def _mix_kernel(n_chunks, cdec_ref, u_ref, va_ref, q_ref, k_ref, vr_ref, gr_ref, ga_ref, gb_ref, x_ref, gt_ref,
                dmat_ref, qdf_ref, qdb_ref, kdf_ref, sb_ref, s0_ref,
                ws_ref, bs_ref, wpa_ref, wpb_ref, wout_ref, g2_ref, sh2_ref, sc2_ref, wr_ref, br_ref,
                o_ref, h_ref, id_ref, wt_ref, cnt_ref, sf_ref, sa_ref, lo_ref):
    @pl.when(pl.program_id(1) == 0)
    def _():
        sf_ref[...] = s0_ref[0]

    for c in range(n_chunks):
        rows = pl.ds(c * CHUNK, CHUNK)
        vn = va_ref[0, rows, :]
        for g in range(A_GROUPS):
            cols = slice(g * A_GROUP_DIM, (g + 1) * A_GROUP_DIM)
            sa_ref[rows, cols] = _dot(ws_ref[g], vn[:, cols]) + bs_ref[g]
        for h in range(R_HEADS):
            qk = slice(h * R_QK_DIM, (h + 1) * R_QK_DIM)
            vv = slice(h * R_V_DIM, (h + 1) * R_V_DIM)
            qb, kb = q_ref[0, rows, qk], k_ref[0, rows, qk]
            qh, kh = qb.astype(_F32), kb.astype(_F32)
            vh = vr_ref[0, rows, vv]
            scores = lax.dot_general(qb, kb, (((1,), (1,)), ((), ())), preferred_element_type=_F32) * dmat_ref[h]
            o = _dot(scores.astype(_BF16), vh)
            s_f = sf_ref[h]
            o = o + _dot((qh * qdf_ref[h]).astype(_BF16), s_f.astype(_BF16))
            o = o + _dot((qh * qdb_ref[h]).astype(_BF16), sb_ref[0, c, h])
            lo_ref[rows, vv] = _layernorm(o)
            sf_ref[h] = s_f * cdec_ref[0, h] + _dot_t((kh * kdf_ref[h]).astype(_BF16), vh)

    gr = gr_ref[0].astype(_F32)
    a = _dot((jax.nn.gelu(u_ref[0].astype(_F32)) * sa_ref[...]).astype(_BF16), wpa_ref[...])
    b = _dot((gr * jax.nn.sigmoid(gr) * lo_ref[...]).astype(_BF16), wpb_ref[...])
    merged = jax.nn.sigmoid(ga_ref[0].astype(_F32)) * a + jax.nn.sigmoid(gb_ref[0].astype(_F32)) * b
    x_new = x_ref[0] + gt_ref[0] * _dot(merged.astype(_BF16), wout_ref[...])
    o_ref[0] = x_new
    _route(x_new, g2_ref, sh2_ref, sc2_ref, wr_ref, br_ref, h_ref, id_ref, wt_ref, cnt_ref)


def _token_mix(p, x, gt, tabs, sb, s0_f, ws, bsb, wpa, wpb, wout, g2, shift2, scale2, wr, br):
    b, n, d = x.shape
    t = b * n
    rows = min(n, 512)
    ncb = rows // CHUNK
    per_b = n // rows
    tiles = rows // MOE_TILE
    flat = lambda width: pl.BlockSpec((rows, width), lambda i, r: (i * per_b + r, 0))
    pspec = lambda width, blk: pl.BlockSpec((1, rows, width), lambda i, r: (i, r, blk))
    const = lambda shape: pl.BlockSpec(shape, lambda i, r: (0,) * len(shape))
    tab = const((R_HEADS, CHUNK, LANES))
    return pl.pallas_call(
        functools.partial(_mix_kernel, ncb),
        grid=(b, n // rows),
        in_specs=[
            pl.BlockSpec(memory_space=pltpu.SMEM),
            pspec(D_A, _U_BLK), pspec(D_A, _VA_BLK), pspec(D_RQK, _Q_BLK), pspec(D_RQK, _K_BLK),
            pspec(D_RV, _VR_BLK), pspec(D_RV, _GR_BLK), pspec(d, _GA_BLK), pspec(d, _GB_BLK),
            pl.BlockSpec((1, rows, d), lambda i, r: (i, r, 0)),
            pl.BlockSpec((1, 1, d), lambda i, r: (i, 0, 0)),
            tab, tab, tab, tab,
            pl.BlockSpec((1, ncb, R_HEADS, R_QK_DIM, R_V_DIM), lambda i, r: (i, r, 0, 0, 0)),
            pl.BlockSpec((1, R_HEADS, R_QK_DIM, R_V_DIM), lambda i, r: (i, 0, 0, 0)),
            const((A_GROUPS, CHUNK, CHUNK)), const((A_GROUPS, CHUNK, LANES)),
            const((D_A, d)), const((D_RV, d)), const((d, d)),
            const((1, d)), pl.BlockSpec((1, 1, d), lambda i, r: (i, 0, 0)), pl.BlockSpec((1, 1, d), lambda i, r: (i, 0, 0)),
            const((d, ROUTE_COLS)), const((1, ROUTE_COLS)),
        ],
        out_specs=[pl.BlockSpec((1, rows, d), lambda i, r: (i, r, 0)), flat(d), flat(ROUTE_COLS), flat(ROUTE_COLS),
                   pl.BlockSpec((tiles, SUBLANES, ROUTE_COLS), lambda i, r: (i * per_b + r, 0, 0))],
        out_shape=[jax.ShapeDtypeStruct((b, n, d), _F32), jax.ShapeDtypeStruct((t, d), _BF16),
                   jax.ShapeDtypeStruct((t, ROUTE_COLS), jnp.int32), jax.ShapeDtypeStruct((t, ROUTE_COLS), _F32),
                   jax.ShapeDtypeStruct((t // MOE_TILE, SUBLANES, ROUTE_COLS), jnp.int32)],
        scratch_shapes=[
            pltpu.VMEM((R_HEADS, R_QK_DIM, R_V_DIM), _F32),
            pltpu.VMEM((rows, D_A), _F32),
            pltpu.VMEM((rows, D_RV), _F32),
        ],
        compiler_params=_cparams("parallel", "arbitrary"),
        name="token_mix",
    )(tabs["cdec"], p, p, p, p, p, p, p, p, x, gt, tabs["dmat"], tabs["qdf"], tabs["qdb"], tabs["kdf"],
      sb, s0_f, ws, bsb, wpa, wpb, wout, g2.reshape(1, d), shift2, scale2, wr, br)


ROUTE_COLS = LANES
MOE_TILE = 256
MOE_SLOTS = 2 * MOE_TILE
COMBINE_TILES = 2
SLOT_SUBROWS = D_MODEL // LANES


def _route(x, g_ref, sh_ref, sc_ref, wr_ref, br_ref, h_ref, id_ref, wt_ref, cnt_ref):
    h = _modnorm(x, g_ref[...], sh_ref[0], sc_ref[0])
    h_hi = h.astype(_BF16)
    h_ref[...] = h_hi
    h_lo = (h - h_hi.astype(_F32)).astype(_BF16)
    r_hi = _dot(h_hi, wr_ref[...])
    logits = r_hi + pltpu.roll(r_hi, ROUTE_COLS // 2, 1) + _dot(h_lo, wr_ref[...]) + br_ref[...]
    lane = lax.broadcasted_iota(jnp.int32, logits.shape, 1)
    neg = -jnp.inf
    gl = jnp.where(lane < MOE_GROUPS, logits, neg)
    g_top = jnp.max(gl, axis=-1, keepdims=True)
    g_sel = jnp.min(jnp.where(gl == g_top, lane, ROUTE_COLS), axis=-1, keepdims=True)
    lse = g_top + jnp.log(jnp.sum(jnp.exp(gl - g_top), axis=-1, keepdims=True))
    p_group = jnp.exp(g_top - lse)
    lo = MOE_GROUPS + g_sel * EXPERTS_PER_GROUP
    el = jnp.where((lane >= lo) & (lane < lo + EXPERTS_PER_GROUP), logits, neg)
    e1 = jnp.max(el, axis=-1, keepdims=True)
    i1 = jnp.min(jnp.where(el == e1, lane, ROUTE_COLS), axis=-1, keepdims=True)
    el2 = jnp.where(lane == i1, neg, el)
    e2 = jnp.max(el2, axis=-1, keepdims=True)
    i2 = jnp.min(jnp.where(el2 == e2, lane, ROUTE_COLS), axis=-1, keepdims=True)
    x2 = jnp.exp(e2 - e1)
    den = 1.0 + x2
    id1, id2 = i1 - MOE_GROUPS, i2 - MOE_GROUPS
    id_ref[...] = jnp.where(lane == 0, id1, jnp.where(lane == 1, id2, 0))
    wt_ref[...] = jnp.where(lane == 0, p_group * (1.0 / den), jnp.where(lane == 1, p_group * (x2 / den), 0.0))
    hits = jnp.where((lane == id1) | (lane == id2), 1, 0)
    for t in range(cnt_ref.shape[0]):
        tile_hits = jnp.sum(hits[t * MOE_TILE:(t + 1) * MOE_TILE], axis=0, keepdims=True)
        cnt_ref[t] = jnp.broadcast_to(tile_hits, (SUBLANES, ROUTE_COLS))


def _dispatch_plan(cnt):
    n_tiles = cnt.shape[0]
    a = n_tiles * MOE_SLOTS
    tot = jnp.sum(cnt, axis=0)
    padded = (tot + MOE_ROWS - 1) // MOE_ROWS * MOE_ROWS
    pend = jnp.cumsum(padded)
    pstart = pend - padded
    off = pstart[None, :] + jnp.cumsum(cnt, axis=0) - cnt
    n_blocks = (a + N_EXPERTS * (MOE_ROWS - 1) + MOE_ROWS - 1) // MOE_ROWS
    blk_start = jnp.arange(n_blocks, dtype=jnp.int32) * MOE_ROWS
    blk_e = jnp.minimum(jnp.sum((pend[None, :] <= blk_start[:, None]).astype(jnp.int32), axis=1), N_EXPERTS - 1)
    owner = (blk_e[:, None] == jnp.arange(N_EXPERTS, dtype=jnp.int32)[None, :]).astype(jnp.int32)
    used_end = jnp.sum(owner * (pstart + tot)[None, :], axis=1)
    blk_rows = jnp.clip(used_end - blk_start, 0, MOE_ROWS)
    pad_start = jnp.concatenate([pstart + tot, blk_start])
    pad_len = jnp.concatenate([padded - tot, jnp.where(blk_start >= pend[-1], MOE_ROWS, 0)])
    i32 = lambda v: v.astype(jnp.int32)
    return (i32(off).reshape(-1), i32(cnt).reshape(-1), i32(pad_start), i32(pad_len)), i32(blk_e), i32(blk_rows), n_blocks


def _copy_runs(cnt_ref, off_ref, tile, make_copy):
    def body(e, src):
        n = cnt_ref[tile * N_EXPERTS + e]

        @pl.when(n > 0)
        def _():
            make_copy(src, off_ref[tile * N_EXPERTS + e], n).start()
        return src + n
    lax.fori_loop(0, N_EXPERTS, body, 0)


def _slot_rows(first, n):
    s = SLOT_SUBROWS
    return pl.ds(pl.multiple_of(first * s, s), pl.multiple_of(n * s, s))


def _load_slot_major(ref, n_slots):
    return jnp.concatenate([ref[pl.ds(s, n_slots, stride=SLOT_SUBROWS), :] for s in range(SLOT_SUBROWS)], axis=1)


def _store_slot_major(ref, val):
    for s in range(SLOT_SUBROWS):
        ref[pl.ds(s, val.shape[0], stride=SLOT_SUBROWS), :] = val[:, s * LANES:(s + 1) * LANES]


def _wait_slot(stage, sem, s):
    pltpu.make_async_copy(stage.at[s], stage.at[s], sem.at[s]).wait()


def _dispatch_kernel(n_main, off_ref, cnt_ref, pad_start_ref, pad_len_ref, h_ref, id_ref, hc_ref, idc_ref,
                     xs_hbm, pos_ref, stage, sem, zeros, zsem):
    i = pl.program_id(0)
    slot = i % 2
    from_main = i < n_main
    ids = id_ref[...] if idc_ref is None else jnp.where(from_main, id_ref[...], idc_ref[...])
    h = h_ref[...] if hc_ref is None else jnp.where(from_main, h_ref[...], hc_ref[...])

    @pl.when(i == 0)
    def _():
        zeros[...] = jnp.zeros(zeros.shape, zeros.dtype)

        def pad_copy(r):
            n = pad_len_ref[r]
            return n, pltpu.make_async_copy(zeros.at[_slot_rows(0, n)], xs_hbm.at[_slot_rows(pad_start_ref[r], n)], zsem)

        def start(r, carry):
            n, copy = pad_copy(r)

            @pl.when(n > 0)
            def _():
                copy.start()
            return carry

        def wait(r, carry):
            n, copy = pad_copy(r)

            @pl.when(n > 0)
            def _():
                copy.wait()
            return carry

        lax.fori_loop(0, pad_len_ref.shape[0], start, 0)
        lax.fori_loop(0, pad_len_ref.shape[0], wait, 0)

    lane = lax.broadcasted_iota(jnp.int32, ids.shape, 1)
    e0 = jnp.where(lane == ids[:, 0:1], 1.0, 0.0)
    e1 = jnp.where(lane == ids[:, 1:2], 1.0, 0.0)
    tt = lax.broadcasted_iota(jnp.int32, (MOE_TILE, MOE_TILE), 0)
    earlier = jnp.where(lax.broadcasted_iota(jnp.int32, (MOE_TILE, MOE_TILE), 1) < tt, 1.0, 0.0).astype(_BF16)
    c0 = _dot(earlier, e0.astype(_BF16))
    c1 = _dot(earlier, e1.astype(_BF16))
    tot0 = jnp.sum(e0, axis=0, keepdims=True)
    ee = lax.broadcasted_iota(jnp.int32, (ROUTE_COLS, ROUTE_COLS), 0)
    lower = jnp.where(ee < lax.broadcasted_iota(jnp.int32, (ROUTE_COLS, ROUTE_COLS), 1), 1.0, 0.0).astype(_BF16)
    base = jnp.sum(_dot((e0 + e1).astype(_BF16), lower), axis=0, keepdims=True)
    pos0 = jnp.sum(e0 * (base + c0), axis=1, keepdims=True)
    pos1 = jnp.sum(e1 * (base + tot0 + c1), axis=1, keepdims=True)
    pos = jnp.where(lane == 0, pos0, jnp.where(lane == 1, pos1, 0.0))
    pos_ref[...] = pos
    pos_t = pos.T
    slot_id = lax.broadcasted_iota(jnp.int32, (MOE_SLOTS, MOE_TILE), 0).astype(_F32)
    pick = jnp.where((slot_id == pos_t[0:1, :]) | (slot_id == pos_t[1:2, :]), 1.0, 0.0).astype(_BF16)
    rows = _dot(pick, h)

    @pl.when(i >= 2)
    def _():
        _wait_slot(stage, sem, slot)

    _store_slot_major(stage.at[slot], rows)
    _copy_runs(cnt_ref, off_ref, i, lambda src, dst, n: pltpu.make_async_copy(
        stage.at[slot, _slot_rows(src, n)], xs_hbm.at[_slot_rows(dst, n)], sem.at[slot]))

    @pl.when(i == pl.num_programs(0) - 1)
    def _():
        _wait_slot(stage, sem, slot)

        @pl.when(i >= 1)
        def _():
            _wait_slot(stage, sem, 1 - slot)


def _dispatch(routed, routed_c, plan, n_blocks):
    d = routed[0].shape[1]
    n_main = routed[0].shape[0] // MOE_TILE
    n_ctx = 0 if routed_c is None else routed_c[0].shape[0] // MOE_TILE
    n_tiles = n_main + n_ctx
    t = n_tiles * MOE_TILE
    pair = lambda f: [pl.BlockSpec((MOE_TILE, d), lambda i, *_: (f(i), 0)),
                      pl.BlockSpec((MOE_TILE, ROUTE_COLS), lambda i, *_: (f(i), 0))]
    if routed_c is None:
        body = lambda *refs: _dispatch_kernel(n_main, *refs[:6], None, None, *refs[6:])
    else:
        body = functools.partial(_dispatch_kernel, n_main)
    grid_spec = pltpu.PrefetchScalarGridSpec(
        num_scalar_prefetch=4,
        grid=(n_tiles,),
        in_specs=pair(lambda i: jnp.minimum(i, n_main - 1)) + ([] if routed_c is None else
                                                              pair(lambda i: jnp.maximum(i - n_main, 0))),
        out_specs=[pl.BlockSpec(memory_space=pl.ANY), pl.BlockSpec((MOE_TILE, ROUTE_COLS), lambda i, *_: (i, 0))],
        scratch_shapes=[pltpu.VMEM((2, MOE_SLOTS * SLOT_SUBROWS, LANES), _F32), pltpu.SemaphoreType.DMA((2,)),
                        pltpu.VMEM((MOE_ROWS * SLOT_SUBROWS, LANES), _F32), pltpu.SemaphoreType.DMA(())],
    )
    return pl.pallas_call(
        body,
        grid_spec=grid_spec,
        out_shape=[jax.ShapeDtypeStruct((n_blocks * MOE_ROWS * SLOT_SUBROWS, LANES), _F32),
                   jax.ShapeDtypeStruct((t, ROUTE_COLS), _F32)],
        compiler_params=_cparams("arbitrary"),
        name="moe_dispatch",
    )(*plan, *routed, *(routed_c or ()))


def _expert_kernel(blk_e_ref, rows_ref, x_ref, w1_ref, w3_ref, w2_ref, o_ref, w1b, w3b, w2b):
    i = pl.program_id(0)
    rows = rows_ref[i]

    @pl.when((i == 0) | (blk_e_ref[i] != blk_e_ref[jnp.maximum(i - 1, 0)]))
    def _():
        w1b[...] = w1_ref[0, 0].astype(_BF16)
        w3b[...] = w3_ref[0, 0].astype(_BF16)
        w2b[...] = w2_ref[0, 0].astype(_BF16)

    def ffn(first, n_slots):
        span = pl.ds(first * SLOT_SUBROWS, n_slots * SLOT_SUBROWS)
        xb = _load_slot_major(x_ref.at[span], n_slots).astype(_BF16)
        h1 = _dot(xb, w1b[...])
        h3 = _dot(xb, w3b[...])
        act = (h1 * jax.nn.sigmoid(h1) * h3).astype(_BF16)
        _store_slot_major(o_ref.at[span], _dot(act, w2b[...]))

    def clear(first, n_slots):
        o_ref[pl.ds(first * SLOT_SUBROWS, n_slots * SLOT_SUBROWS), :] = jnp.zeros((n_slots * SLOT_SUBROWS, LANES), o_ref.dtype)

    @pl.when(rows > MOE_HALF)
    def _():
        ffn(0, MOE_ROWS)

    @pl.when((rows > 0) & (rows <= MOE_HALF))
    def _():
        ffn(0, MOE_HALF)
        clear(MOE_HALF, MOE_HALF)

    @pl.when(rows == 0)
    def _():
        clear(0, MOE_ROWS)


def _experts(xs, blk_e, blk_rows, layer, w1, w3, w2):
    d = w1.shape[2]
    blk = MOE_ROWS * SLOT_SUBROWS
    n_blocks = xs.shape[0] // blk
    grid_spec = pltpu.PrefetchScalarGridSpec(
        num_scalar_prefetch=2,
        grid=(n_blocks,),
        in_specs=[
            pl.BlockSpec((blk, LANES), lambda i, be, br: (i, 0)),
            pl.BlockSpec((1, 1, d, EXPERT_FF), lambda i, be, br: (layer, be[i], 0, 0)),
            pl.BlockSpec((1, 1, d, EXPERT_FF), lambda i, be, br: (layer, be[i], 0, 0)),
            pl.BlockSpec((1, 1, EXPERT_FF, d), lambda i, be, br: (layer, be[i], 0, 0)),
        ],
        out_specs=pl.BlockSpec((blk, LANES), lambda i, be, br: (i, 0)),
        scratch_shapes=[pltpu.VMEM((d, EXPERT_FF), _BF16), pltpu.VMEM((d, EXPERT_FF), _BF16),
                        pltpu.VMEM((EXPERT_FF, d), _BF16)],
    )
    return pl.pallas_call(
        _expert_kernel,
        grid_spec=grid_spec,
        out_shape=jax.ShapeDtypeStruct(xs.shape, xs.dtype),
        compiler_params=_cparams("arbitrary"),
        name="experts",
    )(blk_e, blk_rows, xs, w1, w3, w2)


def _combine_kernel(final, tile0, off_ref, cnt_ref, x_ref, pos_ref, wt_ref, gt_ref, g_ref, ys_hbm, o_ref, stage, sem):
    i = pl.program_id(0)
    parity = i % 2

    def fetch(step, par):
        for sub in range(COMBINE_TILES):
            s = par * COMBINE_TILES + sub
            _copy_runs(cnt_ref, off_ref, tile0 + step * COMBINE_TILES + sub, lambda src, dst, n, s=s: pltpu.make_async_copy(
                ys_hbm.at[_slot_rows(dst, n)], stage.at[s, _slot_rows(src, n)], sem.at[s]))

    @pl.when(i == 0)
    def _():
        fetch(0, 0)

    @pl.when(i + 1 < pl.num_programs(0))
    def _():
        fetch(i + 1, 1 - parity)

    slot_id = lax.broadcasted_iota(jnp.int32, (MOE_TILE, MOE_SLOTS), 1).astype(_F32)
    for sub in range(COMBINE_TILES):
        s = parity * COMBINE_TILES + sub
        rows = slice(sub * MOE_TILE, (sub + 1) * MOE_TILE)
        _wait_slot(stage, sem, s)
        y = _load_slot_major(stage.at[s], MOE_SLOTS).astype(_BF16)
        mix = (jnp.where(slot_id == pos_ref[rows, 0:1], wt_ref[rows, 0:1], 0.0)
               + jnp.where(slot_id == pos_ref[rows, 1:2], wt_ref[rows, 1:2], 0.0)).astype(_BF16)
        x = x_ref[rows, :] + gt_ref[0] * _dot(mix, y)
        if final:
            x = (x * lax.rsqrt(jnp.mean(x * x, axis=-1, keepdims=True) + EPS)) * g_ref[...]
        o_ref[rows, :] = x


def _combine(x, ys, pos, wts, off, cnt, gt, g_final, tok0, final):
    b, n, d = x.shape
    t = b * n
    group = COMBINE_TILES * MOE_TILE
    tile0 = tok0 // MOE_TILE
    row = pl.BlockSpec((group, d), lambda i, o, c: (i, 0))
    lanes = lambda first: pl.BlockSpec((group, ROUTE_COLS), lambda i, o, c: (first + i, 0))
    n_slots = 2 * COMBINE_TILES
    grid_spec = pltpu.PrefetchScalarGridSpec(
        num_scalar_prefetch=2,
        grid=(t // group,),
        in_specs=[row, lanes(tok0 // group), lanes(0), pl.BlockSpec((1, 1, d), lambda i, o, c: (i * group // n, 0, 0)),
                  pl.BlockSpec((1, d), lambda i, o, c: (0, 0)), pl.BlockSpec(memory_space=pl.ANY)],
        out_specs=row,
        scratch_shapes=[pltpu.VMEM((n_slots, MOE_SLOTS * SLOT_SUBROWS, LANES), _F32), pltpu.SemaphoreType.DMA((n_slots,))],
    )
    out = pl.pallas_call(
        functools.partial(_combine_kernel, final, tile0),
        grid_spec=grid_spec,
        out_shape=jax.ShapeDtypeStruct((t, d), _F32),
        compiler_params=_cparams("arbitrary"),
        name="moe_combine",
    )(off, cnt, x.reshape(t, d), pos, wts, gt, g_final.reshape(1, d), ys)
    return out.reshape(b, n, d)


def _rotary_tables(n):
    n_freq = R_QK_DIM // 4
    inv = ROPE_BASE ** (-jnp.arange(n_freq, dtype=_F32) / n_freq)
    rows = n // GRID_W
    row = jnp.repeat(jnp.arange(rows, dtype=_F32), GRID_W)
    col = jnp.tile(jnp.arange(GRID_W, dtype=_F32), rows)
    ang = jnp.concatenate([row[:, None] * inv, col[:, None] * inv], axis=-1)
    cos, sin = jnp.cos(ang), jnp.sin(ang)
    return jnp.repeat(cos, 2, axis=-1), jnp.stack([-sin, sin], axis=-1).reshape(n, R_QK_DIM)


def _decay_tables(decay_logit, l_ctx):
    log_g = jax.nn.log_sigmoid(decay_logit.astype(_F32))
    lgf, lgb = log_g[0][:, None], log_g[1][:, None]
    j = jnp.arange(CHUNK, dtype=_F32)
    diff = j[:, None] - j[None, :]
    dmat = jnp.where(diff[None] >= 0, jnp.exp(lgf[:, :, None] * jnp.maximum(diff, 0.0)[None]),
                     jnp.exp(lgb[:, :, None] * jnp.maximum(-diff, 0.0)[None]))
    lanes = lambda t: jnp.broadcast_to(t[:, :, None], t.shape + (LANES,))
    m = jnp.arange(l_ctx, dtype=_F32)
    return {
        "dmat": dmat,
        "qdf": lanes(jnp.exp(lgf * (j + 1.0))), "qdb": lanes(jnp.exp(lgb * (CHUNK - j))),
        "kdf": lanes(jnp.exp(lgf * (CHUNK - 1.0 - j))), "kdb": lanes(jnp.exp(lgb * j)),
        "cdec": jnp.exp(log_g * CHUNK),
        "ctx_f": lanes(jnp.exp(lgf * ((l_ctx - 1.0) - m))), "ctx_b": lanes(jnp.exp(lgb * m)),
    }


def kernel(x, c, ctx, c_ctx, w_ada, b_ada, g_mix, g_ffn, w_in, w_s, b_s, g_sgu, decay_logit, w_pa, w_pb, w_out,
           w_group, b_group, w_erouter, b_erouter, w1, w3, w2, g_final):
    bsz, n, d = x.shape
    l_ctx = ctx.shape[1]
    t_main = bsz * n

    cond = jnp.zeros((COND_ROWS, d), _F32).at[:bsz].set(c).at[bsz].set(c_ctx)
    mods = _adaln(cond, w_ada, b_ada)
    rot = _rotary_tables(n)
    no_rot = (jnp.ones((l_ctx, LANES), _F32), jnp.zeros((l_ctx, LANES), _F32))
    zero_state = jnp.zeros((bsz, R_HEADS, R_QK_DIM, R_V_DIM), _F32)

    w_in_b = w_in.astype(_BF16)
    xc = ctx
    for l in range(DEPTH):
        update_ctx = l < DEPTH - 1
        m6 = mods[l].reshape(COND_ROWS, 6, d)
        main = [m6[:bsz, i][:, None, :] for i in range(6)]
        cmod = [jnp.broadcast_to(m6[bsz, i][None, None, :], (bsz, 1, d)) for i in range(6)]
        tabs = _decay_tables(decay_logit[l], l_ctx)
        gsgu = g_sgu[l].reshape(1, D_A)
        mix_w = (w_s[l].astype(_BF16), jnp.broadcast_to(b_s[l][:, :, None], (A_GROUPS, CHUNK, LANES)),
                 w_pa[l].astype(_BF16), w_pb[l].astype(_BF16), w_out[l].astype(_BF16))

        p = _proj_in(x, g_mix[l], main[0], main[1], gsgu, *rot, w_in_b, l, _MIX_SEGMENTS)
        if update_ctx:
            pc = _proj_in(xc, g_mix[l], cmod[0], cmod[1], gsgu, *no_rot, w_in_b, l, _MIX_SEGMENTS)
            s_ctx_f, s_ctx_b = _ctx_states(pc, _K_BLK, _VR_BLK, tabs["ctx_f"], tabs["ctx_b"])
        else:
            w_vk = jnp.concatenate([w_in_b[l:l + 1, :, KV_START + D_RQK:KV_END],
                                    w_in_b[l:l + 1, :, KV_START:KV_START + D_RQK]], axis=2)
            vk_c = _proj_in(xc, g_mix[l], cmod[0], cmod[1], gsgu, *no_rot, w_vk, 0, _VK_SEGMENTS)
            s_ctx_f, s_ctx_b = _ctx_states(vk_c, D_RV // D_RQK, 0, tabs["ctx_f"], tabs["ctx_b"])
        sb = _bwd_states(p, tabs["kdb"], tabs["cdec"], s_ctx_b)
        wr = jnp.zeros((d, ROUTE_COLS // 2), _F32).at[:, :MOE_GROUPS].set(w_group[l])
        wr = wr.at[:, MOE_GROUPS:MOE_GROUPS + N_EXPERTS].set(w_erouter[l])
        wr_hi = wr.astype(_BF16)
        wr = jnp.concatenate([wr_hi, (wr - wr_hi.astype(_F32)).astype(_BF16)], axis=1)
        br = jnp.zeros((1, ROUTE_COLS), _F32).at[0, :MOE_GROUPS].set(b_group[l])
        br = br.at[0, MOE_GROUPS:MOE_GROUPS + N_EXPERTS].set(b_erouter[l])
        route_w = (g_ffn[l], wr, br)
        x, h2, ids, wts, cnt = _token_mix(p, x, main[2], tabs, sb, s_ctx_f, *mix_w, route_w[0], main[3], main[4],
                                          *route_w[1:])
        routed_c = None
        if update_ctx:
            sbc = _bwd_states(pc, tabs["kdb"], tabs["cdec"], zero_state)
            xc, h2c, idsc, wtsc, cntc = _token_mix(pc, xc, cmod[2], tabs, sbc, zero_state, *mix_w, route_w[0],
                                                   cmod[3], cmod[4], *route_w[1:])
            routed_c = (h2c, idsc)
            cnt = jnp.concatenate([cnt, cntc], axis=0)

        plan, blk_e, blk_rows, n_blocks = _dispatch_plan(cnt[:, 0, :N_EXPERTS])
        off, cnt = plan[:2]
        xs, pos = _dispatch((h2, ids), routed_c, plan, n_blocks)
        ys = _experts(xs, blk_e, blk_rows, l, w1, w3, w2)
        final = l == DEPTH - 1
        x = _combine(x, ys, pos, wts, off, cnt, main[5], g_final, 0, final)
        if update_ctx:
            xc = _combine(xc, ys, pos, wtsc, off, cnt, cmod[5], g_final, t_main, False)
    return x
```
